```python
import math
import jax, jax.numpy as jnp
from jax import lax
import numpy as np

D_MODEL = 1024
BATCH = 8
SEQ = 2048
DEPTH = 2

N_HEADS = 16
HEAD_DIM = D_MODEL // N_HEADS
Q_BLOCK = 128
MOBA_BLOCK = 256
MOBA_TOPK = 3
N_BUCKETS = 32
MAX_DISTANCE = 128
D_FF = 2816
N_EXPERTS = 8
TOP_K = 2
D_FF_EXPERT = 3584
N_DENSE = (DEPTH + 1) // 2
N_MOE = DEPTH // 2
RMS_EPS = 1e-6
NEG_BIG = -1e30

kernel_name = 'hybrid_stickbreak_moba_moe_block'


def rmsnorm(x, g):
    xf = x.astype(jnp.float32)
    y = xf * lax.rsqrt(jnp.mean(xf * xf, axis=-1, keepdims=True) + RMS_EPS)
    return (y * g.astype(jnp.float32)).astype(x.dtype)


def t5_bucket(dist):
    n = jnp.maximum(dist, 0)
    max_exact = N_BUCKETS // 2
    nf = jnp.maximum(n, 1).astype(jnp.float32)
    large = max_exact + (jnp.log(nf / max_exact) / math.log(MAX_DISTANCE / max_exact)
                         * (N_BUCKETS - max_exact)).astype(jnp.int32)
    large = jnp.minimum(large, N_BUCKETS - 1)
    return jnp.where(n < max_exact, n, large)


def qkv_heads(h, w_qkv):
    b, s, d = h.shape
    qkv = jnp.einsum('bsd,de->bse', h, w_qkv)
    q, k, v = jnp.split(qkv, 3, axis=-1)
    to_heads = lambda t: t.reshape(b, s, N_HEADS, HEAD_DIM).transpose(0, 2, 1, 3)
    return to_heads(q), to_heads(k), to_heads(v)


def stick_breaking_attention(q, k, v):
    b, nh, s, dh = q.shape
    scale = dh ** -0.5
    k_pos = jnp.arange(s)

    def block(c):
        q0 = c * Q_BLOCK
        qc = lax.dynamic_slice_in_dim(q, q0, Q_BLOCK, axis=2)
        z = jnp.einsum('bhqd,bhkd->bhqk', qc, k).astype(jnp.float32) * scale
        q_pos = q0 + jnp.arange(Q_BLOCK)
        past = k_pos[None, :] < q_pos[:, None]
        log_one_minus = jnp.where(past, jax.nn.log_sigmoid(-z), 0.0)
        after = lax.cumsum(log_one_minus, axis=3, reverse=True) - log_one_minus
        w = jnp.where(past, jnp.exp(jax.nn.log_sigmoid(z) + after), 0.0)
        return jnp.einsum('bhqk,bhkd->bhqd', w.astype(v.dtype), v)

    out = lax.map(block, jnp.arange(s // Q_BLOCK))
    return out.transpose(1, 2, 0, 3, 4).reshape(b, nh, s, dh)


def moba_attention(q, k, v, rel_bias):
    b, nh, s, dh = q.shape
    scale = dh ** -0.5
    n_blk = -(-s // MOBA_BLOCK)
    s_pad = n_blk * MOBA_BLOCK
    pad = ((0, 0), (0, 0), (0, s_pad - s), (0, 0))
    kb = jnp.pad(k, pad).reshape(b, nh, n_blk, MOBA_BLOCK, dh)
    vb = jnp.pad(v, pad).reshape(b, nh, n_blk, MOBA_BLOCK, dh)
    k_mean = jnp.mean(kb, axis=3)
    n_sel = min(MOBA_TOPK, n_blk - 1)
    bias_hb = rel_bias.T
    in_blk = jnp.arange(MOBA_BLOCK)
    bi = jnp.arange(b)[:, None, None, None]
    hi = jnp.arange(nh)[None, :, None, None]

    def block(c):
        q0 = c * Q_BLOCK
        own = q0 // MOBA_BLOCK
        qc = lax.dynamic_slice_in_dim(q, q0, Q_BLOCK, axis=2)
        q_pos = q0 + jnp.arange(Q_BLOCK)
        k_own = lax.dynamic_index_in_dim(kb, own, axis=2, keepdims=False)
        v_own = lax.dynamic_index_in_dim(vb, own, axis=2, keepdims=False)
        dist_own = q_pos[:, None] - (own * MOBA_BLOCK + in_blk)[None, :]
        logit_own = (jnp.einsum('bhqd,bhkd->bhqk', qc, k_own).astype(jnp.float32) * scale
                     + bias_hb[:, t5_bucket(dist_own)])
        logit_own = jnp.where(dist_own >= 0, logit_own, NEG_BIG)
        if n_sel == 0:
            p_own = jax.nn.softmax(logit_own, axis=-1).astype(v.dtype)
            return jnp.einsum('bhqk,bhkd->bhqd', p_own, v_own)
        gate = jnp.einsum('bhqd,bhnd->bhqn', qc, k_mean).astype(jnp.float32)
        gate = jnp.where(jnp.arange(n_blk) < own, gate, NEG_BIG)
        _, idx = lax.top_k(gate, n_sel)
        valid = idx < own
        k_sel = kb[bi, hi, idx]
        v_sel = vb[bi, hi, idx]
        dist_sel = q_pos[:, None, None] - (idx[..., None] * MOBA_BLOCK + in_blk)
        logit_sel = (jnp.einsum('bhqd,bhqnkd->bhqnk', qc, k_sel).astype(jnp.float32) * scale
                     + bias_hb[hi[..., None], t5_bucket(dist_sel)])
        logit_sel = jnp.where(valid[..., None], logit_sel, NEG_BIG)
        n_s = n_sel * MOBA_BLOCK
        logits = jnp.concatenate([logit_sel.reshape(b, nh, Q_BLOCK, n_s), logit_own], axis=-1)
        p = jax.nn.softmax(logits, axis=-1).astype(v.dtype)
        p_sel = p[..., :n_s].reshape(b, nh, Q_BLOCK, n_sel, MOBA_BLOCK)
        p_own = p[..., n_s:]
        return (jnp.einsum('bhqnk,bhqnkd->bhqd', p_sel, v_sel)
                + jnp.einsum('bhqk,bhkd->bhqd', p_own, v_own))

    out = lax.map(block, jnp.arange(s // Q_BLOCK))
    return out.transpose(1, 2, 0, 3, 4).reshape(b, nh, s, dh)


def swiglu(h, w1, w3, w2):
    hid = jax.nn.silu(jnp.einsum('bsd,df->bsf', h, w1)) * jnp.einsum('bsd,df->bsf', h, w3)
    return jnp.einsum('bsf,fd->bsd', hid, w2)


def moe_swiglu(h, w_router, w1, w3, w2):
    b, s, d = h.shape
    t = h.reshape(b * s, d)
    logits = jnp.dot(t, w_router).astype(jnp.float32)
    top_val, top_idx = lax.top_k(logits, TOP_K)
    top_w = jax.nn.softmax(top_val, axis=-1)
    gates = jnp.einsum('tk,tke->te', top_w,
                       jax.nn.one_hot(top_idx, N_EXPERTS, dtype=jnp.float32)).astype(h.dtype)

    def expert(args):
        e_w1, e_w3, e_w2, g = args
        hid = jax.nn.silu(t @ e_w1) * (t @ e_w3)
        return (hid @ e_w2) * g[:, None]

    y = jnp.sum(lax.map(expert, (w1, w3, w2, gates.T)), axis=0)
    return y.reshape(b, s, d)


def setup_inputs(seed: int = 0) -> dict:
    key = jax.random.key(seed)
    ks = jax.random.split(key, 14)
    nrm = lambda k, shape, fan_in: jax.random.normal(k, shape, jnp.float32) * fan_in ** -0.5
    x = jax.random.normal(ks[0], (BATCH, SEQ, D_MODEL), jnp.float32)
    w_qkv = nrm(ks[1], (DEPTH, D_MODEL, 3 * D_MODEL), D_MODEL)
    w_o = nrm(ks[2], (DEPTH, D_MODEL, D_MODEL), D_MODEL)
    mixer_norm = 1.0 + 0.02 * jax.random.normal(ks[3], (DEPTH, D_MODEL), jnp.float32)
    ffn_norm = 1.0 + 0.02 * jax.random.normal(ks[4], (DEPTH, D_MODEL), jnp.float32)
    rel_bias = 0.5 * jax.random.normal(ks[5], (N_BUCKETS, N_HEADS), jnp.float32)
    w1 = nrm(ks[6], (N_DENSE, D_MODEL, D_FF), D_MODEL)
    w3 = nrm(ks[7], (N_DENSE, D_MODEL, D_FF), D_MODEL)
    w2 = nrm(ks[8], (N_DENSE, D_FF, D_MODEL), D_FF)
    router = nrm(ks[9], (N_MOE, D_MODEL, N_EXPERTS), D_MODEL)
    e_w1 = nrm(ks[10], (N_MOE, N_EXPERTS, D_MODEL, D_FF_EXPERT), D_MODEL)
    e_w3 = nrm(ks[11], (N_MOE, N_EXPERTS, D_MODEL, D_FF_EXPERT), D_MODEL)
    e_w2 = nrm(ks[12], (N_MOE, N_EXPERTS, D_FF_EXPERT, D_MODEL), D_FF_EXPERT)
    final_norm = 1.0 + 0.02 * jax.random.normal(ks[13], (D_MODEL,), jnp.float32)
    return {'x': x, 'w_qkv': w_qkv, 'w_o': w_o, 'mixer_norm': mixer_norm,
            'ffn_norm': ffn_norm, 'rel_bias': rel_bias, 'w1': w1, 'w3': w3, 'w2': w2,
            'router': router, 'e_w1': e_w1, 'e_w3': e_w3, 'e_w2': e_w2,
            'final_norm': final_norm}


def reference(x, w_qkv, w_o, mixer_norm, ffn_norm, rel_bias, w1, w3, w2,
              router, e_w1, e_w3, e_w2, final_norm):
    b, s, d = x.shape
    h = x
    for i in range(DEPTH):
        hn = rmsnorm(h, mixer_norm[i])
        q, k, v = qkv_heads(hn, w_qkv[i])
        if i % 2 == 0:
            o = stick_breaking_attention(q, k, v)
        else:
            o = moba_attention(q, k, v, rel_bias)
        o = o.transpose(0, 2, 1, 3).reshape(b, s, d)
        h = h + jnp.einsum('bsd,de->bse', o, w_o[i])
        hn = rmsnorm(h, ffn_norm[i])
        j = i // 2
        if i % 2 == 0:
            h = h + swiglu(hn, w1[j], w3[j], w2[j])
        else:
            h = h + moe_swiglu(hn, router[j], e_w1[j], e_w3[j], e_w2[j])
    return rmsnorm(h, final_norm)
```

```python
import functools
import math

import jax
import jax.numpy as jnp
from jax import lax
from jax.experimental import pallas as pl
from jax.experimental.pallas import tpu as pltpu

N_HEADS = 16
HEAD_DIM = 64
MOBA_BLOCK = 256
MOBA_TOPK = 3
N_BUCKETS = 32
MAX_DISTANCE = 128
N_EXPERTS = 8
RMS_EPS = 1e-6
NEG_BIG = -1e30

V7X_LANES = 128
SUBLANES = 8
V7X_VMEM_BYTES = 64 * 1024 * 1024
HEADS_PER_LANE_BLOCK = V7X_LANES // HEAD_DIM

F32 = jnp.float32
BF16 = jnp.bfloat16


def _cparams(semantics, vmem_mb):
    assert vmem_mb * 1024 * 1024 < V7X_VMEM_BYTES
    return pltpu.CompilerParams(dimension_semantics=semantics,
                                vmem_limit_bytes=vmem_mb * 1024 * 1024)


def _rmsnorm_f32(x, g):
    return x * lax.rsqrt(jnp.mean(x * x, axis=-1, keepdims=True) + RMS_EPS) * g


def _silu(a):
    return a * (1.0 / (1.0 + jnp.exp(-a)))


def _dot(a, b):
    return jnp.dot(a, b, preferred_element_type=F32)


def _dot_nt(a, b):
    return lax.dot_general(a, b, (((1,), (1,)), ((), ())), preferred_element_type=F32)


def _qkv_kernel(x_ref, g_ref, w_ref, o_ref, hn_ref):
    j = pl.program_id(1)

    @pl.when(j == 0)
    def _():
        hn_ref[...] = _rmsnorm_f32(x_ref[...], g_ref[...]).astype(BF16)

    scale = jnp.where(j == 0, HEAD_DIM ** -0.5, 1.0)
    o_ref[0] = (_dot(hn_ref[...], w_ref[...]) * scale).astype(o_ref.dtype)


def qkv_proj(h, g, w_bf16, tm):
    t, d = h.shape
    return pl.pallas_call(
        _qkv_kernel,
        grid=(t // tm, 3),
        in_specs=[pl.BlockSpec((tm, d), lambda i, j: (i, 0)),
                  pl.BlockSpec((1, d), lambda i, j: (0, 0)),
                  pl.BlockSpec((d, d), lambda i, j: (0, j))],
        out_specs=pl.BlockSpec((1, tm, d), lambda i, j: (j, i, 0)),
        out_shape=jax.ShapeDtypeStruct((3, t, d), BF16),
        scratch_shapes=[pltpu.VMEM((tm, d), BF16)],
        compiler_params=_cparams(("arbitrary", "arbitrary"), 40),
        name="qkv_proj",
    )(h, g.reshape(1, d), w_bf16)


def _sb_kernel(q_ref, k_ref, v_ref, o_ref, acc_ref, r_ref, u_ref, *, tq):
    qi = pl.program_id(2)
    row = lax.broadcasted_iota(jnp.int32, (tq, tq), 0)
    col = lax.broadcasted_iota(jnp.int32, (tq, tq), 1)

    @pl.when(qi == 0)
    def _():
        u_ref[...] = (row > col).astype(BF16)

    q = q_ref[0, 0]
    lane = lax.broadcasted_iota(jnp.int32, (tq, V7X_LANES), 1)
    past = col < row

    def block(kb, h, qm, diagonal):
        start = pl.multiple_of(kb * tq, tq)
        ks = k_ref[0, 0, pl.ds(start, tq), :]
        vs = v_ref[0, 0, pl.ds(start, tq), :]
        z = _dot_nt(qm, ks)
        log_not = -(jnp.maximum(z, 0.0) + jnp.log1p(jnp.exp(-jnp.abs(z))))
        log_sig = z + log_not
        if diagonal:
            log_not = jnp.where(past, log_not, 0.0)
        after = _dot(log_not.astype(BF16), u_ref[...]) + r_ref[h]
        w = jnp.exp(log_sig + after)
        if diagonal:
            w = jnp.where(past, w, 0.0)
        r_ref[h] = after[:, 0:1] + log_not[:, 0:1]
        acc_ref[h] += _dot(w.astype(BF16), vs)

    for h in range(HEADS_PER_LANE_BLOCK):
        qm = jnp.where(lane // HEAD_DIM == h, q, jnp.zeros_like(q))
        r_ref[h] = jnp.zeros((tq, 1), F32)
        acc_ref[h] = jnp.zeros((tq, V7X_LANES), F32)
        block(qi, h, qm, True)

        def body(s, carry, h=h, qm=qm):
            block(qi - 1 - s, h, qm, False)
            return carry

        lax.fori_loop(0, qi, body, 0)

    o_ref[0] = jnp.where(lane // HEAD_DIM == 0, acc_ref[0], acc_ref[1]).astype(o_ref.dtype)


def stick_breaking_attention(qkv, b, s, tq):
    d = qkv.shape[-1]
    qkv4 = qkv.reshape(3, b, s, d)
    return pl.pallas_call(
        functools.partial(_sb_kernel, tq=tq),
        grid=(b, d // V7X_LANES, s // tq),
        in_specs=[pl.BlockSpec((1, 1, tq, V7X_LANES), lambda bi, hp, qi: (0, bi, qi, hp)),
                  pl.BlockSpec((1, 1, s, V7X_LANES), lambda bi, hp, qi: (1, bi, 0, hp)),
                  pl.BlockSpec((1, 1, s, V7X_LANES), lambda bi, hp, qi: (2, bi, 0, hp))],
        out_specs=pl.BlockSpec((1, tq, V7X_LANES), lambda bi, hp, qi: (bi, qi, hp)),
        out_shape=jax.ShapeDtypeStruct((b, s, d), BF16),
        scratch_shapes=[pltpu.VMEM((HEADS_PER_LANE_BLOCK, tq, V7X_LANES), F32),
                        pltpu.VMEM((HEADS_PER_LANE_BLOCK, tq, 1), F32),
                        pltpu.VMEM((tq, tq), BF16)],
        compiler_params=_cparams(("arbitrary", "arbitrary", "arbitrary"), 32),
        name="stick_breaking",
    )(qkv4, qkv4, qkv4)


def _t5_bias_kernel(rb_ref, o_ref, *, tq):
    h = pl.program_id(0)
    row = lax.broadcasted_iota(jnp.int32, (tq, tq), 0)
    col = lax.broadcasted_iota(jnp.int32, (tq, tq), 1)
    max_exact = N_BUCKETS // 2
    for o in range(2):
        dist = o * tq + row - col
        n = jnp.maximum(dist, 0)
        nf = jnp.maximum(n, 1).astype(F32)
        large = max_exact + (jnp.log(nf / max_exact) / math.log(MAX_DISTANCE / max_exact)
                             * (N_BUCKETS - max_exact)).astype(jnp.int32)
        large = jnp.minimum(large, N_BUCKETS - 1)
        bucket = jnp.where(n < max_exact, n, large)
        bias = jnp.zeros((tq, tq), F32)
        for bkt in range(N_BUCKETS):
            bias = jnp.where(bucket == bkt, rb_ref[bkt, h], bias)
        if o == 0:
            bias = jnp.where(dist >= 0, bias, NEG_BIG)
        o_ref[0, o] = bias


def t5_bias_tiles(rel_bias, tq):
    return pl.pallas_call(
        functools.partial(_t5_bias_kernel, tq=tq),
        grid=(N_HEADS,),
        in_specs=[pl.BlockSpec(memory_space=pltpu.SMEM)],
        out_specs=pl.BlockSpec((1, 2, tq, tq), lambda h: (h, 0, 0, 0)),
        out_shape=jax.ShapeDtypeStruct((N_HEADS, 2, tq, tq), F32),
        compiler_params=_cparams(("arbitrary",), 16),
        name="t5_bias_tiles",
    )(rel_bias)


def _moba_kernel(rb_ref, q_ref, k_ref, v_ref, bt_ref, o_ref, m_ref, l_ref, acc_ref, km_ref, *, tq, nblk):
    hp = pl.program_id(1)
    qi = pl.program_id(2)

    @pl.when(qi == 0)
    def _():
        for n in range(nblk):
            kb = k_ref[0, 0, n * tq:(n + 1) * tq, :].astype(F32)
            km_ref[n:n + 1, :] = jnp.mean(kb, axis=0, keepdims=True)

    q = q_ref[0, 0]
    lane = lax.broadcasted_iota(jnp.int32, (tq, V7X_LANES), 1)
    blk = lax.broadcasted_iota(jnp.int32, (tq, nblk), 1)
    km = km_ref[...]
    km_hi = km.astype(BF16)
    km_lo = (km - km_hi.astype(F32)).astype(BF16)
    n_sel = min(MOBA_TOPK, nblk - 1)

    for h in range(HEADS_PER_LANE_BLOCK):
        qm = jnp.where(lane // HEAD_DIM == h, q, jnp.zeros_like(q))
        head = hp * HEADS_PER_LANE_BLOCK + h

        gate = _dot_nt(qm, km_hi) + _dot_nt(qm, km_lo)
        gate = jnp.where(blk < qi, gate, NEG_BIG)
        sel = []
        for n in range(nblk):
            gn = gate[:, n:n + 1]
            beats = (gate > gn) | ((gate == gn) & (blk < n))
            rank = jnp.sum(beats.astype(F32), axis=1, keepdims=True)
            sel.append((rank < n_sel) & (n < qi))

        def update(s, vs, first, h=h):
            if first:
                m_new = jnp.max(s, axis=1, keepdims=True)
                p = jnp.exp(s - m_new)
                l_ref[h] = jnp.sum(p, axis=1, keepdims=True)
                acc_ref[h] = _dot(p.astype(BF16), vs)
            else:
                m_old = m_ref[h]
                m_new = jnp.maximum(m_old, jnp.max(s, axis=1, keepdims=True))
                alpha = jnp.exp(m_old - m_new)
                p = jnp.exp(s - m_new)
                l_ref[h] = alpha * l_ref[h] + jnp.sum(p, axis=1, keepdims=True)
                acc_ref[h] = alpha * acc_ref[h] + _dot(p.astype(BF16), vs)
            m_ref[h] = m_new

        own = pl.multiple_of(qi * tq, tq)
        s_own = _dot_nt(qm, k_ref[0, 0, pl.ds(own, tq), :]) + bt_ref[h, 0]
        update(s_own, v_ref[0, 0, pl.ds(own, tq), :], True)

        @pl.when(qi >= 1)
        def _(h=h, qm=qm, sel=sel):
            prev = pl.multiple_of((qi - 1) * tq, tq)
            s = _dot_nt(qm, k_ref[0, 0, pl.ds(prev, tq), :]) + bt_ref[h, 1]
            chosen = sel[0] & (qi - 1 == 0)
            for n in range(1, nblk):
                chosen = chosen | (sel[n] & (qi - 1 == n))
            update(jnp.where(chosen, s, NEG_BIG), v_ref[0, 0, pl.ds(prev, tq), :], False)

        far_bias = rb_ref[N_BUCKETS - 1, head]
        for n in range(nblk - 2):
            @pl.when(n <= qi - 2)
            def _(n=n, h=h, qm=qm, sel=sel, far_bias=far_bias):
                s = _dot_nt(qm, k_ref[0, 0, n * tq:(n + 1) * tq, :]) + far_bias
                update(jnp.where(sel[n], s, NEG_BIG), v_ref[0, 0, n * tq:(n + 1) * tq, :], False)

    out0 = acc_ref[0] / l_ref[0]
    out1 = acc_ref[1] / l_ref[1]
    o_ref[0] = jnp.where(lane // HEAD_DIM == 0, out0, out1).astype(o_ref.dtype)


def moba_attention(qkv, rel_bias, b, s):
    d = qkv.shape[-1]
    tq = MOBA_BLOCK
    assert s % tq == 0 and tq >= 2 * MAX_DISTANCE
    nblk = s // tq
    qkv4 = qkv.reshape(3, b, s, d)
    tiles = t5_bias_tiles(rel_bias, tq)
    return pl.pallas_call(
        functools.partial(_moba_kernel, tq=tq, nblk=nblk),
        grid=(b, d // V7X_LANES, nblk),
        in_specs=[pl.BlockSpec(memory_space=pltpu.SMEM),
                  pl.BlockSpec((1, 1, tq, V7X_LANES), lambda bi, hp, qi: (0, bi, qi, hp)),
                  pl.BlockSpec((1, 1, s, V7X_LANES), lambda bi, hp, qi: (1, bi, 0, hp)),
                  pl.BlockSpec((1, 1, s, V7X_LANES), lambda bi, hp, qi: (2, bi, 0, hp)),
                  pl.BlockSpec((HEADS_PER_LANE_BLOCK, 2, tq, tq), lambda bi, hp, qi: (hp, 0, 0, 0))],
        out_specs=pl.BlockSpec((1, tq, V7X_LANES), lambda bi, hp, qi: (bi, qi, hp)),
        out_shape=jax.ShapeDtypeStruct((b, s, d), BF16),
        scratch_shapes=[pltpu.VMEM((HEADS_PER_LANE_BLOCK, tq, 1), F32),
                        pltpu.VMEM((HEADS_PER_LANE_BLOCK, tq, 1), F32),
                        pltpu.VMEM((HEADS_PER_LANE_BLOCK, tq, V7X_LANES), F32),
                        pltpu.VMEM((nblk, V7X_LANES), F32)],
        compiler_params=_cparams(("arbitrary", "arbitrary", "arbitrary"), 32),
        name="moba",
    )(rel_bias, qkv4, qkv4, qkv4, tiles)


def _oproj_kernel(o_ref, w_ref, h_ref, out_ref):
    out_ref[...] = h_ref[...] + _dot(o_ref[...], w_ref[...])


def out_proj(o, w_bf16, h, tm):
    t, d = h.shape
    return pl.pallas_call(
        _oproj_kernel,
        grid=(t // tm,),
        in_specs=[pl.BlockSpec((tm, d), lambda i: (i, 0)),
                  pl.BlockSpec((d, d), lambda i: (0, 0)),
                  pl.BlockSpec((tm, d), lambda i: (i, 0))],
        out_specs=pl.BlockSpec((tm, d), lambda i: (i, 0)),
        out_shape=jax.ShapeDtypeStruct((t, d), F32),
        compiler_params=_cparams(("arbitrary",), 40),
        name="out_proj",
    )(o, w_bf16, h)


def _ffn_kernel(h_ref, g_ref, w1_ref, w3_ref, w2_ref, out_ref, *, chunks):
    x = h_ref[...]
    hn = _rmsnorm_f32(x, g_ref[...]).astype(BF16)
    acc = x
    for c0, c1 in chunks:
        a = _dot(hn, w1_ref[:, c0:c1])
        b = _dot(hn, w3_ref[:, c0:c1])
        acc = acc + _dot((_silu(a) * b).astype(BF16), w2_ref[c0:c1, :])
    out_ref[...] = acc


def dense_ffn(h, g, w1, w3, w2, tm, chunk):
    t, d = h.shape
    f = w1.shape[1]
    chunks = tuple((c, min(c + chunk, f)) for c in range(0, f, chunk))
    whole = lambda shape: pl.BlockSpec(shape, lambda i: (0, 0), pipeline_mode=pl.Buffered(1))
    return pl.pallas_call(
        functools.partial(_ffn_kernel, chunks=chunks),
        grid=(t // tm,),
        in_specs=[pl.BlockSpec((tm, d), lambda i: (i, 0)),
                  pl.BlockSpec((1, d), lambda i: (0, 0)),
                  whole((d, f)), whole((d, f)), whole((f, d))],
        out_specs=pl.BlockSpec((tm, d), lambda i: (i, 0)),
        out_shape=jax.ShapeDtypeStruct((t, d), F32),
        compiler_params=_cparams(("arbitrary",), 56),
        name="dense_ffn",
    )(h, g.reshape(1, d), w1, w3, w2)


COL_E0, COL_E1, COL_W0, COL_W1, COL_R0, COL_R1 = range(6)


def _router_kernel(h_ref, g_ref, wr_ref, hn_ref, slab_ref, cnt_ref, carry_ref, *, tm):
    i = pl.program_id(0)

    @pl.when(i == 0)
    def _():
        carry_ref[...] = jnp.zeros_like(carry_ref)

    hn = _rmsnorm_f32(h_ref[...], g_ref[...])
    hn_ref[...] = hn
    hi = hn.astype(BF16)
    lo = (hn - hi.astype(F32)).astype(BF16)
    w = wr_ref[...]
    w_hi = w.astype(BF16)
    w_lo = (w - w_hi.astype(F32)).astype(BF16)
    logits = _dot(hi, w_hi) + _dot(hi, w_lo) + _dot(lo, w_hi)

    lane = lax.broadcasted_iota(jnp.int32, (tm, V7X_LANES), 1)
    neg_inf = jnp.float32(-jnp.inf)
    lg = jnp.where(lane < N_EXPERTS, logits, neg_inf)
    m0 = jnp.max(lg, axis=1, keepdims=True)
    i0 = jnp.min(jnp.where(lg == m0, lane, V7X_LANES), axis=1, keepdims=True)
    lg1 = jnp.where(lane == i0, neg_inf, lg)
    m1 = jnp.max(lg1, axis=1, keepdims=True)
    i1 = jnp.min(jnp.where(lg1 == m1, lane, V7X_LANES), axis=1, keepdims=True)
    e = jnp.exp(m1 - m0)
    w0 = 1.0 / (1.0 + e)
    w1 = e / (1.0 + e)

    pick0 = lane == i0
    pick1 = lane == i1
    hot = (pick0 | pick1).astype(BF16)
    row = lax.broadcasted_iota(jnp.int32, (tm, tm), 0)
    col = lax.broadcasted_iota(jnp.int32, (tm, tm), 1)
    before = _dot((col < row).astype(BF16), hot) + carry_ref[0:1, :]
    r0 = jnp.sum(jnp.where(pick0, before, 0.0), axis=1, keepdims=True)
    r1 = jnp.sum(jnp.where(pick1, before, 0.0), axis=1, keepdims=True)
    carry_ref[0:1, :] = carry_ref[0:1, :] + jnp.sum(hot.astype(F32), axis=0, keepdims=True)

    slab = jnp.zeros((tm, V7X_LANES), F32)
    for c, val in ((COL_E0, i0.astype(F32)), (COL_E1, i1.astype(F32)), (COL_W0, w0), (COL_W1, w1),
                   (COL_R0, r0), (COL_R1, r1)):
        slab = jnp.where(lane == c, val, slab)
    slab_ref[...] = slab
    cnt_ref[...] = carry_ref[...]


def moe_router(h, g, w_router, tm):
    t, d = h.shape
    wr = jnp.zeros((d, V7X_LANES), F32).at[:, :N_EXPERTS].set(w_router)
    return pl.pallas_call(
        functools.partial(_router_kernel, tm=tm),
        grid=(t // tm,),
        in_specs=[pl.BlockSpec((tm, d), lambda i: (i, 0)),
                  pl.BlockSpec((1, d), lambda i: (0, 0)),
                  pl.BlockSpec((d, V7X_LANES), lambda i: (0, 0))],
        out_specs=[pl.BlockSpec((tm, d), lambda i: (i, 0)),
                   pl.BlockSpec((tm, V7X_LANES), lambda i: (i, 0)),
                   pl.BlockSpec((8, V7X_LANES), lambda i: (0, 0))],
        out_shape=[jax.ShapeDtypeStruct((t, d), F32),
                   jax.ShapeDtypeStruct((t, V7X_LANES), F32),
                   jax.ShapeDtypeStruct((8, V7X_LANES), F32)],
        scratch_shapes=[pltpu.VMEM((8, V7X_LANES), F32)],
        compiler_params=_cparams(("arbitrary",), 40),
        name="moe_router",
    )(h, g.reshape(1, d), wr)


def _moe_kernel(te_ref, nv_ref, nu_ref, tokc_ref, tokn_ref, dst_ref, hn_hbm, w1_ref, w3_ref, w2_ref, y_hbm,
                xbuf, xb16, acc_ref, ybuf, gsem, ssem, *, tm, nf, nt):
    i = pl.program_id(0)
    j = pl.program_id(1)
    nu = nu_ref[0]
    slot = i % 2

    def start_gather(tok_ref, s, n):
        def body(r, carry):
            pltpu.make_async_copy(hn_hbm.at[pl.ds(tok_ref[0, 0, r], 1), :], xbuf.at[s, pl.ds(r, 1), :],
                                  gsem.at[s]).start()
            return carry
        lax.fori_loop(0, n, body, 0)

    def wait_rows(n, src, dst, sem):
        n_al = pl.multiple_of((n // SUBLANES) * SUBLANES, SUBLANES)

        @pl.when(n_al > 0)
        def _():
            pltpu.make_async_copy(src.at[pl.ds(0, n_al), :], dst.at[pl.ds(0, n_al), :], sem).wait()

        def body(r, carry):
            pltpu.make_async_copy(src.at[pl.ds(0, 1), :], dst.at[pl.ds(0, 1), :], sem).wait()
            return carry
        lax.fori_loop(0, n - n_al, body, 0)

    def wait_gather(s, n):
        wait_rows(n, hn_hbm, xbuf.at[s], gsem.at[s])

    def start_scatter(n):
        def body(r, carry):
            pltpu.make_async_copy(ybuf.at[pl.ds(r, 1), :], y_hbm.at[pl.ds(dst_ref[0, 0, r], 1), :],
                                  ssem.at[0]).start()
            return carry
        lax.fori_loop(0, n, body, 0)

    def wait_scatter(n):
        wait_rows(n, ybuf, y_hbm, ssem.at[0])

    @pl.when((j == 0) & (i == 0))
    def _():
        xbuf[...] = jnp.zeros_like(xbuf)

        @pl.when(nu > 0)
        def _():
            start_gather(tokc_ref, 0, nv_ref[0])

    @pl.when((j == 0) & (i < nu))
    def _():
        wait_gather(slot, nv_ref[i])
        xb16[...] = xbuf[slot].astype(BF16)

    @pl.when((j == 0) & (i + 1 < nu))
    def _():
        start_gather(tokn_ref, 1 - slot, nv_ref[jnp.minimum(i + 1, nt - 1)])

    @pl.when(i < nu)
    def _():
        x = xb16[...]
        a = _dot(x, w1_ref[0])
        b = _dot(x, w3_ref[0])
        part = _dot((_silu(a) * b).astype(BF16), w2_ref[0])

        @pl.when(j == 0)
        def _():
            acc_ref[...] = part

        @pl.when(j > 0)
        def _():
            acc_ref[...] += part

    last = j == nf - 1

    @pl.when(last & (i >= 1) & (i - 1 < nu))
    def _():
        wait_scatter(nv_ref[jnp.maximum(i - 1, 0)])

    @pl.when(last & (i < nu))
    def _():
        ybuf[...] = acc_ref[...]
        start_scatter(nv_ref[i])

    @pl.when(last & (i == nt - 1) & (i < nu))
    def _():
        wait_scatter(nv_ref[i])


def moe_experts(hn, slab, counts, w1, w3, w2, tm, tf):
    t, d = hn.shape
    f = w1.shape[2]
    nf = f // tf
    nt = (2 * t) // tm + N_EXPERTS
    rows = nt * tm

    e0 = slab[:, COL_E0].astype(jnp.int32)
    e1 = slab[:, COL_E1].astype(jnp.int32)
    r0 = slab[:, COL_R0].astype(jnp.int32)
    r1 = slab[:, COL_R1].astype(jnp.int32)
    cnt = counts[0, :N_EXPERTS].astype(jnp.int32)
    tiles = (cnt + tm - 1) // tm
    tile_end = jnp.cumsum(tiles)
    tile_start = tile_end - tiles
    nu = tile_end[-1:]
    offs = tile_start * tm
    dest0 = offs[e0] + r0
    dest1 = offs[e1] + r1
    tile_expert = jnp.minimum(
        jnp.sum(jnp.arange(nt, dtype=jnp.int32)[:, None] >= tile_end[None, :], axis=1), N_EXPERTS - 1
    ).astype(jnp.int32)
    tile_ids = jnp.arange(nt, dtype=jnp.int32)
    n_valid = jnp.clip(cnt[tile_expert] - (tile_ids - tile_start[tile_expert]) * tm, 0, tm)
    n_valid = jnp.where(tile_ids < nu[0], n_valid, 0).astype(jnp.int32)
    tok_ids = jnp.arange(t, dtype=jnp.int32)
    tok_sorted = jnp.zeros((rows,), jnp.int32).at[dest0].set(tok_ids).at[dest1].set(tok_ids)
    dst_sorted = jnp.zeros((rows,), jnp.int32).at[dest0].set(tok_ids).at[dest1].set(t + tok_ids)
    tok3 = tok_sorted.reshape(nt, 1, tm)
    dst3 = dst_sorted.reshape(nt, 1, tm)

    def w_in(shape, which):
        def index_map(i, j, te, nv, nu_):
            ii = jnp.minimum(i, nu_[0] - 1)
            jj = jnp.where(i < nu_[0], j, nf - 1)
            return (te[ii], 0, jj) if which == "up" else (te[ii], jj, 0)
        return pl.BlockSpec(shape, index_map)

    smem_tile = lambda f_: pl.BlockSpec((1, 1, tm), f_, memory_space=pltpu.SMEM)
    grid_spec = pltpu.PrefetchScalarGridSpec(
        num_scalar_prefetch=3,
        grid=(nt, nf),
        in_specs=[smem_tile(lambda i, j, te, nv, nu_: (i, 0, 0)),
                  smem_tile(lambda i, j, te, nv, nu_: (jnp.minimum(i + 1, nt - 1), 0, 0)),
                  smem_tile(lambda i, j, te, nv, nu_: (i, 0, 0)),
                  pl.BlockSpec(memory_space=pl.ANY),
                  w_in((1, d, tf), "up"), w_in((1, d, tf), "up"), w_in((1, tf, d), "down")],
        out_specs=pl.BlockSpec(memory_space=pl.ANY),
        scratch_shapes=[pltpu.VMEM((2, tm, d), F32),
                        pltpu.VMEM((tm, d), BF16),
                        pltpu.VMEM((tm, d), F32),
                        pltpu.VMEM((tm, d), F32),
                        pltpu.SemaphoreType.DMA((2,)),
                        pltpu.SemaphoreType.DMA((1,))],
    )
    return pl.pallas_call(
        functools.partial(_moe_kernel, tm=tm, nf=nf, nt=nt),
        grid_spec=grid_spec,
        out_shape=jax.ShapeDtypeStruct((2 * t, d), F32),
        compiler_params=_cparams(("arbitrary", "arbitrary"), 56),
        name="moe_experts",
    )(tile_expert, n_valid, nu, tok3, tok3, dst3, hn, w1, w3, w2)


def _combine_kernel(h_ref, y0_ref, y1_ref, slab_ref, g_ref, out_ref, *, final):
    slab = slab_ref[...]
    x = h_ref[...] + slab[:, COL_W0:COL_W0 + 1] * y0_ref[...] + slab[:, COL_W1:COL_W1 + 1] * y1_ref[...]
    out_ref[...] = _rmsnorm_f32(x, g_ref[...]) if final else x


def moe_combine(h, y, slab, g, tm, final):
    t, d = h.shape
    nb = t // tm
    return pl.pallas_call(
        functools.partial(_combine_kernel, final=final),
        grid=(nb,),
        in_specs=[pl.BlockSpec((tm, d), lambda i: (i, 0)),
                  pl.BlockSpec((tm, d), lambda i: (i, 0)),
                  pl.BlockSpec((tm, d), lambda i: (i + nb, 0)),
                  pl.BlockSpec((tm, V7X_LANES), lambda i: (i, 0)),
                  pl.BlockSpec((1, d), lambda i: (0, 0))],
        out_specs=pl.BlockSpec((tm, d), lambda i: (i, 0)),
        out_shape=jax.ShapeDtypeStruct((t, d), F32),
        compiler_params=_cparams(("arbitrary",), 40),
        name="moe_combine",
    )(h, y, y, slab, g.reshape(1, d))


def _norm_kernel(h_ref, g_ref, out_ref):
    out_ref[...] = _rmsnorm_f32(h_ref[...], g_ref[...])


def final_norm_only(h, g, tm):
    t, d = h.shape
    return pl.pallas_call(
        _norm_kernel,
        grid=(t // tm,),
        in_specs=[pl.BlockSpec((tm, d), lambda i: (i, 0)), pl.BlockSpec((1, d), lambda i: (0, 0))],
        out_specs=pl.BlockSpec((tm, d), lambda i: (i, 0)),
        out_shape=jax.ShapeDtypeStruct((t, d), F32),
        compiler_params=_cparams(("arbitrary",), 40),
        name="final_norm",
    )(h, g.reshape(1, d))


def _row_tile(t, want):
    tm = min(want, t)
    assert t % tm == 0
    return tm


def kernel(x, w_qkv, w_o, mixer_norm, ffn_norm, rel_bias, w1, w3, w2, router, e_w1, e_w3, e_w2, final_norm):
    b, s, d = x.shape
    assert d == N_HEADS * HEAD_DIM and s % MOBA_BLOCK == 0
    t = b * s
    depth = w_qkv.shape[0]
    h = x.reshape(t, d)
    tm_big = _row_tile(t, 1024)
    tm_mid = _row_tile(t, 512)
    normed = False
    for i in range(depth):
        qkv = qkv_proj(h, mixer_norm[i], w_qkv[i].astype(BF16), tm_big)
        if i % 2 == 0:
            o = stick_breaking_attention(qkv, b, s, MOBA_BLOCK)
        else:
            o = moba_attention(qkv, rel_bias, b, s)
        h = out_proj(o.reshape(t, d), w_o[i].astype(BF16), h, tm_mid)
        jj = i // 2
        if i % 2 == 0:
            h = dense_ffn(h, ffn_norm[i], w1[jj].astype(BF16), w3[jj].astype(BF16), w2[jj].astype(BF16),
                          tm_mid, 1024)
        else:
            hn, slab, counts = moe_router(h, ffn_norm[i], router[jj], tm_mid)
            f_e = e_w1.shape[-1]
            tf = f_e // 2 if (f_e // 2) % 256 == 0 else f_e
            y = moe_experts(hn, slab, counts, e_w1[jj].astype(BF16), e_w3[jj].astype(BF16),
                            e_w2[jj].astype(BF16), tm_mid, tf)
            last = i == depth - 1
            h = moe_combine(h, y, slab, final_norm if last else ffn_norm[i], tm_mid, last)
            normed = last
    if not normed:
        h = final_norm_only(h, final_norm, tm_mid)
    return h.reshape(b, s, d)
```

```python
import functools
import math

import jax
import jax.numpy as jnp
from jax import lax
from jax.experimental import pallas as pl
from jax.experimental.pallas import tpu as pltpu

N_HEADS = 16
HEAD_DIM = 64
MOBA_BLOCK = 256
MOBA_TOPK = 3
N_BUCKETS = 32
MAX_DISTANCE = 128
N_EXPERTS = 8
RMS_EPS = 1e-6
NEG_BIG = -1e30

V7X_LANES = 128
SUBLANES = 8
V7X_VMEM_BYTES = 64 * 1024 * 1024
HEADS_PER_LANE_BLOCK = V7X_LANES // HEAD_DIM

F32 = jnp.float32
BF16 = jnp.bfloat16


def _cparams(semantics, vmem_mb):
    assert vmem_mb * 1024 * 1024 < V7X_VMEM_BYTES
    return pltpu.CompilerParams(dimension_semantics=semantics,
                                vmem_limit_bytes=vmem_mb * 1024 * 1024)


def _rmsnorm_f32(x, g):
    return x * lax.rsqrt(jnp.mean(x * x, axis=-1, keepdims=True) + RMS_EPS) * g


def _silu(a):
    return a * (1.0 / (1.0 + jnp.exp(-a)))


def _dot(a, b):
    return jnp.dot(a, b, preferred_element_type=F32)


def _dot_nt(a, b):
    return lax.dot_general(a, b, (((1,), (1,)), ((), ())), preferred_element_type=F32)


def _qkv_kernel(x_ref, g_ref, w_ref, o_ref, hn_ref):
    j = pl.program_id(1)

    @pl.when(j == 0)
    def _():
        hn_ref[...] = _rmsnorm_f32(x_ref[...], g_ref[...]).astype(BF16)

    scale = jnp.where(j == 0, HEAD_DIM ** -0.5, 1.0)
    o_ref[0] = (_dot(hn_ref[...], w_ref[...]) * scale).astype(o_ref.dtype)


def qkv_proj(h, g, w_bf16, tm):
    t, d = h.shape
    return pl.pallas_call(
        _qkv_kernel,
        grid=(t // tm, 3),
        in_specs=[pl.BlockSpec((tm, d), lambda i, j: (i, 0)),
                  pl.BlockSpec((1, d), lambda i, j: (0, 0)),
                  pl.BlockSpec((d, d), lambda i, j: (0, j))],
        out_specs=pl.BlockSpec((1, tm, d), lambda i, j: (j, i, 0)),
        out_shape=jax.ShapeDtypeStruct((3, t, d), BF16),
        scratch_shapes=[pltpu.VMEM((tm, d), BF16)],
        compiler_params=_cparams(("arbitrary", "arbitrary"), 40),
        name="qkv_proj",
    )(h, g.reshape(1, d), w_bf16)


def _sb_kernel(q_ref, k_ref, v_ref, o_ref, acc0_ref, acc1_ref, r0_ref, r1_ref, u_ref, *, tq):
    acc_ref = (acc0_ref, acc1_ref)
    r_ref = (r0_ref, r1_ref)
    qi = pl.program_id(2)
    row = lax.broadcasted_iota(jnp.int32, (tq, tq), 0)
    col = lax.broadcasted_iota(jnp.int32, (tq, tq), 1)

    @pl.when(qi == 0)
    def _():
        u_ref[...] = -(row > col).astype(BF16)

    q = q_ref[0, 0]
    lane = lax.broadcasted_iota(jnp.int32, (tq, V7X_LANES), 1)
    past = col < row
    heads = range(HEADS_PER_LANE_BLOCK)
    qms = [jnp.where(lane // HEAD_DIM == h, q, jnp.zeros_like(q)) for h in heads]

    def span(first, nb, diagonal):
        start = pl.multiple_of(first * tq, nb * tq)
        ks = k_ref[0, 0, pl.ds(start, nb * tq), :]
        vs = v_ref[0, 0, pl.ds(start, nb * tq), :]
        for h in heads:
            z = _dot_nt(qms[h], ks)
            sp = jnp.maximum(z, 0.0) + jnp.log(1.0 + jnp.exp(-jnp.abs(z)))
            log_sig = z - sp
            parts = [sp[:, i * tq:(i + 1) * tq] for i in range(nb)]
            if diagonal:
                parts[-1] = jnp.where(past, parts[-1], 0.0)
            after = _dot(jnp.concatenate(parts, axis=0).astype(BF16), u_ref[...])
            r = r_ref[h][...]
            ws = [None] * nb
            for i in reversed(range(nb)):
                tot = after[i * tq:(i + 1) * tq] + r
                ws[i] = jnp.exp(log_sig[:, i * tq:(i + 1) * tq] + tot)
                r = tot[:, 0:1] - parts[i][:, 0:1]
            if diagonal:
                ws[-1] = jnp.where(past, ws[-1], 0.0)
            r_ref[h][...] = r
            acc_ref[h][...] += _dot(jnp.concatenate(ws, axis=1).astype(BF16), vs)

    for h in heads:
        r_ref[h][...] = jnp.zeros((tq, 1), F32)
        acc_ref[h][...] = jnp.zeros((tq, V7X_LANES), F32)

    @pl.when(qi % 2 == 0)
    def _():
        span(qi, 1, True)

    @pl.when(qi % 2 == 1)
    def _():
        span(qi - 1, 2, True)

    n_pairs = qi // 2

    def body(s, carry):
        span(2 * (n_pairs - 1 - s), 2, False)
        return carry

    lax.fori_loop(0, n_pairs, body, 0)

    o_ref[0] = jnp.where(lane // HEAD_DIM == 0, acc0_ref[...], acc1_ref[...]).astype(o_ref.dtype)


def stick_breaking_attention(qkv, b, s, tq):
    d = qkv.shape[-1]
    qkv4 = qkv.reshape(3, b, s, d)
    return pl.pallas_call(
        functools.partial(_sb_kernel, tq=tq),
        grid=(b, d // V7X_LANES, s // tq),
        in_specs=[pl.BlockSpec((1, 1, tq, V7X_LANES), lambda bi, hp, qi: (0, bi, qi, hp)),
                  pl.BlockSpec((1, 1, s, V7X_LANES), lambda bi, hp, qi: (1, bi, 0, hp)),
                  pl.BlockSpec((1, 1, s, V7X_LANES), lambda bi, hp, qi: (2, bi, 0, hp))],
        out_specs=pl.BlockSpec((1, tq, V7X_LANES), lambda bi, hp, qi: (bi, qi, hp)),
        out_shape=jax.ShapeDtypeStruct((b, s, d), BF16),
        scratch_shapes=[pltpu.VMEM((tq, V7X_LANES), F32), pltpu.VMEM((tq, V7X_LANES), F32),
                        pltpu.VMEM((tq, 1), F32), pltpu.VMEM((tq, 1), F32),
                        pltpu.VMEM((tq, tq), BF16)],
        compiler_params=_cparams(("arbitrary", "arbitrary", "arbitrary"), 32),
        name="stick_breaking",
    )(qkv4, qkv4, qkv4)


def _t5_bias_kernel(rb_ref, o_ref, *, tq):
    h = pl.program_id(0)
    row = lax.broadcasted_iota(jnp.int32, (tq, tq), 0)
    col = lax.broadcasted_iota(jnp.int32, (tq, tq), 1)
    max_exact = N_BUCKETS // 2
    for o in range(2):
        dist = o * tq + row - col
        n = jnp.maximum(dist, 0)
        nf = jnp.maximum(n, 1).astype(F32)
        large = max_exact + (jnp.log(nf / max_exact) / math.log(MAX_DISTANCE / max_exact)
                             * (N_BUCKETS - max_exact)).astype(jnp.int32)
        large = jnp.minimum(large, N_BUCKETS - 1)
        bucket = jnp.where(n < max_exact, n, large)
        bias = jnp.zeros((tq, tq), F32)
        for bkt in range(N_BUCKETS):
            bias = jnp.where(bucket == bkt, rb_ref[bkt, h], bias)
        if o == 0:
            bias = jnp.where(dist >= 0, bias, NEG_BIG)
        o_ref[0, o] = bias


def t5_bias_tiles(rel_bias, tq):
    return pl.pallas_call(
        functools.partial(_t5_bias_kernel, tq=tq),
        grid=(N_HEADS,),
        in_specs=[pl.BlockSpec(memory_space=pltpu.SMEM)],
        out_specs=pl.BlockSpec((1, 2, tq, tq), lambda h: (h, 0, 0, 0)),
        out_shape=jax.ShapeDtypeStruct((N_HEADS, 2, tq, tq), F32),
        compiler_params=_cparams(("arbitrary",), 16),
        name="t5_bias_tiles",
    )(rel_bias)


def _moba_kernel(rb_ref, q_ref, k_ref, v_ref, bt_ref, o_ref,
                 s_ref, mask_ref, far_ref, mx_ref, l_ref, acc_ref, km_ref, *, tq, nblk):
    hp = pl.program_id(1)
    qi = pl.program_id(2)

    @pl.when(qi == 0)
    def _():
        for n in range(nblk):
            kb = k_ref[0, 0, n * tq:(n + 1) * tq, :].astype(F32)
            km_ref[n:n + 1, :] = jnp.mean(kb, axis=0, keepdims=True)

    q = q_ref[0, 0]
    lane = lax.broadcasted_iota(jnp.int32, (tq, V7X_LANES), 1)
    blk = lax.broadcasted_iota(jnp.int32, (nblk, tq), 0)
    km = km_ref[...]
    km_hi = km.astype(BF16)
    km_lo = (km - km_hi.astype(F32)).astype(BF16)
    n_sel = min(MOBA_TOPK, nblk - 1)

    heads = range(HEADS_PER_LANE_BLOCK)
    qms = [jnp.where(lane // HEAD_DIM == h, q, jnp.zeros_like(q)) for h in heads]
    both = lambda x: jnp.concatenate([x, x], axis=1)
    halves = lambda x: (x[:, :V7X_LANES], x[:, V7X_LANES:])
    prev_blk = jnp.maximum(qi - 1, 0)

    big = jnp.asarray(-NEG_BIG, BF16).astype(F32)
    spread_col = lax.broadcasted_iota(jnp.int32, (nblk, nblk * V7X_LANES), 1) // V7X_LANES
    spread_row = lax.broadcasted_iota(jnp.int32, (nblk, nblk * V7X_LANES), 0)
    spread = jnp.where(spread_col == spread_row, big, 0.0).astype(BF16)
    for h in heads:
        gate = _dot_nt(km_hi, qms[h]) + _dot_nt(km_lo, qms[h])
        gate = jnp.where(blk < qi, gate, NEG_BIG)
        chosen = jnp.zeros((nblk, tq), F32)
        for n in range(nblk):
            gn = gate[n:n + 1, :]
            beats = (gate > gn) | ((gate == gn) & (blk < n))
            rank = jnp.sum(beats.astype(F32), axis=0, keepdims=True)
            chosen = jnp.where((blk == n) & (rank < n_sel) & (n < qi), 1.0, chosen)
        lifted = lax.dot_general(chosen.astype(BF16), spread, (((0,), (0,)), ((), ())),
                                 preferred_element_type=F32)
        far_bias = rb_ref[N_BUCKETS - 1, hp * HEADS_PER_LANE_BLOCK + h]
        for n in range(nblk):
            masked = lifted[:, n * V7X_LANES:(n + 1) * V7X_LANES] - big
            mask_ref[h, n] = masked
            far_ref[h, n] = masked + far_bias
        mx_ref[h] = jnp.full((tq, V7X_LANES), NEG_BIG, F32)

    def logits(n, bias_of_head):
        start = pl.multiple_of(n * tq, tq)
        ks = k_ref[0, 0, pl.ds(start, tq), :]
        for h in heads:
            s = _dot_nt(qms[h], ks) + bias_of_head(h)
            s_ref[h, :, pl.ds(start, tq)] = s
            s_lo, s_hi = halves(s)
            mx_ref[h] = jnp.maximum(mx_ref[h], jnp.maximum(s_lo, s_hi))

    def far_body(n, carry):
        logits(n, lambda h: both(far_ref[h, n]))
        return carry

    lax.fori_loop(0, prev_blk, far_body, 0)
    logits(prev_blk, lambda h: bt_ref[h, 1] + both(mask_ref[h, prev_blk]))
    logits(qi, lambda h: bt_ref[h, 0])

    for h in heads:
        mx_ref[h] = jnp.broadcast_to(jnp.max(mx_ref[h], axis=1, keepdims=True), (tq, V7X_LANES))
        l_ref[h] = jnp.zeros((tq, V7X_LANES), F32)
        acc_ref[h] = jnp.zeros((tq, V7X_LANES), F32)

    def prob_body(n, carry):
        start = pl.multiple_of(n * tq, tq)
        vs = v_ref[0, 0, pl.ds(start, tq), :]
        for h in heads:
            p = jnp.exp(s_ref[h, :, pl.ds(start, tq)] - both(mx_ref[h]))
            p_lo, p_hi = halves(p)
            l_ref[h] += p_lo + p_hi
            acc_ref[h] += _dot(p.astype(BF16), vs)
        return carry

    lax.fori_loop(0, qi + 1, prob_body, 0)

    out = [acc_ref[h] / jnp.sum(l_ref[h], axis=1, keepdims=True) for h in heads]
    o_ref[0] = jnp.where(lane // HEAD_DIM == 0, out[0], out[1]).astype(o_ref.dtype)


def moba_attention(qkv, rel_bias, b, s):
    d = qkv.shape[-1]
    tq = MOBA_BLOCK
    assert s % tq == 0 and tq >= 2 * MAX_DISTANCE
    nblk = s // tq
    qkv4 = qkv.reshape(3, b, s, d)
    tiles = t5_bias_tiles(rel_bias, tq)
    return pl.pallas_call(
        functools.partial(_moba_kernel, tq=tq, nblk=nblk),
        grid=(b, d // V7X_LANES, nblk),
        in_specs=[pl.BlockSpec(memory_space=pltpu.SMEM),
                  pl.BlockSpec((1, 1, tq, V7X_LANES), lambda bi, hp, qi: (0, bi, qi, hp)),
                  pl.BlockSpec((1, 1, s, V7X_LANES), lambda bi, hp, qi: (1, bi, 0, hp)),
                  pl.BlockSpec((1, 1, s, V7X_LANES), lambda bi, hp, qi: (2, bi, 0, hp)),
                  pl.BlockSpec((HEADS_PER_LANE_BLOCK, 2, tq, tq), lambda bi, hp, qi: (hp, 0, 0, 0))],
        out_specs=pl.BlockSpec((1, tq, V7X_LANES), lambda bi, hp, qi: (bi, qi, hp)),
        out_shape=jax.ShapeDtypeStruct((b, s, d), BF16),
        scratch_shapes=[pltpu.VMEM((HEADS_PER_LANE_BLOCK, tq, s), F32),
                        pltpu.VMEM((HEADS_PER_LANE_BLOCK, nblk, tq, V7X_LANES), F32),
                        pltpu.VMEM((HEADS_PER_LANE_BLOCK, nblk, tq, V7X_LANES), F32),
                        pltpu.VMEM((HEADS_PER_LANE_BLOCK, tq, V7X_LANES), F32),
                        pltpu.VMEM((HEADS_PER_LANE_BLOCK, tq, V7X_LANES), F32),
                        pltpu.VMEM((HEADS_PER_LANE_BLOCK, tq, V7X_LANES), F32),
                        pltpu.VMEM((nblk, V7X_LANES), F32)],
        compiler_params=_cparams(("arbitrary", "arbitrary", "arbitrary"), 32),
        name="moba",
    )(rel_bias, qkv4, qkv4, qkv4, tiles)


def _oproj_kernel(o_ref, w_ref, h_ref, out_ref):
    out_ref[...] = h_ref[...] + _dot(o_ref[...], w_ref[...])


def out_proj(o, w_bf16, h, tm):
    t, d = h.shape
    return pl.pallas_call(
        _oproj_kernel,
        grid=(t // tm,),
        in_specs=[pl.BlockSpec((tm, d), lambda i: (i, 0)),
                  pl.BlockSpec((d, d), lambda i: (0, 0)),
                  pl.BlockSpec((tm, d), lambda i: (i, 0))],
        out_specs=pl.BlockSpec((tm, d), lambda i: (i, 0)),
        out_shape=jax.ShapeDtypeStruct((t, d), F32),
        compiler_params=_cparams(("arbitrary",), 40),
        name="out_proj",
    )(o, w_bf16, h)


def _ffn_kernel(h_ref, g_ref, w1_ref, w3_ref, w2_ref, out_ref, *, chunks):
    x = h_ref[...]
    hn = _rmsnorm_f32(x, g_ref[...]).astype(BF16)
    acc = x
    for c0, c1 in chunks:
        a = _dot(hn, w1_ref[:, c0:c1])
        b = _dot(hn, w3_ref[:, c0:c1])
        acc = acc + _dot((_silu(a) * b).astype(BF16), w2_ref[c0:c1, :])
    out_ref[...] = acc


def dense_ffn(h, g, w1, w3, w2, tm, chunk):
    t, d = h.shape
    f = w1.shape[1]
    chunks = tuple((c, min(c + chunk, f)) for c in range(0, f, chunk))
    whole = lambda shape: pl.BlockSpec(shape, lambda i: (0, 0), pipeline_mode=pl.Buffered(1))
    return pl.pallas_call(
        functools.partial(_ffn_kernel, chunks=chunks),
        grid=(t // tm,),
        in_specs=[pl.BlockSpec((tm, d), lambda i: (i, 0)),
                  pl.BlockSpec((1, d), lambda i: (0, 0)),
                  whole((d, f)), whole((d, f)), whole((f, d))],
        out_specs=pl.BlockSpec((tm, d), lambda i: (i, 0)),
        out_shape=jax.ShapeDtypeStruct((t, d), F32),
        compiler_params=_cparams(("arbitrary",), 56),
        name="dense_ffn",
    )(h, g.reshape(1, d), w1, w3, w2)


COL_E0, COL_E1, COL_W0, COL_W1, COL_R0, COL_R1 = range(6)


def _router_kernel(h_ref, g_ref, wr_ref, hn_ref, slab_ref, cnt_ref, carry_ref, *, tm):
    i = pl.program_id(0)

    @pl.when(i == 0)
    def _():
        carry_ref[...] = jnp.zeros_like(carry_ref)

    hn = _rmsnorm_f32(h_ref[...], g_ref[...])
    hn_ref[...] = hn
    hi = hn.astype(BF16)
    lo = (hn - hi.astype(F32)).astype(BF16)
    w = wr_ref[...]
    w_hi = w.astype(BF16)
    w_lo = (w - w_hi.astype(F32)).astype(BF16)
    logits = _dot(hi, w_hi) + _dot(hi, w_lo) + _dot(lo, w_hi)

    lane = lax.broadcasted_iota(jnp.int32, (tm, V7X_LANES), 1)
    neg_inf = jnp.float32(-jnp.inf)
    lg = jnp.where(lane < N_EXPERTS, logits, neg_inf)
    m0 = jnp.max(lg, axis=1, keepdims=True)
    i0 = jnp.min(jnp.where(lg == m0, lane, V7X_LANES), axis=1, keepdims=True)
    lg1 = jnp.where(lane == i0, neg_inf, lg)
    m1 = jnp.max(lg1, axis=1, keepdims=True)
    i1 = jnp.min(jnp.where(lg1 == m1, lane, V7X_LANES), axis=1, keepdims=True)
    e = jnp.exp(m1 - m0)
    w0 = 1.0 / (1.0 + e)
    w1 = e / (1.0 + e)

    pick0 = lane == i0
    pick1 = lane == i1
    hot = (pick0 | pick1).astype(BF16)
    row = lax.broadcasted_iota(jnp.int32, (tm, tm), 0)
    col = lax.broadcasted_iota(jnp.int32, (tm, tm), 1)
    before = _dot((col < row).astype(BF16), hot) + carry_ref[0:1, :]
    r0 = jnp.sum(jnp.where(pick0, before, 0.0), axis=1, keepdims=True)
    r1 = jnp.sum(jnp.where(pick1, before, 0.0), axis=1, keepdims=True)
    carry_ref[0:1, :] = carry_ref[0:1, :] + jnp.sum(hot.astype(F32), axis=0, keepdims=True)

    slab = jnp.zeros((tm, V7X_LANES), F32)
    for c, val in ((COL_E0, i0.astype(F32)), (COL_E1, i1.astype(F32)), (COL_W0, w0), (COL_W1, w1),
                   (COL_R0, r0), (COL_R1, r1)):
        slab = jnp.where(lane == c, val, slab)
    slab_ref[...] = slab
    cnt_ref[...] = carry_ref[...]


def moe_router(h, g, w_router, tm):
    t, d = h.shape
    wr = jnp.zeros((d, V7X_LANES), F32).at[:, :N_EXPERTS].set(w_router)
    return pl.pallas_call(
        functools.partial(_router_kernel, tm=tm),
        grid=(t // tm,),
        in_specs=[pl.BlockSpec((tm, d), lambda i: (i, 0)),
                  pl.BlockSpec((1, d), lambda i: (0, 0)),
                  pl.BlockSpec((d, V7X_LANES), lambda i: (0, 0))],
        out_specs=[pl.BlockSpec((tm, d), lambda i: (i, 0)),
                   pl.BlockSpec((tm, V7X_LANES), lambda i: (i, 0)),
                   pl.BlockSpec((8, V7X_LANES), lambda i: (0, 0))],
        out_shape=[jax.ShapeDtypeStruct((t, d), F32),
                   jax.ShapeDtypeStruct((t, V7X_LANES), F32),
                   jax.ShapeDtypeStruct((8, V7X_LANES), F32)],
        scratch_shapes=[pltpu.VMEM((8, V7X_LANES), F32)],
        compiler_params=_cparams(("arbitrary",), 40),
        name="moe_router",
    )(h, g.reshape(1, d), wr)


def _moe_kernel(te_ref, nv_ref, nu_ref, tokc_ref, tokn_ref, dst_ref, hn_hbm, w1_ref, w3_ref, w2_ref, y_hbm,
                xbuf, xb16, acc_ref, ybuf, gsem, ssem, *, tm, nf, nt):
    i = pl.program_id(0)
    j = pl.program_id(1)
    nu = nu_ref[0]
    slot = i % 2

    def start_gather(tok_ref, s, n):
        def body(r, carry):
            pltpu.make_async_copy(hn_hbm.at[pl.ds(tok_ref[0, 0, r], 1), :], xbuf.at[s, pl.ds(r, 1), :],
                                  gsem.at[s]).start()
            return carry
        lax.fori_loop(0, n, body, 0)

    def wait_rows(n, src, dst, sem):
        n_al = pl.multiple_of((n // SUBLANES) * SUBLANES, SUBLANES)

        @pl.when(n_al > 0)
        def _():
            pltpu.make_async_copy(src.at[pl.ds(0, n_al), :], dst.at[pl.ds(0, n_al), :], sem).wait()

        def body(r, carry):
            pltpu.make_async_copy(src.at[pl.ds(0, 1), :], dst.at[pl.ds(0, 1), :], sem).wait()
            return carry
        lax.fori_loop(0, n - n_al, body, 0)

    def wait_gather(s, n):
        wait_rows(n, hn_hbm, xbuf.at[s], gsem.at[s])

    def start_scatter(n):
        def body(r, carry):
            pltpu.make_async_copy(ybuf.at[pl.ds(r, 1), :], y_hbm.at[pl.ds(dst_ref[0, 0, r], 1), :],
                                  ssem.at[0]).start()
            return carry
        lax.fori_loop(0, n, body, 0)

    def wait_scatter(n):
        wait_rows(n, ybuf, y_hbm, ssem.at[0])

    @pl.when((j == 0) & (i == 0))
    def _():
        xbuf[...] = jnp.zeros_like(xbuf)

        @pl.when(nu > 0)
        def _():
            start_gather(tokc_ref, 0, nv_ref[0])

    @pl.when((j == 0) & (i < nu))
    def _():
        wait_gather(slot, nv_ref[i])
        xb16[...] = xbuf[slot].astype(BF16)

    @pl.when((j == 0) & (i + 1 < nu))
    def _():
        start_gather(tokn_ref, 1 - slot, nv_ref[jnp.minimum(i + 1, nt - 1)])

    @pl.when(i < nu)
    def _():
        x = xb16[...]
        a = _dot(x, w1_ref[0])
        b = _dot(x, w3_ref[0])
        part = _dot((_silu(a) * b).astype(BF16), w2_ref[0])

        @pl.when(j == 0)
        def _():
            acc_ref[...] = part

        @pl.when(j > 0)
        def _():
            acc_ref[...] += part

    last = j == nf - 1

    @pl.when(last & (i >= 1) & (i - 1 < nu))
    def _():
        wait_scatter(nv_ref[jnp.maximum(i - 1, 0)])

    @pl.when(last & (i < nu))
    def _():
        ybuf[...] = acc_ref[...]
        start_scatter(nv_ref[i])

    @pl.when(last & (i == nt - 1) & (i < nu))
    def _():
        wait_scatter(nv_ref[i])


def moe_experts(hn, slab, counts, w1, w3, w2, tm, tf):
    t, d = hn.shape
    f = w1.shape[2]
    nf = f // tf
    nt = (2 * t) // tm + N_EXPERTS
    rows = nt * tm

    e0 = slab[:, COL_E0].astype(jnp.int32)
    e1 = slab[:, COL_E1].astype(jnp.int32)
    r0 = slab[:, COL_R0].astype(jnp.int32)
    r1 = slab[:, COL_R1].astype(jnp.int32)
    cnt = counts[0, :N_EXPERTS].astype(jnp.int32)
    tiles = (cnt + tm - 1) // tm
    tile_end = jnp.cumsum(tiles)
    tile_start = tile_end - tiles
    nu = tile_end[-1:]
    offs = tile_start * tm
    dest0 = offs[e0] + r0
    dest1 = offs[e1] + r1
    tile_expert = jnp.minimum(
        jnp.sum(jnp.arange(nt, dtype=jnp.int32)[:, None] >= tile_end[None, :], axis=1), N_EXPERTS - 1
    ).astype(jnp.int32)
    tile_ids = jnp.arange(nt, dtype=jnp.int32)
    n_valid = jnp.clip(cnt[tile_expert] - (tile_ids - tile_start[tile_expert]) * tm, 0, tm)
    n_valid = jnp.where(tile_ids < nu[0], n_valid, 0).astype(jnp.int32)
    tok_ids = jnp.arange(t, dtype=jnp.int32)
    tok_sorted = jnp.zeros((rows,), jnp.int32).at[dest0].set(tok_ids).at[dest1].set(tok_ids)
    dst_sorted = jnp.zeros((rows,), jnp.int32).at[dest0].set(tok_ids).at[dest1].set(t + tok_ids)
    tok3 = tok_sorted.reshape(nt, 1, tm)
    dst3 = dst_sorted.reshape(nt, 1, tm)

    def w_in(shape, which):
        def index_map(i, j, te, nv, nu_):
            ii = jnp.minimum(i, nu_[0] - 1)
            jj = jnp.where(i < nu_[0], j, nf - 1)
            return (te[ii], 0, jj) if which == "up" else (te[ii], jj, 0)
        return pl.BlockSpec(shape, index_map)

    smem_tile = lambda f_: pl.BlockSpec((1, 1, tm), f_, memory_space=pltpu.SMEM)
    grid_spec = pltpu.PrefetchScalarGridSpec(
        num_scalar_prefetch=3,
        grid=(nt, nf),
        in_specs=[smem_tile(lambda i, j, te, nv, nu_: (i, 0, 0)),
                  smem_tile(lambda i, j, te, nv, nu_: (jnp.minimum(i + 1, nt - 1), 0, 0)),
                  smem_tile(lambda i, j, te, nv, nu_: (i, 0, 0)),
                  pl.BlockSpec(memory_space=pl.ANY),
                  w_in((1, d, tf), "up"), w_in((1, d, tf), "up"), w_in((1, tf, d), "down")],
        out_specs=pl.BlockSpec(memory_space=pl.ANY),
        scratch_shapes=[pltpu.VMEM((2, tm, d), F32),
                        pltpu.VMEM((tm, d), BF16),
                        pltpu.VMEM((tm, d), F32),
                        pltpu.VMEM((tm, d), F32),
                        pltpu.SemaphoreType.DMA((2,)),
                        pltpu.SemaphoreType.DMA((1,))],
    )
    return pl.pallas_call(
        functools.partial(_moe_kernel, tm=tm, nf=nf, nt=nt),
        grid_spec=grid_spec,
        out_shape=jax.ShapeDtypeStruct((2 * t, d), F32),
        compiler_params=_cparams(("arbitrary", "arbitrary"), 56),
        name="moe_experts",
    )(tile_expert, n_valid, nu, tok3, tok3, dst3, hn, w1, w3, w2)


def _combine_kernel(h_ref, y0_ref, y1_ref, slab_ref, g_ref, out_ref, *, final):
    slab = slab_ref[...]
    x = h_ref[...] + slab[:, COL_W0:COL_W0 + 1] * y0_ref[...] + slab[:, COL_W1:COL_W1 + 1] * y1_ref[...]
    out_ref[...] = _rmsnorm_f32(x, g_ref[...]) if final else x


def moe_combine(h, y, slab, g, tm, final):
    t, d = h.shape
    nb = t // tm
    return pl.pallas_call(
        functools.partial(_combine_kernel, final=final),
        grid=(nb,),
        in_specs=[pl.BlockSpec((tm, d), lambda i: (i, 0)),
                  pl.BlockSpec((tm, d), lambda i: (i, 0)),
                  pl.BlockSpec((tm, d), lambda i: (i + nb, 0)),
                  pl.BlockSpec((tm, V7X_LANES), lambda i: (i, 0)),
                  pl.BlockSpec((1, d), lambda i: (0, 0))],
        out_specs=pl.BlockSpec((tm, d), lambda i: (i, 0)),
        out_shape=jax.ShapeDtypeStruct((t, d), F32),
        compiler_params=_cparams(("arbitrary",), 40),
        name="moe_combine",
    )(h, y, y, slab, g.reshape(1, d))


def _norm_kernel(h_ref, g_ref, out_ref):
    out_ref[...] = _rmsnorm_f32(h_ref[...], g_ref[...])


def final_norm_only(h, g, tm):
    t, d = h.shape
    return pl.pallas_call(
        _norm_kernel,
        grid=(t // tm,),
        in_specs=[pl.BlockSpec((tm, d), lambda i: (i, 0)), pl.BlockSpec((1, d), lambda i: (0, 0))],
        out_specs=pl.BlockSpec((tm, d), lambda i: (i, 0)),
        out_shape=jax.ShapeDtypeStruct((t, d), F32),
        compiler_params=_cparams(("arbitrary",), 40),
        name="final_norm",
    )(h, g.reshape(1, d))


def _row_tile(t, want):
    tm = min(want, t)
    assert t % tm == 0
    return tm


def kernel(x, w_qkv, w_o, mixer_norm, ffn_norm, rel_bias, w1, w3, w2, router, e_w1, e_w3, e_w2, final_norm):
    b, s, d = x.shape
    assert d == N_HEADS * HEAD_DIM and s % MOBA_BLOCK == 0
    t = b * s
    depth = w_qkv.shape[0]
    h = x.reshape(t, d)
    tm_big = _row_tile(t, 1024)
    tm_mid = _row_tile(t, 512)
    normed = False
    for i in range(depth):
        qkv = qkv_proj(h, mixer_norm[i], w_qkv[i].astype(BF16), tm_big)
        if i % 2 == 0:
            o = stick_breaking_attention(qkv, b, s, MOBA_BLOCK)
        else:
            o = moba_attention(qkv, rel_bias, b, s)
        h = out_proj(o.reshape(t, d), w_o[i].astype(BF16), h, tm_mid)
        jj = i // 2
        if i % 2 == 0:
            h = dense_ffn(h, ffn_norm[i], w1[jj].astype(BF16), w3[jj].astype(BF16), w2[jj].astype(BF16),
                          tm_mid, 1024)
        else:
            hn, slab, counts = moe_router(h, ffn_norm[i], router[jj], tm_mid)
            f_e = e_w1.shape[-1]
            tf = f_e // 2 if (f_e // 2) % 256 == 0 else f_e
            y = moe_experts(hn, slab, counts, e_w1[jj].astype(BF16), e_w3[jj].astype(BF16),
                            e_w2[jj].astype(BF16), tm_mid, tf)
            last = i == depth - 1
            h = moe_combine(h, y, slab, final_norm if last else ffn_norm[i], tm_mid, last)
            normed = last
    if not normed:
        h = final_norm_only(h, final_norm, tm_mid)
    return h.reshape(b, s, d)
```

```python
import functools
import math

import jax
import jax.numpy as jnp
from jax import lax
from jax.experimental import pallas as pl
from jax.experimental.pallas import tpu as pltpu

N_HEADS = 16
HEAD_DIM = 64
MOBA_BLOCK = 256
MOBA_TOPK = 3
N_BUCKETS = 32
MAX_DISTANCE = 128
N_EXPERTS = 8
RMS_EPS = 1e-6
NEG_BIG = -1e30

V7X_LANES = 128
ROW_DMA_UNROLL = 8
SB_GROUP = 4
V7X_VMEM_BYTES = 64 * 1024 * 1024
HEADS_PER_LANE_BLOCK = V7X_LANES // HEAD_DIM

F32 = jnp.float32
BF16 = jnp.bfloat16


def _cparams(semantics, vmem_mb):
    assert vmem_mb * 1024 * 1024 < V7X_VMEM_BYTES
    return pltpu.CompilerParams(dimension_semantics=semantics,
                                vmem_limit_bytes=vmem_mb * 1024 * 1024)


def _rmsnorm_f32(x, g):
    return x * lax.rsqrt(jnp.mean(x * x, axis=-1, keepdims=True) + RMS_EPS) * g


def _silu(a):
    return a * (1.0 / (1.0 + jnp.exp(-a)))


def _dot(a, b):
    return jnp.dot(a, b, preferred_element_type=F32)


def _dot_nt(a, b):
    return lax.dot_general(a, b, (((1,), (1,)), ((), ())), preferred_element_type=F32)


def _qkv_kernel(x_ref, g_ref, w_ref, o_ref, hn_ref):
    j = pl.program_id(1)

    @pl.when(j == 0)
    def _():
        hn_ref[...] = _rmsnorm_f32(x_ref[...], g_ref[...]).astype(BF16)

    scale = jnp.where(j == 0, HEAD_DIM ** -0.5, 1.0)
    o_ref[0] = (_dot(hn_ref[...], w_ref[...]) * scale).astype(o_ref.dtype)


def qkv_proj(h, g, w_bf16, tm):
    t, d = h.shape
    return pl.pallas_call(
        _qkv_kernel,
        grid=(t // tm, 3),
        in_specs=[pl.BlockSpec((tm, d), lambda i, j: (i, 0)),
                  pl.BlockSpec((1, d), lambda i, j: (0, 0)),
                  pl.BlockSpec((d, d), lambda i, j: (0, j))],
        out_specs=pl.BlockSpec((1, tm, d), lambda i, j: (j, i, 0)),
        out_shape=jax.ShapeDtypeStruct((3, t, d), BF16),
        scratch_shapes=[pltpu.VMEM((tm, d), BF16)],
        compiler_params=_cparams(("arbitrary", "arbitrary"), 40),
        name="qkv_proj",
    )(h, g.reshape(1, d), w_bf16)


def _sb_kernel(q_ref, k_ref, v_ref, o_ref, acc0_ref, acc1_ref, r0_ref, r1_ref, u_ref, *, tq):
    acc_ref = (acc0_ref, acc1_ref)
    r_ref = (r0_ref, r1_ref)
    qi = pl.program_id(2)
    row = lax.broadcasted_iota(jnp.int32, (tq, tq), 0)
    col = lax.broadcasted_iota(jnp.int32, (tq, tq), 1)

    @pl.when(qi == 0)
    def _():
        u_ref[...] = -(row > col).astype(BF16)

    q = q_ref[0, 0]
    lane = lax.broadcasted_iota(jnp.int32, (tq, V7X_LANES), 1)
    past = col < row
    heads = range(HEADS_PER_LANE_BLOCK)
    qms = [jnp.where(lane // HEAD_DIM == h, q, jnp.zeros_like(q)) for h in heads]

    def span(first, nb, diagonal):
        start = pl.multiple_of(first * tq, tq)
        ks = k_ref[0, 0, pl.ds(start, nb * tq), :]
        vs = v_ref[0, 0, pl.ds(start, nb * tq), :]
        for h in heads:
            z = _dot_nt(qms[h], ks)
            sp = jnp.maximum(z, 0.0) + jnp.log(1.0 + jnp.exp(-jnp.abs(z)))
            log_sig = z - sp
            parts = [sp[:, i * tq:(i + 1) * tq] for i in range(nb)]
            if diagonal:
                parts[-1] = jnp.where(past, parts[-1], 0.0)
            after = _dot(jnp.concatenate(parts, axis=0).astype(BF16), u_ref[...])
            r = r_ref[h][...]
            ws = [None] * nb
            for i in reversed(range(nb)):
                tot = after[i * tq:(i + 1) * tq] + r
                ws[i] = jnp.exp(log_sig[:, i * tq:(i + 1) * tq] + tot)
                r = tot[:, 0:1] - parts[i][:, 0:1]
            if diagonal:
                ws[-1] = jnp.where(past, ws[-1], 0.0)
            r_ref[h][...] = r
            acc_ref[h][...] += _dot(jnp.concatenate(ws, axis=1).astype(BF16), vs)

    for h in heads:
        r_ref[h][...] = jnp.zeros((tq, 1), F32)
        acc_ref[h][...] = jnp.zeros((tq, V7X_LANES), F32)

    n_groups = qi // SB_GROUP
    for in_group in range(SB_GROUP):
        @pl.when(qi % SB_GROUP == in_group)
        def _(in_group=in_group):
            span(n_groups * SB_GROUP, in_group + 1, True)

    def body(s, carry):
        span((n_groups - 1 - s) * SB_GROUP, SB_GROUP, False)
        return carry

    lax.fori_loop(0, n_groups, body, 0)

    o_ref[0] = jnp.where(lane // HEAD_DIM == 0, acc0_ref[...], acc1_ref[...]).astype(o_ref.dtype)


def stick_breaking_attention(qkv, b, s, tq):
    d = qkv.shape[-1]
    qkv4 = qkv.reshape(3, b, s, d)
    return pl.pallas_call(
        functools.partial(_sb_kernel, tq=tq),
        grid=(b, d // V7X_LANES, s // tq),
        in_specs=[pl.BlockSpec((1, 1, tq, V7X_LANES), lambda bi, hp, qi: (0, bi, qi, hp)),
                  pl.BlockSpec((1, 1, s, V7X_LANES), lambda bi, hp, qi: (1, bi, 0, hp)),
                  pl.BlockSpec((1, 1, s, V7X_LANES), lambda bi, hp, qi: (2, bi, 0, hp))],
        out_specs=pl.BlockSpec((1, tq, V7X_LANES), lambda bi, hp, qi: (bi, qi, hp)),
        out_shape=jax.ShapeDtypeStruct((b, s, d), BF16),
        scratch_shapes=[pltpu.VMEM((tq, V7X_LANES), F32), pltpu.VMEM((tq, V7X_LANES), F32),
                        pltpu.VMEM((tq, 1), F32), pltpu.VMEM((tq, 1), F32),
                        pltpu.VMEM((tq, tq), BF16)],
        compiler_params=_cparams(("arbitrary", "arbitrary", "arbitrary"), 32),
        name="stick_breaking",
    )(qkv4, qkv4, qkv4)


def _t5_bias_kernel(rb_ref, o_ref, *, tq):
    h = pl.program_id(0)
    row = lax.broadcasted_iota(jnp.int32, (tq, tq), 0)
    col = lax.broadcasted_iota(jnp.int32, (tq, tq), 1)
    max_exact = N_BUCKETS // 2
    for o in range(2):
        dist = o * tq + row - col
        n = jnp.maximum(dist, 0)
        nf = jnp.maximum(n, 1).astype(F32)
        large = max_exact + (jnp.log(nf / max_exact) / math.log(MAX_DISTANCE / max_exact)
                             * (N_BUCKETS - max_exact)).astype(jnp.int32)
        large = jnp.minimum(large, N_BUCKETS - 1)
        bucket = jnp.where(n < max_exact, n, large)
        bias = jnp.zeros((tq, tq), F32)
        for bkt in range(N_BUCKETS):
            bias = jnp.where(bucket == bkt, rb_ref[bkt, h], bias)
        if o == 0:
            bias = jnp.where(dist >= 0, bias, NEG_BIG)
        o_ref[0, o] = bias


def t5_bias_tiles(rel_bias, tq):
    return pl.pallas_call(
        functools.partial(_t5_bias_kernel, tq=tq),
        grid=(N_HEADS,),
        in_specs=[pl.BlockSpec(memory_space=pltpu.SMEM)],
        out_specs=pl.BlockSpec((1, 2, tq, tq), lambda h: (h, 0, 0, 0)),
        out_shape=jax.ShapeDtypeStruct((N_HEADS, 2, tq, tq), F32),
        compiler_params=_cparams(("arbitrary",), 16),
        name="t5_bias_tiles",
    )(rel_bias)


def _moba_kernel(rb_ref, q_ref, k_ref, v_ref, bt_ref, o_ref,
                 s_ref, pick_ref, mx_ref, l_ref, acc_ref, km_ref, *, tq, nblk):
    hp = pl.program_id(1)
    qi = pl.program_id(2)

    @pl.when(qi == 0)
    def _():
        for n in range(nblk):
            kb = k_ref[0, 0, n * tq:(n + 1) * tq, :].astype(F32)
            km_ref[n:n + 1, :] = jnp.mean(kb, axis=0, keepdims=True)

    q = q_ref[0, 0]
    lane = lax.broadcasted_iota(jnp.int32, (tq, V7X_LANES), 1)
    blk = lax.broadcasted_iota(jnp.int32, (nblk, tq), 0)
    km = km_ref[...]
    km_hi = km.astype(BF16)
    km_lo = (km - km_hi.astype(F32)).astype(BF16)
    n_sel = min(MOBA_TOPK, nblk - 1)

    heads = range(HEADS_PER_LANE_BLOCK)
    qms = [jnp.where(lane // HEAD_DIM == h, q, jnp.zeros_like(q)) for h in heads]
    both = lambda x: jnp.concatenate([x, x], axis=1)
    halves = lambda x: (x[:, :V7X_LANES], x[:, V7X_LANES:])

    big = jnp.asarray(-NEG_BIG, BF16).astype(F32)
    spread_row = lax.broadcasted_iota(jnp.int32, (V7X_LANES, V7X_LANES), 0)
    no_rows = jnp.zeros((V7X_LANES - nblk, tq), F32)
    for h in heads:
        gate = _dot_nt(km_hi, qms[h]) + _dot_nt(km_lo, qms[h])
        gate = jnp.where(blk < qi, gate, NEG_BIG)
        rows = [gate[n:n + 1, :] for n in range(nblk)]
        rank = [jnp.zeros((1, tq), F32) for _ in range(nblk)]
        for n in range(nblk):
            for m in range(n):
                m_wins = jnp.where(rows[m] >= rows[n], 1.0, 0.0)
                rank[n] = rank[n] + m_wins
                rank[m] = rank[m] + (1.0 - m_wins)
        chosen = [jnp.where((rank[n] < n_sel) & (n < qi), 1.0, 0.0) for n in range(nblk)]
        pick_ref[h] = jnp.concatenate(chosen + [no_rows], axis=0).T.astype(BF16)
        mx_ref[h] = jnp.full((tq, V7X_LANES), NEG_BIG, F32)

    def block_masks(n, bias=None):
        spread_n = jnp.where(spread_row == n, big, 0.0).astype(BF16)
        masks = []
        for h in heads:
            m = _dot(pick_ref[h], spread_n) - big
            masks.append(both(m if bias is None else m + bias[h]))
        return masks

    far_bias = [rb_ref[N_BUCKETS - 1, hp * HEADS_PER_LANE_BLOCK + h] for h in heads]

    def logits(n, bias_of_head):
        start = pl.multiple_of(n * tq, tq)
        ks = k_ref[0, 0, pl.ds(start, tq), :]
        for h in heads:
            s = _dot_nt(qms[h], ks) + bias_of_head(h)
            s_ref[h, :, pl.ds(start, tq)] = s
            s_lo, s_hi = halves(s)
            mx_ref[h] = jnp.maximum(mx_ref[h], jnp.maximum(s_lo, s_hi))

    logits(qi, lambda h: bt_ref[h, 0])

    def far_body(n, carry):
        masks = block_masks(n, far_bias)
        logits(n, lambda h: masks[h])
        return carry

    lax.fori_loop(0, qi - 1, far_body, 0)

    @pl.when(qi >= 1)
    def _():
        masks = block_masks(qi - 1)
        logits(qi - 1, lambda h: bt_ref[h, 1] + masks[h])

    for h in heads:
        mx_ref[h] = jnp.broadcast_to(jnp.max(mx_ref[h], axis=1, keepdims=True), (tq, V7X_LANES))
        l_ref[h] = jnp.zeros((tq, V7X_LANES), F32)
        acc_ref[h] = jnp.zeros((tq, V7X_LANES), F32)

    def prob_body(n, carry):
        start = pl.multiple_of(n * tq, tq)
        vs = v_ref[0, 0, pl.ds(start, tq), :]
        for h in heads:
            p = jnp.exp(s_ref[h, :, pl.ds(start, tq)] - both(mx_ref[h]))
            p_lo, p_hi = halves(p)
            l_ref[h] += p_lo + p_hi
            acc_ref[h] += _dot(p.astype(BF16), vs)
        return carry

    lax.fori_loop(0, qi + 1, prob_body, 0)

    out = [acc_ref[h] / jnp.sum(l_ref[h], axis=1, keepdims=True) for h in heads]
    o_ref[0] = jnp.where(lane // HEAD_DIM == 0, out[0], out[1]).astype(o_ref.dtype)


def moba_attention(qkv, rel_bias, b, s):
    d = qkv.shape[-1]
    tq = MOBA_BLOCK
    assert s % tq == 0 and tq >= 2 * MAX_DISTANCE
    nblk = s // tq
    qkv4 = qkv.reshape(3, b, s, d)
    tiles = t5_bias_tiles(rel_bias, tq)
    return pl.pallas_call(
        functools.partial(_moba_kernel, tq=tq, nblk=nblk),
        grid=(b, d // V7X_LANES, nblk),
        in_specs=[pl.BlockSpec(memory_space=pltpu.SMEM),
                  pl.BlockSpec((1, 1, tq, V7X_LANES), lambda bi, hp, qi: (0, bi, qi, hp)),
                  pl.BlockSpec((1, 1, s, V7X_LANES), lambda bi, hp, qi: (1, bi, 0, hp)),
                  pl.BlockSpec((1, 1, s, V7X_LANES), lambda bi, hp, qi: (2, bi, 0, hp)),
                  pl.BlockSpec((HEADS_PER_LANE_BLOCK, 2, tq, tq), lambda bi, hp, qi: (hp, 0, 0, 0))],
        out_specs=pl.BlockSpec((1, tq, V7X_LANES), lambda bi, hp, qi: (bi, qi, hp)),
        out_shape=jax.ShapeDtypeStruct((b, s, d), BF16),
        scratch_shapes=[pltpu.VMEM((HEADS_PER_LANE_BLOCK, tq, s), F32),
                        pltpu.VMEM((HEADS_PER_LANE_BLOCK, tq, V7X_LANES), BF16),
                        pltpu.VMEM((HEADS_PER_LANE_BLOCK, tq, V7X_LANES), F32),
                        pltpu.VMEM((HEADS_PER_LANE_BLOCK, tq, V7X_LANES), F32),
                        pltpu.VMEM((HEADS_PER_LANE_BLOCK, tq, V7X_LANES), F32),
                        pltpu.VMEM((nblk, V7X_LANES), F32)],
        compiler_params=_cparams(("arbitrary", "arbitrary", "arbitrary"), 32),
        name="moba",
    )(rel_bias, qkv4, qkv4, qkv4, tiles)


def _oproj_kernel(o_ref, w_ref, h_ref, out_ref):
    out_ref[...] = h_ref[...] + _dot(o_ref[...], w_ref[...])


def out_proj(o, w_bf16, h, tm):
    t, d = h.shape
    return pl.pallas_call(
        _oproj_kernel,
        grid=(t // tm,),
        in_specs=[pl.BlockSpec((tm, d), lambda i: (i, 0)),
                  pl.BlockSpec((d, d), lambda i: (0, 0)),
                  pl.BlockSpec((tm, d), lambda i: (i, 0))],
        out_specs=pl.BlockSpec((tm, d), lambda i: (i, 0)),
        out_shape=jax.ShapeDtypeStruct((t, d), F32),
        compiler_params=_cparams(("arbitrary",), 40),
        name="out_proj",
    )(o, w_bf16, h)


def _ffn_kernel(h_ref, g_ref, w1_ref, w3_ref, w2_ref, out_ref, *, chunks):
    x = h_ref[...]
    hn = _rmsnorm_f32(x, g_ref[...]).astype(BF16)
    acc = x
    for c0, c1 in chunks:
        a = _dot(hn, w1_ref[:, c0:c1])
        b = _dot(hn, w3_ref[:, c0:c1])
        acc = acc + _dot((_silu(a) * b).astype(BF16), w2_ref[c0:c1, :])
    out_ref[...] = acc


def dense_ffn(h, g, w1, w3, w2, tm, chunk):
    t, d = h.shape
    f = w1.shape[1]
    chunks = tuple((c, min(c + chunk, f)) for c in range(0, f, chunk))
    whole = lambda shape: pl.BlockSpec(shape, lambda i: (0, 0), pipeline_mode=pl.Buffered(1))
    return pl.pallas_call(
        functools.partial(_ffn_kernel, chunks=chunks),
        grid=(t // tm,),
        in_specs=[pl.BlockSpec((tm, d), lambda i: (i, 0)),
                  pl.BlockSpec((1, d), lambda i: (0, 0)),
                  whole((d, f)), whole((d, f)), whole((f, d))],
        out_specs=pl.BlockSpec((tm, d), lambda i: (i, 0)),
        out_shape=jax.ShapeDtypeStruct((t, d), F32),
        compiler_params=_cparams(("arbitrary",), 56),
        name="dense_ffn",
    )(h, g.reshape(1, d), w1, w3, w2)


def _row_tile_chunks(d):
    assert d % V7X_LANES == 0
    return d // V7X_LANES


def _store_row_tiles(ref, x):
    n, d = x.shape
    c = _row_tile_chunks(d)
    for k in range(c):
        ref[pl.ds(k, n, stride=c), :] = x[:, k * V7X_LANES:(k + 1) * V7X_LANES]


def _load_row_tiles(ref, n, d):
    c = _row_tile_chunks(d)
    return [ref[pl.ds(k, n, stride=c), :] for k in range(c)]


COL_E0, COL_E1, COL_W0, COL_W1, COL_R0, COL_R1 = range(6)


def _router_kernel(h_ref, g_ref, wr_ref, hn_ref, slab_ref, cnt_ref, carry_ref, *, tm):
    i = pl.program_id(0)

    @pl.when(i == 0)
    def _():
        carry_ref[...] = jnp.zeros_like(carry_ref)

    hn = _rmsnorm_f32(h_ref[...], g_ref[...])
    _store_row_tiles(hn_ref, hn)
    hi = hn.astype(BF16)
    lo = (hn - hi.astype(F32)).astype(BF16)
    w = wr_ref[...]
    w_hi = w.astype(BF16)
    w_lo = (w - w_hi.astype(F32)).astype(BF16)
    logits = _dot(hi, w_hi) + _dot(hi, w_lo) + _dot(lo, w_hi)

    lane = lax.broadcasted_iota(jnp.int32, (tm, V7X_LANES), 1)
    neg_inf = jnp.float32(-jnp.inf)
    lg = jnp.where(lane < N_EXPERTS, logits, neg_inf)
    m0 = jnp.max(lg, axis=1, keepdims=True)
    i0 = jnp.min(jnp.where(lg == m0, lane, V7X_LANES), axis=1, keepdims=True)
    lg1 = jnp.where(lane == i0, neg_inf, lg)
    m1 = jnp.max(lg1, axis=1, keepdims=True)
    i1 = jnp.min(jnp.where(lg1 == m1, lane, V7X_LANES), axis=1, keepdims=True)
    e = jnp.exp(m1 - m0)
    w0 = 1.0 / (1.0 + e)
    w1 = e / (1.0 + e)

    pick0 = lane == i0
    pick1 = lane == i1
    hot = (pick0 | pick1).astype(BF16)
    row = lax.broadcasted_iota(jnp.int32, (tm, tm), 0)
    col = lax.broadcasted_iota(jnp.int32, (tm, tm), 1)
    before = _dot((col < row).astype(BF16), hot) + carry_ref[0:1, :]
    r0 = jnp.sum(jnp.where(pick0, before, 0.0), axis=1, keepdims=True)
    r1 = jnp.sum(jnp.where(pick1, before, 0.0), axis=1, keepdims=True)
    carry_ref[0:1, :] = carry_ref[0:1, :] + jnp.sum(hot.astype(F32), axis=0, keepdims=True)

    slab = jnp.zeros((tm, V7X_LANES), F32)
    for c, val in ((COL_E0, i0.astype(F32)), (COL_E1, i1.astype(F32)), (COL_W0, w0), (COL_W1, w1),
                   (COL_R0, r0), (COL_R1, r1)):
        slab = jnp.where(lane == c, val, slab)
    slab_ref[...] = slab
    cnt_ref[...] = carry_ref[...]


def moe_router(h, g, w_router, tm):
    t, d = h.shape
    wr = jnp.zeros((d, V7X_LANES), F32).at[:, :N_EXPERTS].set(w_router)
    return pl.pallas_call(
        functools.partial(_router_kernel, tm=tm),
        grid=(t // tm,),
        in_specs=[pl.BlockSpec((tm, d), lambda i: (i, 0)),
                  pl.BlockSpec((1, d), lambda i: (0, 0)),
                  pl.BlockSpec((d, V7X_LANES), lambda i: (0, 0))],
        out_specs=[pl.BlockSpec((tm * d // V7X_LANES, V7X_LANES), lambda i: (i, 0)),
                   pl.BlockSpec((tm, V7X_LANES), lambda i: (i, 0)),
                   pl.BlockSpec((8, V7X_LANES), lambda i: (0, 0))],
        out_shape=[jax.ShapeDtypeStruct((t * d // V7X_LANES, V7X_LANES), F32),
                   jax.ShapeDtypeStruct((t, V7X_LANES), F32),
                   jax.ShapeDtypeStruct((8, V7X_LANES), F32)],
        scratch_shapes=[pltpu.VMEM((8, V7X_LANES), F32)],
        compiler_params=_cparams(("arbitrary",), 40),
        name="moe_router",
    )(h, g.reshape(1, d), wr)


def _moe_kernel(te_ref, nu_ref, tokc_ref, tokn_ref, dst_ref, hn_hbm, w1_ref, w3_ref, w2_ref, y_hbm,
                xbuf, xb16, acc_ref, ybuf, gsem, ssem, *, tm, nf, nt, n_real):
    i = pl.program_id(0)
    j = pl.program_id(1)
    nu = nu_ref[0]
    slot = i % 2
    d = acc_ref.shape[1]
    c = _row_tile_chunks(d)

    def row_loop(start_row_copy):
        def body(g, carry):
            for k in range(ROW_DMA_UNROLL):
                start_row_copy(g * ROW_DMA_UNROLL + k)
            return carry
        lax.fori_loop(0, tm // ROW_DMA_UNROLL, body, 0)

    def row(r):
        return pl.ds(pl.multiple_of(r * c, c), c)

    def start_gather(tok_ref, s):
        row_loop(lambda r: pltpu.make_async_copy(
            hn_hbm.at[row(tok_ref[0, 0, r]), :], xbuf.at[s, row(r), :], gsem.at[s]).start())

    def wait_gather(s):
        pltpu.make_async_copy(hn_hbm.at[pl.ds(0, tm * c), :], xbuf.at[s], gsem.at[s]).wait()

    def start_scatter():
        row_loop(lambda r: pltpu.make_async_copy(
            ybuf.at[row(r), :], y_hbm.at[row(dst_ref[0, 0, r]), :], ssem.at[0]).start())

    def spare_rows_copy():
        return pltpu.make_async_copy(ybuf, y_hbm.at[pl.ds(n_real * c, tm * c), :], ssem.at[0])

    def wait_scatter():
        spare_rows_copy().wait()

    @pl.when((j == 0) & (i == 0))
    def _():
        ybuf[...] = jnp.zeros_like(ybuf)
        spare_rows_copy().start()
        spare_rows_copy().wait()

        @pl.when(nu > 0)
        def _():
            start_gather(tokc_ref, 0)

    @pl.when((j == 0) & (i < nu))
    def _():
        wait_gather(slot)
        for k, chunk in enumerate(_load_row_tiles(xbuf.at[slot], tm, d)):
            xb16[:, k * V7X_LANES:(k + 1) * V7X_LANES] = chunk.astype(BF16)

    @pl.when((j == 0) & (i + 1 < nu))
    def _():
        start_gather(tokn_ref, 1 - slot)

    @pl.when(i < nu)
    def _():
        x = xb16[...]
        a = _dot(x, w1_ref[0])
        b = _dot(x, w3_ref[0])
        part = _dot((_silu(a) * b).astype(BF16), w2_ref[0])

        @pl.when(j == 0)
        def _():
            acc_ref[...] = part

        @pl.when(j > 0)
        def _():
            acc_ref[...] += part

    last = j == nf - 1

    @pl.when(last & (i >= 1) & (i - 1 < nu))
    def _():
        wait_scatter()

    @pl.when(last & (i < nu))
    def _():
        _store_row_tiles(ybuf, acc_ref[...])
        start_scatter()

    @pl.when(last & (i == nt - 1) & (i < nu))
    def _():
        wait_scatter()


def moe_experts(hn, slab, counts, w1, w3, w2, tm, tf):
    t = slab.shape[0]
    d = w1.shape[1]
    c = _row_tile_chunks(d)
    f = w1.shape[2]
    nf = f // tf
    nt = (2 * t) // tm + N_EXPERTS
    rows = nt * tm

    e0 = slab[:, COL_E0].astype(jnp.int32)
    e1 = slab[:, COL_E1].astype(jnp.int32)
    r0 = slab[:, COL_R0].astype(jnp.int32)
    r1 = slab[:, COL_R1].astype(jnp.int32)
    cnt = counts[0, :N_EXPERTS].astype(jnp.int32)
    tiles = (cnt + tm - 1) // tm
    tile_end = jnp.cumsum(tiles)
    tile_start = tile_end - tiles
    nu = tile_end[-1:]
    offs = tile_start * tm
    dest0 = offs[e0] + r0
    dest1 = offs[e1] + r1
    tile_expert = jnp.minimum(
        jnp.sum(jnp.arange(nt, dtype=jnp.int32)[:, None] >= tile_end[None, :], axis=1), N_EXPERTS - 1
    ).astype(jnp.int32)
    tok_ids = jnp.arange(t, dtype=jnp.int32)
    spare = 2 * t + (jnp.arange(rows, dtype=jnp.int32) % tm)
    dst_sorted = spare.at[jnp.concatenate([dest0, dest1])].set(jnp.concatenate([tok_ids, t + tok_ids]))
    tok_sorted = jnp.where(dst_sorted >= 2 * t, 0, jnp.where(dst_sorted >= t, dst_sorted - t, dst_sorted))
    tok3 = tok_sorted.reshape(nt, 1, tm)
    dst3 = dst_sorted.reshape(nt, 1, tm)

    def w_in(shape, which):
        def index_map(i, j, te, nu_):
            ii = jnp.minimum(i, nu_[0] - 1)
            jj = jnp.where(i < nu_[0], j, nf - 1)
            return (te[ii], 0, jj) if which == "up" else (te[ii], jj, 0)
        return pl.BlockSpec(shape, index_map)

    smem_tile = lambda f_: pl.BlockSpec((1, 1, tm), f_, memory_space=pltpu.SMEM)
    grid_spec = pltpu.PrefetchScalarGridSpec(
        num_scalar_prefetch=2,
        grid=(nt, nf),
        in_specs=[smem_tile(lambda i, j, te, nu_: (i, 0, 0)),
                  smem_tile(lambda i, j, te, nu_: (jnp.minimum(i + 1, nt - 1), 0, 0)),
                  smem_tile(lambda i, j, te, nu_: (i, 0, 0)),
                  pl.BlockSpec(memory_space=pl.ANY),
                  w_in((1, d, tf), "up"), w_in((1, d, tf), "up"), w_in((1, tf, d), "down")],
        out_specs=pl.BlockSpec(memory_space=pl.ANY),
        scratch_shapes=[pltpu.VMEM((2, tm * c, V7X_LANES), F32),
                        pltpu.VMEM((tm, d), BF16),
                        pltpu.VMEM((tm, d), F32),
                        pltpu.VMEM((tm * c, V7X_LANES), F32),
                        pltpu.SemaphoreType.DMA((2,)),
                        pltpu.SemaphoreType.DMA((1,))],
    )
    return pl.pallas_call(
        functools.partial(_moe_kernel, tm=tm, nf=nf, nt=nt, n_real=2 * t),
        grid_spec=grid_spec,
        out_shape=jax.ShapeDtypeStruct(((2 * t + tm) * c, V7X_LANES), F32),
        compiler_params=_cparams(("arbitrary", "arbitrary"), 56),
        name="moe_experts",
    )(tile_expert, nu, tok3, tok3, dst3, hn, w1, w3, w2)


def _combine_kernel(h_ref, y0_ref, y1_ref, slab_ref, g_ref, out_ref, *, final):
    slab = slab_ref[...]
    tm, d = h_ref.shape
    y0 = jnp.concatenate(_load_row_tiles(y0_ref, tm, d), axis=1)
    y1 = jnp.concatenate(_load_row_tiles(y1_ref, tm, d), axis=1)
    x = h_ref[...] + slab[:, COL_W0:COL_W0 + 1] * y0 + slab[:, COL_W1:COL_W1 + 1] * y1
    out_ref[...] = _rmsnorm_f32(x, g_ref[...]) if final else x


def moe_combine(h, y, slab, g, tm, final):
    t, d = h.shape
    nb = t // tm
    c = _row_tile_chunks(d)
    return pl.pallas_call(
        functools.partial(_combine_kernel, final=final),
        grid=(nb,),
        in_specs=[pl.BlockSpec((tm, d), lambda i: (i, 0)),
                  pl.BlockSpec((tm * c, V7X_LANES), lambda i: (i, 0)),
                  pl.BlockSpec((tm * c, V7X_LANES), lambda i: (i + nb, 0)),
                  pl.BlockSpec((tm, V7X_LANES), lambda i: (i, 0)),
                  pl.BlockSpec((1, d), lambda i: (0, 0))],
        out_specs=pl.BlockSpec((tm, d), lambda i: (i, 0)),
        out_shape=jax.ShapeDtypeStruct((t, d), F32),
        compiler_params=_cparams(("arbitrary",), 40),
        name="moe_combine",
    )(h, y, y, slab, g.reshape(1, d))


def _norm_kernel(h_ref, g_ref, out_ref):
    out_ref[...] = _rmsnorm_f32(h_ref[...], g_ref[...])


def final_norm_only(h, g, tm):
    t, d = h.shape
    return pl.pallas_call(
        _norm_kernel,
        grid=(t // tm,),
        in_specs=[pl.BlockSpec((tm, d), lambda i: (i, 0)), pl.BlockSpec((1, d), lambda i: (0, 0))],
        out_specs=pl.BlockSpec((tm, d), lambda i: (i, 0)),
        out_shape=jax.ShapeDtypeStruct((t, d), F32),
        compiler_params=_cparams(("arbitrary",), 40),
        name="final_norm",
    )(h, g.reshape(1, d))


def _row_tile(t, want):
    tm = min(want, t)
    assert t % tm == 0
    return tm


def kernel(x, w_qkv, w_o, mixer_norm, ffn_norm, rel_bias, w1, w3, w2, router, e_w1, e_w3, e_w2, final_norm):
    b, s, d = x.shape
    assert d == N_HEADS * HEAD_DIM and s % MOBA_BLOCK == 0
    t = b * s
    depth = w_qkv.shape[0]
    h = x.reshape(t, d)
    tm_big = _row_tile(t, 1024)
    tm_mid = _row_tile(t, 512)
    normed = False
    for i in range(depth):
        qkv = qkv_proj(h, mixer_norm[i], w_qkv[i].astype(BF16), tm_big)
        if i % 2 == 0:
            o = stick_breaking_attention(qkv, b, s, MOBA_BLOCK)
        else:
            o = moba_attention(qkv, rel_bias, b, s)
        h = out_proj(o.reshape(t, d), w_o[i].astype(BF16), h, tm_mid)
        jj = i // 2
        if i % 2 == 0:
            h = dense_ffn(h, ffn_norm[i], w1[jj].astype(BF16), w3[jj].astype(BF16), w2[jj].astype(BF16),
                          tm_mid, 1024)
        else:
            hn, slab, counts = moe_router(h, ffn_norm[i], router[jj], tm_mid)
            f_e = e_w1.shape[-1]
            tf = f_e // 2 if (f_e // 2) % 256 == 0 else f_e
            y = moe_experts(hn, slab, counts, e_w1[jj].astype(BF16), e_w3[jj].astype(BF16),
                            e_w2[jj].astype(BF16), tm_mid, tf)
            last = i == depth - 1
            h = moe_combine(h, y, slab, final_norm if last else ffn_norm[i], tm_mid, last)
            normed = last
    if not normed:
        h = final_norm_only(h, final_norm, tm_mid)
    return h.reshape(b, s, d)
```

```python
import functools
import math

import jax
import jax.numpy as jnp
from jax import lax
from jax.experimental import pallas as pl
from jax.experimental.pallas import tpu as pltpu

N_HEADS = 16
HEAD_DIM = 64
MOBA_BLOCK = 256
MOBA_TOPK = 3
N_BUCKETS = 32
MAX_DISTANCE = 128
N_EXPERTS = 8
RMS_EPS = 1e-6
NEG_BIG = -1e30

V7X_LANES = 128
ROW_DMA_UNROLL = 8
SB_GROUP = 4
V7X_VMEM_BYTES = 64 * 1024 * 1024
HEADS_PER_LANE_BLOCK = V7X_LANES // HEAD_DIM

F32 = jnp.float32
BF16 = jnp.bfloat16


def _cparams(semantics, vmem_mb):
    assert vmem_mb * 1024 * 1024 < V7X_VMEM_BYTES
    return pltpu.CompilerParams(dimension_semantics=semantics,
                                vmem_limit_bytes=vmem_mb * 1024 * 1024)


def _rmsnorm_f32(x, g):
    return x * lax.rsqrt(jnp.mean(x * x, axis=-1, keepdims=True) + RMS_EPS) * g


def _silu(a):
    return a * (1.0 / (1.0 + jnp.exp(-a)))


def _dot(a, b):
    return jnp.dot(a, b, preferred_element_type=F32)


def _dot_nt(a, b):
    return lax.dot_general(a, b, (((1,), (1,)), ((), ())), preferred_element_type=F32)


def _qkv_kernel(x_ref, g_ref, w_ref, o_ref, hn_ref):
    j = pl.program_id(1)

    @pl.when(j == 0)
    def _():
        hn_ref[...] = _rmsnorm_f32(x_ref[...], g_ref[...]).astype(BF16)

    scale = jnp.where(j == 0, HEAD_DIM ** -0.5, 1.0)
    o_ref[0] = (_dot(hn_ref[...], w_ref[...]) * scale).astype(o_ref.dtype)


def qkv_proj(h, g, w_bf16, tm):
    t, d = h.shape
    return pl.pallas_call(
        _qkv_kernel,
        grid=(t // tm, 3),
        in_specs=[pl.BlockSpec((tm, d), lambda i, j: (i, 0)),
                  pl.BlockSpec((1, d), lambda i, j: (0, 0)),
                  pl.BlockSpec((d, d), lambda i, j: (0, j))],
        out_specs=pl.BlockSpec((1, tm, d), lambda i, j: (j, i, 0)),
        out_shape=jax.ShapeDtypeStruct((3, t, d), BF16),
        scratch_shapes=[pltpu.VMEM((tm, d), BF16)],
        compiler_params=_cparams(("arbitrary", "arbitrary"), 40),
        name="qkv_proj",
    )(h, g.reshape(1, d), w_bf16)


def _sb_kernel(q_ref, k_ref, v_ref, o_ref, acc0_ref, acc1_ref, r0_ref, r1_ref, u_ref, *, tq):
    acc_ref = (acc0_ref, acc1_ref)
    r_ref = (r0_ref, r1_ref)
    qi = pl.program_id(2)
    row = lax.broadcasted_iota(jnp.int32, (tq, tq), 0)
    col = lax.broadcasted_iota(jnp.int32, (tq, tq), 1)

    @pl.when(qi == 0)
    def _():
        u_ref[...] = -(row > col).astype(BF16)

    q = q_ref[0, 0]
    lane = lax.broadcasted_iota(jnp.int32, (tq, V7X_LANES), 1)
    past = col < row
    heads = range(HEADS_PER_LANE_BLOCK)
    qms = [jnp.where(lane // HEAD_DIM == h, q, jnp.zeros_like(q)) for h in heads]

    def span(first, nb, diagonal):
        tasks = [(i, h) for i in reversed(range(nb)) for h in heads]
        state = [dict() for _ in tasks]
        r = [r_ref[h][...] for h in heads]
        acc = [None for h in heads]

        def stage(k, st, i, h):
            on_diagonal = diagonal and i == nb - 1
            start = pl.multiple_of((first + i) * tq, tq)
            if k == 0:
                st["z"] = _dot_nt(qms[h], k_ref[0, 0, pl.ds(start, tq), :])
            elif k == 1:
                z = st.pop("z")
                zb = z.astype(BF16)
                t = jnp.log(1 + jnp.exp(-jnp.abs(zb)))
                sp = jnp.maximum(zb, 0) + t
                if on_diagonal:
                    sp = jnp.where(past, sp, jnp.zeros_like(sp))
                st["sp"] = sp
                st["log_sig"] = jnp.minimum(z, 0.0) - t.astype(F32)
            elif k == 2:
                st["after"] = _dot(st["sp"], u_ref[...])
            elif k == 3:
                tot = st.pop("after") + r[h]
                w = jnp.exp(st.pop("log_sig") + tot)
                if on_diagonal:
                    w = jnp.where(past, w, 0.0)
                r[h] = tot[:, 0:1] - st.pop("sp")[:, 0:1].astype(F32)
                st["w"] = w.astype(BF16)
            else:
                part = _dot(st.pop("w"), v_ref[0, 0, pl.ds(start, tq), :])
                acc[h] = part if acc[h] is None else acc[h] + part

        n_stages = 5
        for step in range(len(tasks) + n_stages - 1):
            for k in reversed(range(n_stages)):
                ti = step - k
                if 0 <= ti < len(tasks):
                    stage(k, state[ti], *tasks[ti])
        for h in heads:
            r_ref[h][...] = r[h]
            acc_ref[h][...] += acc[h]

    for h in heads:
        r_ref[h][...] = jnp.zeros((tq, 1), F32)
        acc_ref[h][...] = jnp.zeros((tq, V7X_LANES), F32)

    n_groups = qi // SB_GROUP
    for in_group in range(SB_GROUP):
        @pl.when(qi % SB_GROUP == in_group)
        def _(in_group=in_group):
            span(n_groups * SB_GROUP, in_group + 1, True)

    def body(s, carry):
        span((n_groups - 1 - s) * SB_GROUP, SB_GROUP, False)
        return carry

    lax.fori_loop(0, n_groups, body, 0)

    o_ref[0] = jnp.where(lane // HEAD_DIM == 0, acc0_ref[...], acc1_ref[...]).astype(o_ref.dtype)


def stick_breaking_attention(qkv, b, s, tq):
    d = qkv.shape[-1]
    qkv4 = qkv.reshape(3, b, s, d)
    return pl.pallas_call(
        functools.partial(_sb_kernel, tq=tq),
        grid=(b, d // V7X_LANES, s // tq),
        in_specs=[pl.BlockSpec((1, 1, tq, V7X_LANES), lambda bi, hp, qi: (0, bi, qi, hp)),
                  pl.BlockSpec((1, 1, s, V7X_LANES), lambda bi, hp, qi: (1, bi, 0, hp)),
                  pl.BlockSpec((1, 1, s, V7X_LANES), lambda bi, hp, qi: (2, bi, 0, hp))],
        out_specs=pl.BlockSpec((1, tq, V7X_LANES), lambda bi, hp, qi: (bi, qi, hp)),
        out_shape=jax.ShapeDtypeStruct((b, s, d), BF16),
        scratch_shapes=[pltpu.VMEM((tq, V7X_LANES), F32), pltpu.VMEM((tq, V7X_LANES), F32),
                        pltpu.VMEM((tq, 1), F32), pltpu.VMEM((tq, 1), F32),
                        pltpu.VMEM((tq, tq), BF16)],
        compiler_params=_cparams(("arbitrary", "arbitrary", "arbitrary"), 32),
        name="stick_breaking",
    )(qkv4, qkv4, qkv4)


def _t5_bias_kernel(rb_ref, o_ref, *, tq):
    h = pl.program_id(0)
    row = lax.broadcasted_iota(jnp.int32, (tq, tq), 0)
    col = lax.broadcasted_iota(jnp.int32, (tq, tq), 1)
    max_exact = N_BUCKETS // 2
    for o in range(2):
        dist = o * tq + row - col
        n = jnp.maximum(dist, 0)
        nf = jnp.maximum(n, 1).astype(F32)
        large = max_exact + (jnp.log(nf / max_exact) / math.log(MAX_DISTANCE / max_exact)
                             * (N_BUCKETS - max_exact)).astype(jnp.int32)
        large = jnp.minimum(large, N_BUCKETS - 1)
        bucket = jnp.where(n < max_exact, n, large)
        bias = jnp.zeros((tq, tq), F32)
        for bkt in range(N_BUCKETS):
            bias = jnp.where(bucket == bkt, rb_ref[bkt, h], bias)
        if o == 0:
            bias = jnp.where(dist >= 0, bias, NEG_BIG)
        o_ref[0, o] = bias


def t5_bias_tiles(rel_bias, tq):
    return pl.pallas_call(
        functools.partial(_t5_bias_kernel, tq=tq),
        grid=(N_HEADS,),
        in_specs=[pl.BlockSpec(memory_space=pltpu.SMEM)],
        out_specs=pl.BlockSpec((1, 2, tq, tq), lambda h: (h, 0, 0, 0)),
        out_shape=jax.ShapeDtypeStruct((N_HEADS, 2, tq, tq), F32),
        compiler_params=_cparams(("arbitrary",), 16),
        name="t5_bias_tiles",
    )(rel_bias)


def _moba_kernel(rb_ref, q_ref, k_ref, v_ref, bt_ref, o_ref,
                 s_ref, pick_ref, mx_ref, l_ref, acc_ref, km_ref, *, tq, nblk):
    hp = pl.program_id(1)
    qi = pl.program_id(2)

    @pl.when(qi == 0)
    def _():
        for n in range(nblk):
            kb = k_ref[0, 0, n * tq:(n + 1) * tq, :].astype(F32)
            km_ref[n:n + 1, :] = jnp.mean(kb, axis=0, keepdims=True)

    q = q_ref[0, 0]
    lane = lax.broadcasted_iota(jnp.int32, (tq, V7X_LANES), 1)
    blk = lax.broadcasted_iota(jnp.int32, (nblk, tq), 0)
    km = km_ref[...]
    km_hi = km.astype(BF16)
    km_lo = (km - km_hi.astype(F32)).astype(BF16)
    n_sel = min(MOBA_TOPK, nblk - 1)

    heads = range(HEADS_PER_LANE_BLOCK)
    qms = [jnp.where(lane // HEAD_DIM == h, q, jnp.zeros_like(q)) for h in heads]
    both = lambda x: jnp.concatenate([x, x], axis=1)
    halves = lambda x: (x[:, :V7X_LANES], x[:, V7X_LANES:])

    big = jnp.asarray(-NEG_BIG, BF16).astype(F32)
    spread_row = lax.broadcasted_iota(jnp.int32, (V7X_LANES, V7X_LANES), 0)
    no_rows = jnp.zeros((V7X_LANES - nblk, tq), F32)
    for h in heads:
        gate = _dot_nt(km_hi, qms[h]) + _dot_nt(km_lo, qms[h])
        gate = jnp.where(blk < qi, gate, NEG_BIG)
        rows = [gate[n:n + 1, :] for n in range(nblk)]
        rank = [jnp.zeros((1, tq), F32) for _ in range(nblk)]
        for n in range(nblk):
            for m in range(n):
                m_wins = jnp.where(rows[m] >= rows[n], 1.0, 0.0)
                rank[n] = rank[n] + m_wins
                rank[m] = rank[m] + (1.0 - m_wins)
        chosen = [jnp.where((rank[n] < n_sel) & (n < qi), 1.0, 0.0) for n in range(nblk)]
        pick_ref[h] = jnp.concatenate(chosen + [no_rows], axis=0).T.astype(BF16)
        mx_ref[h] = jnp.full((tq, V7X_LANES), NEG_BIG, F32)

    def block_masks(n, bias=None):
        spread_n = jnp.where(spread_row == n, big, 0.0).astype(BF16)
        masks = []
        for h in heads:
            m = _dot(pick_ref[h], spread_n) - big
            masks.append(both(m if bias is None else m + bias[h]))
        return masks

    far_bias = [rb_ref[N_BUCKETS - 1, hp * HEADS_PER_LANE_BLOCK + h] for h in heads]

    def logits(n, bias_of_head):
        start = pl.multiple_of(n * tq, tq)
        ks = k_ref[0, 0, pl.ds(start, tq), :]
        for h in heads:
            s = _dot_nt(qms[h], ks) + bias_of_head(h)
            s_ref[h, :, pl.ds(start, tq)] = s
            s_lo, s_hi = halves(s)
            mx_ref[h] = jnp.maximum(mx_ref[h], jnp.maximum(s_lo, s_hi))

    logits(qi, lambda h: bt_ref[h, 0])

    def far_body(n, carry):
        masks = block_masks(n, far_bias)
        logits(n, lambda h: masks[h])
        return carry

    lax.fori_loop(0, qi - 1, far_body, 0)

    @pl.when(qi >= 1)
    def _():
        masks = block_masks(qi - 1)
        logits(qi - 1, lambda h: bt_ref[h, 1] + masks[h])

    for h in heads:
        mx_ref[h] = jnp.broadcast_to(jnp.max(mx_ref[h], axis=1, keepdims=True), (tq, V7X_LANES))
        l_ref[h] = jnp.zeros((tq, V7X_LANES), F32)
        acc_ref[h] = jnp.zeros((tq, V7X_LANES), F32)

    def prob_body(n, carry):
        start = pl.multiple_of(n * tq, tq)
        vs = v_ref[0, 0, pl.ds(start, tq), :]
        for h in heads:
            p = jnp.exp(s_ref[h, :, pl.ds(start, tq)] - both(mx_ref[h]))
            p_lo, p_hi = halves(p)
            l_ref[h] += p_lo + p_hi
            acc_ref[h] += _dot(p.astype(BF16), vs)
        return carry

    lax.fori_loop(0, qi + 1, prob_body, 0)

    out = [acc_ref[h] / jnp.sum(l_ref[h], axis=1, keepdims=True) for h in heads]
    o_ref[0] = jnp.where(lane // HEAD_DIM == 0, out[0], out[1]).astype(o_ref.dtype)


def moba_attention(qkv, rel_bias, b, s):
    d = qkv.shape[-1]
    tq = MOBA_BLOCK
    assert s % tq == 0 and tq >= 2 * MAX_DISTANCE
    nblk = s // tq
    qkv4 = qkv.reshape(3, b, s, d)
    tiles = t5_bias_tiles(rel_bias, tq)
    return pl.pallas_call(
        functools.partial(_moba_kernel, tq=tq, nblk=nblk),
        grid=(b, d // V7X_LANES, nblk),
        in_specs=[pl.BlockSpec(memory_space=pltpu.SMEM),
                  pl.BlockSpec((1, 1, tq, V7X_LANES), lambda bi, hp, qi: (0, bi, qi, hp)),
                  pl.BlockSpec((1, 1, s, V7X_LANES), lambda bi, hp, qi: (1, bi, 0, hp)),
                  pl.BlockSpec((1, 1, s, V7X_LANES), lambda bi, hp, qi: (2, bi, 0, hp)),
                  pl.BlockSpec((HEADS_PER_LANE_BLOCK, 2, tq, tq), lambda bi, hp, qi: (hp, 0, 0, 0))],
        out_specs=pl.BlockSpec((1, tq, V7X_LANES), lambda bi, hp, qi: (bi, qi, hp)),
        out_shape=jax.ShapeDtypeStruct((b, s, d), BF16),
        scratch_shapes=[pltpu.VMEM((HEADS_PER_LANE_BLOCK, tq, s), F32),
                        pltpu.VMEM((HEADS_PER_LANE_BLOCK, tq, V7X_LANES), BF16),
                        pltpu.VMEM((HEADS_PER_LANE_BLOCK, tq, V7X_LANES), F32),
                        pltpu.VMEM((HEADS_PER_LANE_BLOCK, tq, V7X_LANES), F32),
                        pltpu.VMEM((HEADS_PER_LANE_BLOCK, tq, V7X_LANES), F32),
                        pltpu.VMEM((nblk, V7X_LANES), F32)],
        compiler_params=_cparams(("arbitrary", "arbitrary", "arbitrary"), 32),
        name="moba",
    )(rel_bias, qkv4, qkv4, qkv4, tiles)


def _oproj_kernel(o_ref, w_ref, h_ref, out_ref):
    out_ref[...] = h_ref[...] + _dot(o_ref[...], w_ref[...])


def out_proj(o, w_bf16, h, tm):
    t, d = h.shape
    return pl.pallas_call(
        _oproj_kernel,
        grid=(t // tm,),
        in_specs=[pl.BlockSpec((tm, d), lambda i: (i, 0)),
                  pl.BlockSpec((d, d), lambda i: (0, 0)),
                  pl.BlockSpec((tm, d), lambda i: (i, 0))],
        out_specs=pl.BlockSpec((tm, d), lambda i: (i, 0)),
        out_shape=jax.ShapeDtypeStruct((t, d), F32),
        compiler_params=_cparams(("arbitrary",), 40),
        name="out_proj",
    )(o, w_bf16, h)


def _ffn_kernel(h_ref, g_ref, w1_ref, w3_ref, w2_ref, out_ref, *, chunks):
    x = h_ref[...]
    hn = _rmsnorm_f32(x, g_ref[...]).astype(BF16)
    acc = x
    for c0, c1 in chunks:
        a = _dot(hn, w1_ref[:, c0:c1])
        b = _dot(hn, w3_ref[:, c0:c1])
        acc = acc + _dot((_silu(a) * b).astype(BF16), w2_ref[c0:c1, :])
    out_ref[...] = acc


def dense_ffn(h, g, w1, w3, w2, tm, chunk):
    t, d = h.shape
    f = w1.shape[1]
    chunks = tuple((c, min(c + chunk, f)) for c in range(0, f, chunk))
    whole = lambda shape: pl.BlockSpec(shape, lambda i: (0, 0), pipeline_mode=pl.Buffered(1))
    return pl.pallas_call(
        functools.partial(_ffn_kernel, chunks=chunks),
        grid=(t // tm,),
        in_specs=[pl.BlockSpec((tm, d), lambda i: (i, 0)),
                  pl.BlockSpec((1, d), lambda i: (0, 0)),
                  whole((d, f)), whole((d, f)), whole((f, d))],
        out_specs=pl.BlockSpec((tm, d), lambda i: (i, 0)),
        out_shape=jax.ShapeDtypeStruct((t, d), F32),
        compiler_params=_cparams(("arbitrary",), 56),
        name="dense_ffn",
    )(h, g.reshape(1, d), w1, w3, w2)


def _row_tile_chunks(d):
    assert d % V7X_LANES == 0
    return d // V7X_LANES


def _store_row_tiles(ref, x):
    n, d = x.shape
    c = _row_tile_chunks(d)
    for k in range(c):
        ref[pl.ds(k, n, stride=c), :] = x[:, k * V7X_LANES:(k + 1) * V7X_LANES]


def _load_row_tiles(ref, n, d):
    c = _row_tile_chunks(d)
    return [ref[pl.ds(k, n, stride=c), :] for k in range(c)]


COL_E0, COL_E1, COL_W0, COL_W1, COL_R0, COL_R1 = range(6)


def _router_kernel(h_ref, g_ref, wr_ref, hn_ref, slab_ref, cnt_ref, carry_ref, *, tm):
    i = pl.program_id(0)

    @pl.when(i == 0)
    def _():
        carry_ref[...] = jnp.zeros_like(carry_ref)

    hn = _rmsnorm_f32(h_ref[...], g_ref[...])
    _store_row_tiles(hn_ref, hn)
    hi = hn.astype(BF16)
    lo = (hn - hi.astype(F32)).astype(BF16)
    w = wr_ref[...]
    w_hi = w.astype(BF16)
    w_lo = (w - w_hi.astype(F32)).astype(BF16)
    logits = _dot(hi, w_hi) + _dot(hi, w_lo) + _dot(lo, w_hi)

    lane = lax.broadcasted_iota(jnp.int32, (tm, V7X_LANES), 1)
    neg_inf = jnp.float32(-jnp.inf)
    lg = jnp.where(lane < N_EXPERTS, logits, neg_inf)
    m0 = jnp.max(lg, axis=1, keepdims=True)
    i0 = jnp.min(jnp.where(lg == m0, lane, V7X_LANES), axis=1, keepdims=True)
    lg1 = jnp.where(lane == i0, neg_inf, lg)
    m1 = jnp.max(lg1, axis=1, keepdims=True)
    i1 = jnp.min(jnp.where(lg1 == m1, lane, V7X_LANES), axis=1, keepdims=True)
    e = jnp.exp(m1 - m0)
    w0 = 1.0 / (1.0 + e)
    w1 = e / (1.0 + e)

    pick0 = lane == i0
    pick1 = lane == i1
    hot = (pick0 | pick1).astype(BF16)
    row = lax.broadcasted_iota(jnp.int32, (tm, tm), 0)
    col = lax.broadcasted_iota(jnp.int32, (tm, tm), 1)
    before = _dot((col < row).astype(BF16), hot) + carry_ref[0:1, :]
    r0 = jnp.sum(jnp.where(pick0, before, 0.0), axis=1, keepdims=True)
    r1 = jnp.sum(jnp.where(pick1, before, 0.0), axis=1, keepdims=True)
    carry_ref[0:1, :] = carry_ref[0:1, :] + jnp.sum(hot.astype(F32), axis=0, keepdims=True)

    slab = jnp.zeros((tm, V7X_LANES), F32)
    for c, val in ((COL_E0, i0.astype(F32)), (COL_E1, i1.astype(F32)), (COL_W0, w0), (COL_W1, w1),
                   (COL_R0, r0), (COL_R1, r1)):
        slab = jnp.where(lane == c, val, slab)
    slab_ref[...] = slab
    cnt_ref[...] = carry_ref[...]


def moe_router(h, g, w_router, tm):
    t, d = h.shape
    wr = jnp.zeros((d, V7X_LANES), F32).at[:, :N_EXPERTS].set(w_router)
    return pl.pallas_call(
        functools.partial(_router_kernel, tm=tm),
        grid=(t // tm,),
        in_specs=[pl.BlockSpec((tm, d), lambda i: (i, 0)),
                  pl.BlockSpec((1, d), lambda i: (0, 0)),
                  pl.BlockSpec((d, V7X_LANES), lambda i: (0, 0))],
        out_specs=[pl.BlockSpec((tm * d // V7X_LANES, V7X_LANES), lambda i: (i, 0)),
                   pl.BlockSpec((tm, V7X_LANES), lambda i: (i, 0)),
                   pl.BlockSpec((8, V7X_LANES), lambda i: (0, 0))],
        out_shape=[jax.ShapeDtypeStruct((t * d // V7X_LANES, V7X_LANES), F32),
                   jax.ShapeDtypeStruct((t, V7X_LANES), F32),
                   jax.ShapeDtypeStruct((8, V7X_LANES), F32)],
        scratch_shapes=[pltpu.VMEM((8, V7X_LANES), F32)],
        compiler_params=_cparams(("arbitrary",), 40),
        name="moe_router",
    )(h, g.reshape(1, d), wr)


def _moe_kernel(te_ref, nu_ref, tokc_ref, tokn_ref, dst_ref, hn_hbm, w1_ref, w3_ref, w2_ref, y_hbm,
                xbuf, xb16, acc_ref, ybuf, gsem, ssem, *, tm, nf, nt, n_real):
    i = pl.program_id(0)
    j = pl.program_id(1)
    nu = nu_ref[0]
    slot = i % 2
    d = acc_ref.shape[1]
    c = _row_tile_chunks(d)

    def row_loop(start_row_copy):
        def body(g, carry):
            for k in range(ROW_DMA_UNROLL):
                start_row_copy(g * ROW_DMA_UNROLL + k)
            return carry
        lax.fori_loop(0, tm // ROW_DMA_UNROLL, body, 0)

    def row(r):
        return pl.ds(pl.multiple_of(r * c, c), c)

    def start_gather(tok_ref, s):
        row_loop(lambda r: pltpu.make_async_copy(
            hn_hbm.at[row(tok_ref[0, 0, r]), :], xbuf.at[s, row(r), :], gsem.at[s]).start())

    def wait_gather(s):
        pltpu.make_async_copy(hn_hbm.at[pl.ds(0, tm * c), :], xbuf.at[s], gsem.at[s]).wait()

    def start_scatter():
        row_loop(lambda r: pltpu.make_async_copy(
            ybuf.at[row(r), :], y_hbm.at[row(dst_ref[0, 0, r]), :], ssem.at[0]).start())

    def spare_rows_copy():
        return pltpu.make_async_copy(ybuf, y_hbm.at[pl.ds(n_real * c, tm * c), :], ssem.at[0])

    def wait_scatter():
        spare_rows_copy().wait()

    @pl.when((j == 0) & (i == 0))
    def _():
        ybuf[...] = jnp.zeros_like(ybuf)
        spare_rows_copy().start()
        spare_rows_copy().wait()

        @pl.when(nu > 0)
        def _():
            start_gather(tokc_ref, 0)

    @pl.when((j == 0) & (i < nu))
    def _():
        wait_gather(slot)
        for k, chunk in enumerate(_load_row_tiles(xbuf.at[slot], tm, d)):
            xb16[:, k * V7X_LANES:(k + 1) * V7X_LANES] = chunk.astype(BF16)

    @pl.when((j == 0) & (i + 1 < nu))
    def _():
        start_gather(tokn_ref, 1 - slot)

    @pl.when(i < nu)
    def _():
        x = xb16[...]
        a = _dot(x, w1_ref[0])
        b = _dot(x, w3_ref[0])
        part = _dot((_silu(a) * b).astype(BF16), w2_ref[0])

        @pl.when(j == 0)
        def _():
            acc_ref[...] = part

        @pl.when(j > 0)
        def _():
            acc_ref[...] += part

    last = j == nf - 1

    @pl.when(last & (i >= 1) & (i - 1 < nu))
    def _():
        wait_scatter()

    @pl.when(last & (i < nu))
    def _():
        _store_row_tiles(ybuf, acc_ref[...])
        start_scatter()

    @pl.when(last & (i == nt - 1) & (i < nu))
    def _():
        wait_scatter()


def moe_experts(hn, slab, counts, w1, w3, w2, tm, tf):
    t = slab.shape[0]
    d = w1.shape[1]
    c = _row_tile_chunks(d)
    f = w1.shape[2]
    nf = f // tf
    nt = (2 * t) // tm + N_EXPERTS
    rows = nt * tm

    e0 = slab[:, COL_E0].astype(jnp.int32)
    e1 = slab[:, COL_E1].astype(jnp.int32)
    r0 = slab[:, COL_R0].astype(jnp.int32)
    r1 = slab[:, COL_R1].astype(jnp.int32)
    cnt = counts[0, :N_EXPERTS].astype(jnp.int32)
    tiles = (cnt + tm - 1) // tm
    tile_end = jnp.cumsum(tiles)
    tile_start = tile_end - tiles
    nu = tile_end[-1:]
    offs = tile_start * tm
    dest0 = offs[e0] + r0
    dest1 = offs[e1] + r1
    tile_expert = jnp.minimum(
        jnp.sum(jnp.arange(nt, dtype=jnp.int32)[:, None] >= tile_end[None, :], axis=1), N_EXPERTS - 1
    ).astype(jnp.int32)
    tok_ids = jnp.arange(t, dtype=jnp.int32)
    spare = 2 * t + (jnp.arange(rows, dtype=jnp.int32) % tm)
    dst_sorted = spare.at[jnp.concatenate([dest0, dest1])].set(jnp.concatenate([tok_ids, t + tok_ids]))
    tok_sorted = jnp.where(dst_sorted >= 2 * t, 0, jnp.where(dst_sorted >= t, dst_sorted - t, dst_sorted))
    tok3 = tok_sorted.reshape(nt, 1, tm)
    dst3 = dst_sorted.reshape(nt, 1, tm)

    def w_in(shape, which):
        def index_map(i, j, te, nu_):
            ii = jnp.minimum(i, nu_[0] - 1)
            jj = jnp.where(i < nu_[0], j, nf - 1)
            return (te[ii], 0, jj) if which == "up" else (te[ii], jj, 0)
        return pl.BlockSpec(shape, index_map)

    smem_tile = lambda f_: pl.BlockSpec((1, 1, tm), f_, memory_space=pltpu.SMEM)
    grid_spec = pltpu.PrefetchScalarGridSpec(
        num_scalar_prefetch=2,
        grid=(nt, nf),
        in_specs=[smem_tile(lambda i, j, te, nu_: (i, 0, 0)),
                  smem_tile(lambda i, j, te, nu_: (jnp.minimum(i + 1, nt - 1), 0, 0)),
                  smem_tile(lambda i, j, te, nu_: (i, 0, 0)),
                  pl.BlockSpec(memory_space=pl.ANY),
                  w_in((1, d, tf), "up"), w_in((1, d, tf), "up"), w_in((1, tf, d), "down")],
        out_specs=pl.BlockSpec(memory_space=pl.ANY),
        scratch_shapes=[pltpu.VMEM((2, tm * c, V7X_LANES), F32),
                        pltpu.VMEM((tm, d), BF16),
                        pltpu.VMEM((tm, d), F32),
                        pltpu.VMEM((tm * c, V7X_LANES), F32),
                        pltpu.SemaphoreType.DMA((2,)),
                        pltpu.SemaphoreType.DMA((1,))],
    )
    return pl.pallas_call(
        functools.partial(_moe_kernel, tm=tm, nf=nf, nt=nt, n_real=2 * t),
        grid_spec=grid_spec,
        out_shape=jax.ShapeDtypeStruct(((2 * t + tm) * c, V7X_LANES), F32),
        compiler_params=_cparams(("arbitrary", "arbitrary"), 56),
        name="moe_experts",
    )(tile_expert, nu, tok3, tok3, dst3, hn, w1, w3, w2)


def _combine_kernel(h_ref, y0_ref, y1_ref, slab_ref, g_ref, out_ref, *, final):
    slab = slab_ref[...]
    tm, d = h_ref.shape
    y0 = jnp.concatenate(_load_row_tiles(y0_ref, tm, d), axis=1)
    y1 = jnp.concatenate(_load_row_tiles(y1_ref, tm, d), axis=1)
    x = h_ref[...] + slab[:, COL_W0:COL_W0 + 1] * y0 + slab[:, COL_W1:COL_W1 + 1] * y1
    out_ref[...] = _rmsnorm_f32(x, g_ref[...]) if final else x


def moe_combine(h, y, slab, g, tm, final):
    t, d = h.shape
    nb = t // tm
    c = _row_tile_chunks(d)
    return pl.pallas_call(
        functools.partial(_combine_kernel, final=final),
        grid=(nb,),
        in_specs=[pl.BlockSpec((tm, d), lambda i: (i, 0)),
                  pl.BlockSpec((tm * c, V7X_LANES), lambda i: (i, 0)),
                  pl.BlockSpec((tm * c, V7X_LANES), lambda i: (i + nb, 0)),
                  pl.BlockSpec((tm, V7X_LANES), lambda i: (i, 0)),
                  pl.BlockSpec((1, d), lambda i: (0, 0))],
        out_specs=pl.BlockSpec((tm, d), lambda i: (i, 0)),
        out_shape=jax.ShapeDtypeStruct((t, d), F32),
        compiler_params=_cparams(("arbitrary",), 40),
        name="moe_combine",
    )(h, y, y, slab, g.reshape(1, d))


def _norm_kernel(h_ref, g_ref, out_ref):
    out_ref[...] = _rmsnorm_f32(h_ref[...], g_ref[...])


def final_norm_only(h, g, tm):
    t, d = h.shape
    return pl.pallas_call(
        _norm_kernel,
        grid=(t // tm,),
        in_specs=[pl.BlockSpec((tm, d), lambda i: (i, 0)), pl.BlockSpec((1, d), lambda i: (0, 0))],
        out_specs=pl.BlockSpec((tm, d), lambda i: (i, 0)),
        out_shape=jax.ShapeDtypeStruct((t, d), F32),
        compiler_params=_cparams(("arbitrary",), 40),
        name="final_norm",
    )(h, g.reshape(1, d))


def _row_tile(t, want):
    tm = min(want, t)
    assert t % tm == 0
    return tm


def kernel(x, w_qkv, w_o, mixer_norm, ffn_norm, rel_bias, w1, w3, w2, router, e_w1, e_w3, e_w2, final_norm):
    b, s, d = x.shape
    assert d == N_HEADS * HEAD_DIM and s % MOBA_BLOCK == 0
    t = b * s
    depth = w_qkv.shape[0]
    h = x.reshape(t, d)
    tm_big = _row_tile(t, 1024)
    tm_mid = _row_tile(t, 512)
    normed = False
    for i in range(depth):
        qkv = qkv_proj(h, mixer_norm[i], w_qkv[i].astype(BF16), tm_big)
        if i % 2 == 0:
            o = stick_breaking_attention(qkv, b, s, MOBA_BLOCK)
        else:
            o = moba_attention(qkv, rel_bias, b, s)
        h = out_proj(o.reshape(t, d), w_o[i].astype(BF16), h, tm_mid)
        jj = i // 2
        if i % 2 == 0:
            h = dense_ffn(h, ffn_norm[i], w1[jj].astype(BF16), w3[jj].astype(BF16), w2[jj].astype(BF16),
                          tm_mid, 1024)
        else:
            hn, slab, counts = moe_router(h, ffn_norm[i], router[jj], tm_mid)
            f_e = e_w1.shape[-1]
            tf = f_e // 2 if (f_e // 2) % 256 == 0 else f_e
            y = moe_experts(hn, slab, counts, e_w1[jj].astype(BF16), e_w3[jj].astype(BF16),
                            e_w2[jj].astype(BF16), tm_mid, tf)
            last = i == depth - 1
            h = moe_combine(h, y, slab, final_norm if last else ffn_norm[i], tm_mid, last)
            normed = last
    if not normed:
        h = final_norm_only(h, final_norm, tm_mid)
    return h.reshape(b, s, d)
```

```python
import functools
import math

import jax
import jax.numpy as jnp
from jax import lax
from jax.experimental import pallas as pl
from jax.experimental.pallas import tpu as pltpu

N_HEADS = 16
HEAD_DIM = 64
MOBA_BLOCK = 256
MOBA_TOPK = 3
N_BUCKETS = 32
MAX_DISTANCE = 128
N_EXPERTS = 8
RMS_EPS = 1e-6
NEG_BIG = -1e30

V7X_LANES = 128
ROW_DMA_UNROLL = 8
SB_GROUP = 4
MOBA_GROUP = 4
V7X_VMEM_BYTES = 64 * 1024 * 1024
HEADS_PER_LANE_BLOCK = V7X_LANES // HEAD_DIM

F32 = jnp.float32
BF16 = jnp.bfloat16


def _cparams(semantics, vmem_mb):
    assert vmem_mb * 1024 * 1024 < V7X_VMEM_BYTES
    return pltpu.CompilerParams(dimension_semantics=semantics,
                                vmem_limit_bytes=vmem_mb * 1024 * 1024)


def _rmsnorm_f32(x, g):
    return x * lax.rsqrt(jnp.mean(x * x, axis=-1, keepdims=True) + RMS_EPS) * g


def _silu(a):
    return a * (1.0 / (1.0 + jnp.exp(-a)))


def _dot(a, b):
    return jnp.dot(a, b, preferred_element_type=F32)


def _dot_nt(a, b):
    return lax.dot_general(a, b, (((1,), (1,)), ((), ())), preferred_element_type=F32)


def _qkv_kernel(x_ref, g_ref, w_ref, o_ref, hn_ref):
    j = pl.program_id(1)

    @pl.when(j == 0)
    def _():
        hn_ref[...] = _rmsnorm_f32(x_ref[...], g_ref[...]).astype(BF16)

    scale = jnp.where(j == 0, HEAD_DIM ** -0.5, 1.0)
    o_ref[0] = (_dot(hn_ref[...], w_ref[...]) * scale).astype(o_ref.dtype)


def qkv_proj(h, g, w_bf16, tm):
    t, d = h.shape
    return pl.pallas_call(
        _qkv_kernel,
        grid=(t // tm, 3),
        in_specs=[pl.BlockSpec((tm, d), lambda i, j: (i, 0)),
                  pl.BlockSpec((1, d), lambda i, j: (0, 0)),
                  pl.BlockSpec((d, d), lambda i, j: (0, j))],
        out_specs=pl.BlockSpec((1, tm, d), lambda i, j: (j, i, 0)),
        out_shape=jax.ShapeDtypeStruct((3, t, d), BF16),
        scratch_shapes=[pltpu.VMEM((tm, d), BF16)],
        compiler_params=_cparams(("arbitrary", "arbitrary"), 40),
        name="qkv_proj",
    )(h, g.reshape(1, d), w_bf16)


def _sb_kernel(q_ref, k_ref, v_ref, o_ref, acc0_ref, acc1_ref, r0_ref, r1_ref, u_ref, *, tq):
    acc_ref = (acc0_ref, acc1_ref)
    r_ref = (r0_ref, r1_ref)
    qi = pl.program_id(2)
    row = lax.broadcasted_iota(jnp.int32, (tq, tq), 0)
    col = lax.broadcasted_iota(jnp.int32, (tq, tq), 1)

    @pl.when(qi == 0)
    def _():
        u_ref[...] = -(row > col).astype(BF16)

    q = q_ref[0, 0]
    lane = lax.broadcasted_iota(jnp.int32, (tq, V7X_LANES), 1)
    past = col < row
    heads = range(HEADS_PER_LANE_BLOCK)
    qms = [jnp.where(lane // HEAD_DIM == h, q, jnp.zeros_like(q)) for h in heads]

    def span(first, nb, diagonal):
        tasks = [(i, h) for i in reversed(range(nb)) for h in heads]
        state = [dict() for _ in tasks]
        r = [r_ref[h][...] for h in heads]
        acc = [None for h in heads]

        def stage(k, st, i, h):
            on_diagonal = diagonal and i == nb - 1
            start = pl.multiple_of((first + i) * tq, tq)
            if k == 0:
                st["z"] = _dot_nt(qms[h], k_ref[0, 0, pl.ds(start, tq), :])
            elif k == 1:
                z = st.pop("z")
                zb = z.astype(BF16)
                t = jnp.log(1 + jnp.exp(-jnp.abs(zb)))
                sp = jnp.maximum(zb, 0) + t
                if on_diagonal:
                    sp = jnp.where(past, sp, jnp.zeros_like(sp))
                st["sp"] = sp
                st["log_sig"] = jnp.minimum(z, 0.0) - t.astype(F32)
            elif k == 2:
                st["after"] = _dot(st["sp"], u_ref[...])
            elif k == 3:
                tot = st.pop("after") + r[h]
                w = jnp.exp(st.pop("log_sig") + tot)
                if on_diagonal:
                    w = jnp.where(past, w, 0.0)
                r[h] = tot[:, 0:1] - st.pop("sp")[:, 0:1].astype(F32)
                st["w"] = w.astype(BF16)
            else:
                part = _dot(st.pop("w"), v_ref[0, 0, pl.ds(start, tq), :])
                acc[h] = part if acc[h] is None else acc[h] + part

        n_stages = 5
        for step in range(len(tasks) + n_stages - 1):
            for k in reversed(range(n_stages)):
                ti = step - k
                if 0 <= ti < len(tasks):
                    stage(k, state[ti], *tasks[ti])
        for h in heads:
            r_ref[h][...] = r[h]
            acc_ref[h][...] += acc[h]

    for h in heads:
        r_ref[h][...] = jnp.zeros((tq, 1), F32)
        acc_ref[h][...] = jnp.zeros((tq, V7X_LANES), F32)

    n_groups = qi // SB_GROUP
    for in_group in range(SB_GROUP):
        @pl.when(qi % SB_GROUP == in_group)
        def _(in_group=in_group):
            span(n_groups * SB_GROUP, in_group + 1, True)

    def body(s, carry):
        span((n_groups - 1 - s) * SB_GROUP, SB_GROUP, False)
        return carry

    lax.fori_loop(0, n_groups, body, 0)

    o_ref[0] = jnp.where(lane // HEAD_DIM == 0, acc0_ref[...], acc1_ref[...]).astype(o_ref.dtype)


def stick_breaking_attention(qkv, b, s, tq):
    d = qkv.shape[-1]
    qkv4 = qkv.reshape(3, b, s, d)
    return pl.pallas_call(
        functools.partial(_sb_kernel, tq=tq),
        grid=(b, d // V7X_LANES, s // tq),
        in_specs=[pl.BlockSpec((1, 1, tq, V7X_LANES), lambda bi, hp, qi: (0, bi, qi, hp)),
                  pl.BlockSpec((1, 1, s, V7X_LANES), lambda bi, hp, qi: (1, bi, 0, hp)),
                  pl.BlockSpec((1, 1, s, V7X_LANES), lambda bi, hp, qi: (2, bi, 0, hp))],
        out_specs=pl.BlockSpec((1, tq, V7X_LANES), lambda bi, hp, qi: (bi, qi, hp)),
        out_shape=jax.ShapeDtypeStruct((b, s, d), BF16),
        scratch_shapes=[pltpu.VMEM((tq, V7X_LANES), F32), pltpu.VMEM((tq, V7X_LANES), F32),
                        pltpu.VMEM((tq, 1), F32), pltpu.VMEM((tq, 1), F32),
                        pltpu.VMEM((tq, tq), BF16)],
        compiler_params=_cparams(("arbitrary", "arbitrary", "arbitrary"), 32),
        name="stick_breaking",
    )(qkv4, qkv4, qkv4)


def _t5_bias_kernel(rb_ref, o_ref, *, tq):
    h = pl.program_id(0)
    row = lax.broadcasted_iota(jnp.int32, (tq, tq), 0)
    col = lax.broadcasted_iota(jnp.int32, (tq, tq), 1)
    max_exact = N_BUCKETS // 2
    for o in range(2):
        dist = o * tq + row - col
        n = jnp.maximum(dist, 0)
        nf = jnp.maximum(n, 1).astype(F32)
        large = max_exact + (jnp.log(nf / max_exact) / math.log(MAX_DISTANCE / max_exact)
                             * (N_BUCKETS - max_exact)).astype(jnp.int32)
        large = jnp.minimum(large, N_BUCKETS - 1)
        bucket = jnp.where(n < max_exact, n, large)
        bias = jnp.zeros((tq, tq), F32)
        for bkt in range(N_BUCKETS):
            bias = jnp.where(bucket == bkt, rb_ref[bkt, h], bias)
        if o == 0:
            bias = jnp.where(dist >= 0, bias, NEG_BIG)
        o_ref[0, o] = bias


def t5_bias_tiles(rel_bias, tq):
    return pl.pallas_call(
        functools.partial(_t5_bias_kernel, tq=tq),
        grid=(N_HEADS,),
        in_specs=[pl.BlockSpec(memory_space=pltpu.SMEM)],
        out_specs=pl.BlockSpec((1, 2, tq, tq), lambda h: (h, 0, 0, 0)),
        out_shape=jax.ShapeDtypeStruct((N_HEADS, 2, tq, tq), F32),
        compiler_params=_cparams(("arbitrary",), 16),
        name="t5_bias_tiles",
    )(rel_bias)


def _moba_kernel(rb_ref, q_ref, qall_ref, k_ref, v_ref, bt_ref, o_ref,
                 s_ref, pick_ref, mx_ref, l_ref, acc_ref, *, tq, nblk):
    hp = pl.program_id(1)
    qi = pl.program_id(2)

    heads = range(HEADS_PER_LANE_BLOCK)
    s_len = nblk * tq
    n_sel = min(MOBA_TOPK, nblk - 1)

    @pl.when(qi == 0)
    def _():
        km = jnp.concatenate(
            [jnp.mean(k_ref[0, 0, n * tq:(n + 1) * tq, :].astype(F32), axis=0, keepdims=True)
             for n in range(nblk)], axis=0)
        km_hi = km.astype(BF16)
        km_lo = (km - km_hi.astype(F32)).astype(BF16)
        q_all = qall_ref[0, 0]
        lane_all = lax.broadcasted_iota(jnp.int32, (s_len, V7X_LANES), 1)
        blk = lax.broadcasted_iota(jnp.int32, (nblk, s_len), 0)
        own = lax.broadcasted_iota(jnp.int32, (nblk, s_len), 1) // tq
        own_row = own[0:1, :]
        no_rows = jnp.zeros((V7X_LANES - nblk, s_len), F32)
        for h in heads:
            qm = jnp.where(lane_all // HEAD_DIM == h, q_all, jnp.zeros_like(q_all))
            gate = _dot_nt(km_hi, qm) + _dot_nt(km_lo, qm)
            gate = jnp.where(blk < own, gate, NEG_BIG)
            rows = [gate[n:n + 1, :] for n in range(nblk)]
            rank = [jnp.zeros((1, s_len), F32) for _ in range(nblk)]
            for n in range(nblk):
                for m in range(n):
                    m_wins = jnp.where(rows[m] >= rows[n], 1.0, 0.0)
                    rank[n] = rank[n] + m_wins
                    rank[m] = rank[m] + (1.0 - m_wins)
            chosen = [jnp.where((rank[n] < n_sel) & (n < own_row), 1.0, 0.0) for n in range(nblk)]
            pick_ref[h] = jnp.concatenate(chosen + [no_rows], axis=0).T.astype(BF16)

    q = q_ref[0, 0]
    lane = lax.broadcasted_iota(jnp.int32, (tq, V7X_LANES), 1)
    qms = [jnp.where(lane // HEAD_DIM == h, q, jnp.zeros_like(q)) for h in heads]
    both = lambda x: jnp.concatenate([x, x], axis=1)
    halves = lambda x: (x[:, :V7X_LANES], x[:, V7X_LANES:])

    big = jnp.asarray(-NEG_BIG, BF16).astype(F32)
    spread_row = lax.broadcasted_iota(jnp.int32, (V7X_LANES, V7X_LANES), 0)
    tile_rows = pl.ds(pl.multiple_of(qi * tq, tq), tq)
    for h in heads:
        mx_ref[h] = jnp.full((tq, V7X_LANES), NEG_BIG, F32)

    far_bias = [rb_ref[N_BUCKETS - 1, hp * HEADS_PER_LANE_BLOCK + h] for h in heads]

    def pipeline(tasks, stages):
        state = [dict() for _ in tasks]
        for step in range(len(tasks) + len(stages) - 1):
            for k in reversed(range(len(stages))):
                t = step - k
                if 0 <= t < len(tasks):
                    stages[k](state[t], *tasks[t])

    def logits_span(first, roles):
        mx = [mx_ref[h] for h in heads]

        def products(st, i, h):
            start = pl.multiple_of((first + i) * tq, tq)
            st["qk"] = _dot_nt(qms[h], k_ref[0, 0, pl.ds(start, tq), :])
            if roles[i] != "own":
                spread_n = jnp.where(spread_row == first + i, big, 0.0).astype(BF16)
                st["lift"] = _dot(pick_ref[h, tile_rows, :], spread_n)

        def finish(st, i, h):
            start = pl.multiple_of((first + i) * tq, tq)
            if roles[i] == "own":
                bias = bt_ref[h, 0]
            elif roles[i] == "prev":
                bias = bt_ref[h, 1] + both(st.pop("lift") - big)
            else:
                bias = both(st.pop("lift") - big + far_bias[h])
            s = st.pop("qk") + bias
            s_ref[h, :, pl.ds(start, tq)] = s
            s_lo, s_hi = halves(s)
            mx[h] = jnp.maximum(mx[h], jnp.maximum(s_lo, s_hi))

        pipeline([(i, h) for i in range(len(roles)) for h in heads], [products, finish])
        for h in heads:
            mx_ref[h] = mx[h]

    def probs_span(first, nb):
        m = [both(mx_ref[h]) for h in heads]
        l = [l_ref[h] for h in heads]
        acc = [acc_ref[h] for h in heads]

        def exponentials(st, i, h):
            start = pl.multiple_of((first + i) * tq, tq)
            p = jnp.exp(s_ref[h, :, pl.ds(start, tq)] - m[h])
            p_lo, p_hi = halves(p)
            l[h] = l[h] + (p_lo + p_hi)
            st["p"] = p.astype(BF16)

        def accumulate(st, i, h):
            start = pl.multiple_of((first + i) * tq, tq)
            acc[h] = acc[h] + _dot(st.pop("p"), v_ref[0, 0, pl.ds(start, tq), :])

        pipeline([(i, h) for i in range(nb) for h in heads], [exponentials, accumulate])
        for h in heads:
            l_ref[h] = l[h]
            acc_ref[h] = acc[h]

    group = MOBA_GROUP
    own_group = qi // group
    in_group = qi % group

    def far_groups(g, carry):
        logits_span(g * group, ["far"] * group)
        return carry

    lax.fori_loop(0, own_group - 1, far_groups, 0)

    @pl.when((own_group >= 1) & (in_group == 0))
    def _():
        logits_span((own_group - 1) * group, ["far"] * (group - 1) + ["prev"])

    @pl.when((own_group >= 1) & (in_group != 0))
    def _():
        logits_span((own_group - 1) * group, ["far"] * group)

    for size in range(1, group + 1):
        @pl.when(in_group == size - 1)
        def _(size=size):
            logits_span(own_group * group, (["far"] * group + ["prev", "own"])[-size:]
                        if size >= 2 else ["own"])

    for h in heads:
        mx_ref[h] = jnp.broadcast_to(jnp.max(mx_ref[h], axis=1, keepdims=True), (tq, V7X_LANES))
        l_ref[h] = jnp.zeros((tq, V7X_LANES), F32)
        acc_ref[h] = jnp.zeros((tq, V7X_LANES), F32)

    def prob_groups(g, carry):
        probs_span(g * group, group)
        return carry

    lax.fori_loop(0, own_group, prob_groups, 0)
    for size in range(1, group + 1):
        @pl.when(in_group == size - 1)
        def _(size=size):
            probs_span(own_group * group, size)


    out = [acc_ref[h] / jnp.sum(l_ref[h], axis=1, keepdims=True) for h in heads]
    o_ref[0] = jnp.where(lane // HEAD_DIM == 0, out[0], out[1]).astype(o_ref.dtype)


def moba_attention(qkv, rel_bias, b, s):
    d = qkv.shape[-1]
    tq = MOBA_BLOCK
    assert s % tq == 0 and tq >= 2 * MAX_DISTANCE
    nblk = s // tq
    qkv4 = qkv.reshape(3, b, s, d)
    tiles = t5_bias_tiles(rel_bias, tq)
    return pl.pallas_call(
        functools.partial(_moba_kernel, tq=tq, nblk=nblk),
        grid=(b, d // V7X_LANES, nblk),
        in_specs=[pl.BlockSpec(memory_space=pltpu.SMEM),
                  pl.BlockSpec((1, 1, tq, V7X_LANES), lambda bi, hp, qi: (0, bi, qi, hp)),
                  pl.BlockSpec((1, 1, s, V7X_LANES), lambda bi, hp, qi: (0, bi, 0, hp)),
                  pl.BlockSpec((1, 1, s, V7X_LANES), lambda bi, hp, qi: (1, bi, 0, hp)),
                  pl.BlockSpec((1, 1, s, V7X_LANES), lambda bi, hp, qi: (2, bi, 0, hp)),
                  pl.BlockSpec((HEADS_PER_LANE_BLOCK, 2, tq, tq), lambda bi, hp, qi: (hp, 0, 0, 0))],
        out_specs=pl.BlockSpec((1, tq, V7X_LANES), lambda bi, hp, qi: (bi, qi, hp)),
        out_shape=jax.ShapeDtypeStruct((b, s, d), BF16),
        scratch_shapes=[pltpu.VMEM((HEADS_PER_LANE_BLOCK, tq, s), F32),
                        pltpu.VMEM((HEADS_PER_LANE_BLOCK, s, V7X_LANES), BF16),
                        pltpu.VMEM((HEADS_PER_LANE_BLOCK, tq, V7X_LANES), F32),
                        pltpu.VMEM((HEADS_PER_LANE_BLOCK, tq, V7X_LANES), F32),
                        pltpu.VMEM((HEADS_PER_LANE_BLOCK, tq, V7X_LANES), F32)],
        compiler_params=_cparams(("arbitrary", "arbitrary", "arbitrary"), 32),
        name="moba",
    )(rel_bias, qkv4, qkv4, qkv4, qkv4, tiles)


def _oproj_kernel(o_ref, w_ref, h_ref, out_ref):
    out_ref[...] = h_ref[...] + _dot(o_ref[...], w_ref[...])


def out_proj(o, w_bf16, h, tm):
    t, d = h.shape
    return pl.pallas_call(
        _oproj_kernel,
        grid=(t // tm,),
        in_specs=[pl.BlockSpec((tm, d), lambda i: (i, 0)),
                  pl.BlockSpec((d, d), lambda i: (0, 0)),
                  pl.BlockSpec((tm, d), lambda i: (i, 0))],
        out_specs=pl.BlockSpec((tm, d), lambda i: (i, 0)),
        out_shape=jax.ShapeDtypeStruct((t, d), F32),
        compiler_params=_cparams(("arbitrary",), 40),
        name="out_proj",
    )(o, w_bf16, h)


def _ffn_kernel(h_ref, g_ref, w1_ref, w3_ref, w2_ref, out_ref, *, chunks):
    x = h_ref[...]
    hn = _rmsnorm_f32(x, g_ref[...]).astype(BF16)
    acc = x
    for c0, c1 in chunks:
        a = _dot(hn, w1_ref[:, c0:c1])
        b = _dot(hn, w3_ref[:, c0:c1])
        acc = acc + _dot((_silu(a) * b).astype(BF16), w2_ref[c0:c1, :])
    out_ref[...] = acc


def dense_ffn(h, g, w1, w3, w2, tm, chunk):
    t, d = h.shape
    f = w1.shape[1]
    chunks = tuple((c, min(c + chunk, f)) for c in range(0, f, chunk))
    whole = lambda shape: pl.BlockSpec(shape, lambda i: (0, 0), pipeline_mode=pl.Buffered(1))
    return pl.pallas_call(
        functools.partial(_ffn_kernel, chunks=chunks),
        grid=(t // tm,),
        in_specs=[pl.BlockSpec((tm, d), lambda i: (i, 0)),
                  pl.BlockSpec((1, d), lambda i: (0, 0)),
                  whole((d, f)), whole((d, f)), whole((f, d))],
        out_specs=pl.BlockSpec((tm, d), lambda i: (i, 0)),
        out_shape=jax.ShapeDtypeStruct((t, d), F32),
        compiler_params=_cparams(("arbitrary",), 56),
        name="dense_ffn",
    )(h, g.reshape(1, d), w1, w3, w2)


def _row_tile_chunks(d):
    assert d % V7X_LANES == 0
    return d // V7X_LANES


def _store_row_tiles(ref, x):
    n, d = x.shape
    c = _row_tile_chunks(d)
    for k in range(c):
        ref[pl.ds(k, n, stride=c), :] = x[:, k * V7X_LANES:(k + 1) * V7X_LANES]


def _load_row_tiles(ref, n, d):
    c = _row_tile_chunks(d)
    return [ref[pl.ds(k, n, stride=c), :] for k in range(c)]


COL_E0, COL_E1, COL_W0, COL_W1, COL_R0, COL_R1 = range(6)


def _router_kernel(h_ref, g_ref, wr_ref, hn_ref, slab_ref, cnt_ref, carry_ref, *, tm):
    i = pl.program_id(0)

    @pl.when(i == 0)
    def _():
        carry_ref[...] = jnp.zeros_like(carry_ref)

    hn = _rmsnorm_f32(h_ref[...], g_ref[...])
    _store_row_tiles(hn_ref, hn)
    hi = hn.astype(BF16)
    lo = (hn - hi.astype(F32)).astype(BF16)
    w = wr_ref[...]
    w_hi = w.astype(BF16)
    w_lo = (w - w_hi.astype(F32)).astype(BF16)
    logits = _dot(hi, w_hi) + _dot(hi, w_lo) + _dot(lo, w_hi)

    lane = lax.broadcasted_iota(jnp.int32, (tm, V7X_LANES), 1)
    neg_inf = jnp.float32(-jnp.inf)
    lg = jnp.where(lane < N_EXPERTS, logits, neg_inf)
    m0 = jnp.max(lg, axis=1, keepdims=True)
    i0 = jnp.min(jnp.where(lg == m0, lane, V7X_LANES), axis=1, keepdims=True)
    lg1 = jnp.where(lane == i0, neg_inf, lg)
    m1 = jnp.max(lg1, axis=1, keepdims=True)
    i1 = jnp.min(jnp.where(lg1 == m1, lane, V7X_LANES), axis=1, keepdims=True)
    e = jnp.exp(m1 - m0)
    w0 = 1.0 / (1.0 + e)
    w1 = e / (1.0 + e)

    pick0 = lane == i0
    pick1 = lane == i1
    hot = (pick0 | pick1).astype(BF16)
    row = lax.broadcasted_iota(jnp.int32, (tm, tm), 0)
    col = lax.broadcasted_iota(jnp.int32, (tm, tm), 1)
    before = _dot((col < row).astype(BF16), hot) + carry_ref[0:1, :]
    r0 = jnp.sum(jnp.where(pick0, before, 0.0), axis=1, keepdims=True)
    r1 = jnp.sum(jnp.where(pick1, before, 0.0), axis=1, keepdims=True)
    carry_ref[0:1, :] = carry_ref[0:1, :] + jnp.sum(hot.astype(F32), axis=0, keepdims=True)

    slab = jnp.zeros((tm, V7X_LANES), F32)
    for c, val in ((COL_E0, i0.astype(F32)), (COL_E1, i1.astype(F32)), (COL_W0, w0), (COL_W1, w1),
                   (COL_R0, r0), (COL_R1, r1)):
        slab = jnp.where(lane == c, val, slab)
    slab_ref[...] = slab
    cnt_ref[...] = carry_ref[...]


def moe_router(h, g, w_router, tm):
    t, d = h.shape
    wr = jnp.zeros((d, V7X_LANES), F32).at[:, :N_EXPERTS].set(w_router)
    return pl.pallas_call(
        functools.partial(_router_kernel, tm=tm),
        grid=(t // tm,),
        in_specs=[pl.BlockSpec((tm, d), lambda i: (i, 0)),
                  pl.BlockSpec((1, d), lambda i: (0, 0)),
                  pl.BlockSpec((d, V7X_LANES), lambda i: (0, 0))],
        out_specs=[pl.BlockSpec((tm * d // V7X_LANES, V7X_LANES), lambda i: (i, 0)),
                   pl.BlockSpec((tm, V7X_LANES), lambda i: (i, 0)),
                   pl.BlockSpec((8, V7X_LANES), lambda i: (0, 0))],
        out_shape=[jax.ShapeDtypeStruct((t * d // V7X_LANES, V7X_LANES), F32),
                   jax.ShapeDtypeStruct((t, V7X_LANES), F32),
                   jax.ShapeDtypeStruct((8, V7X_LANES), F32)],
        scratch_shapes=[pltpu.VMEM((8, V7X_LANES), F32)],
        compiler_params=_cparams(("arbitrary",), 40),
        name="moe_router",
    )(h, g.reshape(1, d), wr)


def _moe_kernel(te_ref, nu_ref, tokc_ref, tokn_ref, dst_ref, hn_hbm, w1_ref, w3_ref, w2_ref, y_hbm,
                xbuf, xb16, acc_ref, ybuf, gsem, ssem, *, tm, nf, nt, n_real):
    i = pl.program_id(0)
    j = pl.program_id(1)
    nu = nu_ref[0]
    slot = i % 2
    d = acc_ref.shape[1]
    c = _row_tile_chunks(d)

    def row_loop(start_row_copy):
        def body(g, carry):
            for k in range(ROW_DMA_UNROLL):
                start_row_copy(g * ROW_DMA_UNROLL + k)
            return carry
        lax.fori_loop(0, tm // ROW_DMA_UNROLL, body, 0)

    def row(r):
        return pl.ds(pl.multiple_of(r * c, c), c)

    def start_gather(tok_ref, s):
        row_loop(lambda r: pltpu.make_async_copy(
            hn_hbm.at[row(tok_ref[0, 0, r]), :], xbuf.at[s, row(r), :], gsem.at[s]).start())

    def wait_gather(s):
        pltpu.make_async_copy(hn_hbm.at[pl.ds(0, tm * c), :], xbuf.at[s], gsem.at[s]).wait()

    def start_scatter():
        row_loop(lambda r: pltpu.make_async_copy(
            ybuf.at[row(r), :], y_hbm.at[row(dst_ref[0, 0, r]), :], ssem.at[0]).start())

    def spare_rows_copy():
        return pltpu.make_async_copy(ybuf, y_hbm.at[pl.ds(n_real * c, tm * c), :], ssem.at[0])

    def wait_scatter():
        spare_rows_copy().wait()

    @pl.when((j == 0) & (i == 0))
    def _():
        ybuf[...] = jnp.zeros_like(ybuf)
        spare_rows_copy().start()
        spare_rows_copy().wait()

        @pl.when(nu > 0)
        def _():
            start_gather(tokc_ref, 0)

    @pl.when((j == 0) & (i < nu))
    def _():
        wait_gather(slot)
        for k, chunk in enumerate(_load_row_tiles(xbuf.at[slot], tm, d)):
            xb16[:, k * V7X_LANES:(k + 1) * V7X_LANES] = chunk.astype(BF16)

    @pl.when((j == 0) & (i + 1 < nu))
    def _():
        start_gather(tokn_ref, 1 - slot)

    @pl.when(i < nu)
    def _():
        x = xb16[...]
        a = _dot(x, w1_ref[0])
        b = _dot(x, w3_ref[0])
        part = _dot((_silu(a) * b).astype(BF16), w2_ref[0])

        @pl.when(j == 0)
        def _():
            acc_ref[...] = part

        @pl.when(j > 0)
        def _():
            acc_ref[...] += part

    last = j == nf - 1

    @pl.when(last & (i >= 1) & (i - 1 < nu))
    def _():
        wait_scatter()

    @pl.when(last & (i < nu))
    def _():
        _store_row_tiles(ybuf, acc_ref[...])
        start_scatter()

    @pl.when(last & (i == nt - 1) & (i < nu))
    def _():
        wait_scatter()


def moe_experts(hn, slab, counts, w1, w3, w2, tm, tf):
    t = slab.shape[0]
    d = w1.shape[1]
    c = _row_tile_chunks(d)
    f = w1.shape[2]
    nf = f // tf
    nt = (2 * t) // tm + N_EXPERTS
    rows = nt * tm

    e0 = slab[:, COL_E0].astype(jnp.int32)
    e1 = slab[:, COL_E1].astype(jnp.int32)
    r0 = slab[:, COL_R0].astype(jnp.int32)
    r1 = slab[:, COL_R1].astype(jnp.int32)
    cnt = counts[0, :N_EXPERTS].astype(jnp.int32)
    tiles = (cnt + tm - 1) // tm
    tile_end = jnp.cumsum(tiles)
    tile_start = tile_end - tiles
    nu = tile_end[-1:]
    offs = tile_start * tm
    dest0 = offs[e0] + r0
    dest1 = offs[e1] + r1
    tile_expert = jnp.minimum(
        jnp.sum(jnp.arange(nt, dtype=jnp.int32)[:, None] >= tile_end[None, :], axis=1), N_EXPERTS - 1
    ).astype(jnp.int32)
    tok_ids = jnp.arange(t, dtype=jnp.int32)
    spare = 2 * t + (jnp.arange(rows, dtype=jnp.int32) % tm)
    dst_sorted = spare.at[jnp.concatenate([dest0, dest1])].set(jnp.concatenate([tok_ids, t + tok_ids]))
    tok_sorted = jnp.where(dst_sorted >= 2 * t, 0, jnp.where(dst_sorted >= t, dst_sorted - t, dst_sorted))
    tok3 = tok_sorted.reshape(nt, 1, tm)
    dst3 = dst_sorted.reshape(nt, 1, tm)

    def w_in(shape, which):
        def index_map(i, j, te, nu_):
            ii = jnp.minimum(i, nu_[0] - 1)
            jj = jnp.where(i < nu_[0], j, nf - 1)
            return (te[ii], 0, jj) if which == "up" else (te[ii], jj, 0)
        return pl.BlockSpec(shape, index_map)

    smem_tile = lambda f_: pl.BlockSpec((1, 1, tm), f_, memory_space=pltpu.SMEM)
    grid_spec = pltpu.PrefetchScalarGridSpec(
        num_scalar_prefetch=2,
        grid=(nt, nf),
        in_specs=[smem_tile(lambda i, j, te, nu_: (i, 0, 0)),
                  smem_tile(lambda i, j, te, nu_: (jnp.minimum(i + 1, nt - 1), 0, 0)),
                  smem_tile(lambda i, j, te, nu_: (i, 0, 0)),
                  pl.BlockSpec(memory_space=pl.ANY),
                  w_in((1, d, tf), "up"), w_in((1, d, tf), "up"), w_in((1, tf, d), "down")],
        out_specs=pl.BlockSpec(memory_space=pl.ANY),
        scratch_shapes=[pltpu.VMEM((2, tm * c, V7X_LANES), F32),
                        pltpu.VMEM((tm, d), BF16),
                        pltpu.VMEM((tm, d), F32),
                        pltpu.VMEM((tm * c, V7X_LANES), F32),
                        pltpu.SemaphoreType.DMA((2,)),
                        pltpu.SemaphoreType.DMA((1,))],
    )
    return pl.pallas_call(
        functools.partial(_moe_kernel, tm=tm, nf=nf, nt=nt, n_real=2 * t),
        grid_spec=grid_spec,
        out_shape=jax.ShapeDtypeStruct(((2 * t + tm) * c, V7X_LANES), F32),
        compiler_params=_cparams(("arbitrary", "arbitrary"), 56),
        name="moe_experts",
    )(tile_expert, nu, tok3, tok3, dst3, hn, w1, w3, w2)


def _combine_kernel(h_ref, y0_ref, y1_ref, slab_ref, g_ref, out_ref, *, final):
    slab = slab_ref[...]
    tm, d = h_ref.shape
    y0 = jnp.concatenate(_load_row_tiles(y0_ref, tm, d), axis=1)
    y1 = jnp.concatenate(_load_row_tiles(y1_ref, tm, d), axis=1)
    x = h_ref[...] + slab[:, COL_W0:COL_W0 + 1] * y0 + slab[:, COL_W1:COL_W1 + 1] * y1
    out_ref[...] = _rmsnorm_f32(x, g_ref[...]) if final else x


def moe_combine(h, y, slab, g, tm, final):
    t, d = h.shape
    nb = t // tm
    c = _row_tile_chunks(d)
    return pl.pallas_call(
        functools.partial(_combine_kernel, final=final),
        grid=(nb,),
        in_specs=[pl.BlockSpec((tm, d), lambda i: (i, 0)),
                  pl.BlockSpec((tm * c, V7X_LANES), lambda i: (i, 0)),
                  pl.BlockSpec((tm * c, V7X_LANES), lambda i: (i + nb, 0)),
                  pl.BlockSpec((tm, V7X_LANES), lambda i: (i, 0)),
                  pl.BlockSpec((1, d), lambda i: (0, 0))],
        out_specs=pl.BlockSpec((tm, d), lambda i: (i, 0)),
        out_shape=jax.ShapeDtypeStruct((t, d), F32),
        compiler_params=_cparams(("arbitrary",), 40),
        name="moe_combine",
    )(h, y, y, slab, g.reshape(1, d))


def _norm_kernel(h_ref, g_ref, out_ref):
    out_ref[...] = _rmsnorm_f32(h_ref[...], g_ref[...])


def final_norm_only(h, g, tm):
    t, d = h.shape
    return pl.pallas_call(
        _norm_kernel,
        grid=(t // tm,),
        in_specs=[pl.BlockSpec((tm, d), lambda i: (i, 0)), pl.BlockSpec((1, d), lambda i: (0, 0))],
        out_specs=pl.BlockSpec((tm, d), lambda i: (i, 0)),
        out_shape=jax.ShapeDtypeStruct((t, d), F32),
        compiler_params=_cparams(("arbitrary",), 40),
        name="final_norm",
    )(h, g.reshape(1, d))


def _row_tile(t, want):
    tm = min(want, t)
    assert t % tm == 0
    return tm


def kernel(x, w_qkv, w_o, mixer_norm, ffn_norm, rel_bias, w1, w3, w2, router, e_w1, e_w3, e_w2, final_norm):
    b, s, d = x.shape
    assert d == N_HEADS * HEAD_DIM and s % MOBA_BLOCK == 0
    t = b * s
    depth = w_qkv.shape[0]
    h = x.reshape(t, d)
    tm_big = _row_tile(t, 1024)
    tm_mid = _row_tile(t, 512)
    normed = False
    for i in range(depth):
        qkv = qkv_proj(h, mixer_norm[i], w_qkv[i].astype(BF16), tm_big)
        if i % 2 == 0:
            o = stick_breaking_attention(qkv, b, s, MOBA_BLOCK)
        else:
            o = moba_attention(qkv, rel_bias, b, s)
        h = out_proj(o.reshape(t, d), w_o[i].astype(BF16), h, tm_mid)
        jj = i // 2
        if i % 2 == 0:
            h = dense_ffn(h, ffn_norm[i], w1[jj].astype(BF16), w3[jj].astype(BF16), w2[jj].astype(BF16),
                          tm_mid, 1024)
        else:
            hn, slab, counts = moe_router(h, ffn_norm[i], router[jj], tm_mid)
            f_e = e_w1.shape[-1]
            tf = f_e // 2 if (f_e // 2) % 256 == 0 else f_e
            y = moe_experts(hn, slab, counts, e_w1[jj].astype(BF16), e_w3[jj].astype(BF16),
                            e_w2[jj].astype(BF16), tm_mid, tf)
            last = i == depth - 1
            h = moe_combine(h, y, slab, final_norm if last else ffn_norm[i], tm_mid, last)
            normed = last
    if not normed:
        h = final_norm_only(h, final_norm, tm_mid)
    return h.reshape(b, s, d)
```

```python
import functools
import math

import jax
import jax.numpy as jnp
from jax import lax
from jax.experimental import pallas as pl
from jax.experimental.pallas import tpu as pltpu

N_HEADS = 16
HEAD_DIM = 64
MOBA_BLOCK = 256
MOBA_TOPK = 3
N_BUCKETS = 32
MAX_DISTANCE = 128
N_EXPERTS = 8
RMS_EPS = 1e-6
NEG_BIG = -1e30

V7X_LANES = 128
ROW_DMA_UNROLL = 8
SB_GROUP = 4
MOBA_GROUP = 4
V7X_VMEM_BYTES = 64 * 1024 * 1024
HEADS_PER_LANE_BLOCK = V7X_LANES // HEAD_DIM

F32 = jnp.float32
BF16 = jnp.bfloat16


def _cparams(semantics, vmem_mb):
    assert vmem_mb * 1024 * 1024 < V7X_VMEM_BYTES
    return pltpu.CompilerParams(dimension_semantics=semantics,
                                vmem_limit_bytes=vmem_mb * 1024 * 1024)


def _rmsnorm_f32(x, g):
    return x * lax.rsqrt(jnp.mean(x * x, axis=-1, keepdims=True) + RMS_EPS) * g


def _silu(a):
    return a * (1.0 / (1.0 + jnp.exp(-a)))


def _dot(a, b):
    return jnp.dot(a, b, preferred_element_type=F32)


def _dot_nt(a, b):
    return lax.dot_general(a, b, (((1,), (1,)), ((), ())), preferred_element_type=F32)


def _qkv_kernel(x_ref, g_ref, w_ref, o_ref, hn_ref):
    j = pl.program_id(1)

    @pl.when(j == 0)
    def _():
        hn_ref[...] = _rmsnorm_f32(x_ref[...], g_ref[...]).astype(BF16)

    scale = jnp.where(j == 0, HEAD_DIM ** -0.5, 1.0)
    o_ref[0] = (_dot(hn_ref[...], w_ref[...]) * scale).astype(o_ref.dtype)


def qkv_proj(h, g, w_bf16, tm):
    t, d = h.shape
    return pl.pallas_call(
        _qkv_kernel,
        grid=(t // tm, 3),
        in_specs=[pl.BlockSpec((tm, d), lambda i, j: (i, 0)),
                  pl.BlockSpec((1, d), lambda i, j: (0, 0)),
                  pl.BlockSpec((d, d), lambda i, j: (0, j))],
        out_specs=pl.BlockSpec((1, tm, d), lambda i, j: (j, i, 0)),
        out_shape=jax.ShapeDtypeStruct((3, t, d), BF16),
        scratch_shapes=[pltpu.VMEM((tm, d), BF16)],
        compiler_params=_cparams(("arbitrary", "arbitrary"), 40),
        name="qkv_proj",
    )(h, g.reshape(1, d), w_bf16)


def _sb_kernel(q_ref, k_ref, v_ref, o_ref, acc0_ref, acc1_ref, r0_ref, r1_ref, u_ref, *, tq):
    acc_ref = (acc0_ref, acc1_ref)
    r_ref = (r0_ref, r1_ref)
    qi = pl.program_id(2)
    row = lax.broadcasted_iota(jnp.int32, (tq, tq), 0)
    col = lax.broadcasted_iota(jnp.int32, (tq, tq), 1)

    @pl.when(qi == 0)
    def _():
        u_ref[...] = -(row > col).astype(BF16)

    q = q_ref[0, 0]
    lane = lax.broadcasted_iota(jnp.int32, (tq, V7X_LANES), 1)
    past = col < row
    heads = range(HEADS_PER_LANE_BLOCK)
    qms = [jnp.where(lane // HEAD_DIM == h, q, jnp.zeros_like(q)) for h in heads]

    def span(first, nb, diagonal):
        tasks = [(i, h) for i in reversed(range(nb)) for h in heads]
        state = [dict() for _ in tasks]
        r = [r_ref[h][...] for h in heads]
        acc = [None for h in heads]

        def stage(k, st, i, h):
            on_diagonal = diagonal and i == nb - 1
            start = pl.multiple_of((first + i) * tq, tq)
            if k == 0:
                st["z"] = _dot_nt(qms[h], k_ref[0, 0, pl.ds(start, tq), :])
            elif k == 1:
                z = st.pop("z")
                zb = z.astype(BF16)
                t = jnp.log(1 + jnp.exp(-jnp.abs(zb)))
                sp = jnp.maximum(zb, 0) + t
                if on_diagonal:
                    sp = jnp.where(past, sp, jnp.zeros_like(sp))
                st["sp"] = sp
                st["log_sig"] = jnp.minimum(z, 0.0) - t.astype(F32)
            elif k == 2:
                st["after"] = _dot(st["sp"], u_ref[...])
            elif k == 3:
                tot = st.pop("after") + r[h]
                w = jnp.exp(st.pop("log_sig") + tot)
                if on_diagonal:
                    w = jnp.where(past, w, 0.0)
                r[h] = tot[:, 0:1] - st.pop("sp")[:, 0:1].astype(F32)
                st["w"] = w.astype(BF16)
            else:
                part = _dot(st.pop("w"), v_ref[0, 0, pl.ds(start, tq), :])
                acc[h] = part if acc[h] is None else acc[h] + part

        n_stages = 5
        for step in range(len(tasks) + n_stages - 1):
            for k in reversed(range(n_stages)):
                ti = step - k
                if 0 <= ti < len(tasks):
                    stage(k, state[ti], *tasks[ti])
        for h in heads:
            r_ref[h][...] = r[h]
            acc_ref[h][...] += acc[h]

    for h in heads:
        r_ref[h][...] = jnp.zeros((tq, 1), F32)
        acc_ref[h][...] = jnp.zeros((tq, V7X_LANES), F32)

    n_groups = qi // SB_GROUP
    for in_group in range(SB_GROUP):
        @pl.when(qi % SB_GROUP == in_group)
        def _(in_group=in_group):
            span(n_groups * SB_GROUP, in_group + 1, True)

    def body(s, carry):
        span((n_groups - 1 - s) * SB_GROUP, SB_GROUP, False)
        return carry

    lax.fori_loop(0, n_groups, body, 0)

    o_ref[0] = jnp.where(lane // HEAD_DIM == 0, acc0_ref[...], acc1_ref[...]).astype(o_ref.dtype)


def stick_breaking_attention(qkv, b, s, tq):
    d = qkv.shape[-1]
    qkv4 = qkv.reshape(3, b, s, d)
    return pl.pallas_call(
        functools.partial(_sb_kernel, tq=tq),
        grid=(b, d // V7X_LANES, s // tq),
        in_specs=[pl.BlockSpec((1, 1, tq, V7X_LANES), lambda bi, hp, qi: (0, bi, qi, hp)),
                  pl.BlockSpec((1, 1, s, V7X_LANES), lambda bi, hp, qi: (1, bi, 0, hp)),
                  pl.BlockSpec((1, 1, s, V7X_LANES), lambda bi, hp, qi: (2, bi, 0, hp))],
        out_specs=pl.BlockSpec((1, tq, V7X_LANES), lambda bi, hp, qi: (bi, qi, hp)),
        out_shape=jax.ShapeDtypeStruct((b, s, d), BF16),
        scratch_shapes=[pltpu.VMEM((tq, V7X_LANES), F32), pltpu.VMEM((tq, V7X_LANES), F32),
                        pltpu.VMEM((tq, 1), F32), pltpu.VMEM((tq, 1), F32),
                        pltpu.VMEM((tq, tq), BF16)],
        compiler_params=_cparams(("arbitrary", "arbitrary", "arbitrary"), 32),
        name="stick_breaking",
    )(qkv4, qkv4, qkv4)


def _t5_bias_kernel(rb_ref, o_ref, *, tq):
    h = pl.program_id(0)
    row = lax.broadcasted_iota(jnp.int32, (tq, tq), 0)
    col = lax.broadcasted_iota(jnp.int32, (tq, tq), 1)
    max_exact = N_BUCKETS // 2
    for o in range(2):
        dist = o * tq + row - col
        n = jnp.maximum(dist, 0)
        nf = jnp.maximum(n, 1).astype(F32)
        large = max_exact + (jnp.log(nf / max_exact) / math.log(MAX_DISTANCE / max_exact)
                             * (N_BUCKETS - max_exact)).astype(jnp.int32)
        large = jnp.minimum(large, N_BUCKETS - 1)
        bucket = jnp.where(n < max_exact, n, large)
        bias = jnp.zeros((tq, tq), F32)
        for bkt in range(N_BUCKETS):
            bias = jnp.where(bucket == bkt, rb_ref[bkt, h], bias)
        if o == 0:
            bias = jnp.where(dist >= 0, bias, NEG_BIG)
        o_ref[0, o] = bias


def t5_bias_tiles(rel_bias, tq):
    return pl.pallas_call(
        functools.partial(_t5_bias_kernel, tq=tq),
        grid=(N_HEADS,),
        in_specs=[pl.BlockSpec(memory_space=pltpu.SMEM)],
        out_specs=pl.BlockSpec((1, 2, tq, tq), lambda h: (h, 0, 0, 0)),
        out_shape=jax.ShapeDtypeStruct((N_HEADS, 2, tq, tq), F32),
        compiler_params=_cparams(("arbitrary",), 16),
        name="t5_bias_tiles",
    )(rel_bias)


def _moba_kernel(rb_ref, q_ref, qall_ref, k_ref, v_ref, bt_ref, o_ref,
                 s_ref, pick_ref, mx_ref, l_ref, acc_ref, *, tq, nblk):
    hp = pl.program_id(1)
    qi = pl.program_id(2)

    heads = range(HEADS_PER_LANE_BLOCK)
    s_len = nblk * tq
    n_sel = min(MOBA_TOPK, nblk - 1)

    @pl.when(qi == 0)
    def _():
        km = jnp.concatenate(
            [jnp.mean(k_ref[0, 0, n * tq:(n + 1) * tq, :].astype(F32), axis=0, keepdims=True)
             for n in range(nblk)], axis=0)
        km_hi = km.astype(BF16)
        km_lo = (km - km_hi.astype(F32)).astype(BF16)
        q_all = qall_ref[0, 0]
        lane_all = lax.broadcasted_iota(jnp.int32, (s_len, V7X_LANES), 1)
        blk = lax.broadcasted_iota(jnp.int32, (nblk, s_len), 0)
        own = lax.broadcasted_iota(jnp.int32, (nblk, s_len), 1) // tq
        own_row = own[0:1, :]
        no_rows = jnp.zeros((V7X_LANES - nblk, s_len), F32)
        for h in heads:
            qm = jnp.where(lane_all // HEAD_DIM == h, q_all, jnp.zeros_like(q_all))
            gate = _dot_nt(km_hi, qm) + _dot_nt(km_lo, qm)
            gate = jnp.where(blk < own, gate, NEG_BIG)
            rows = [gate[n:n + 1, :] for n in range(nblk)]
            rank = [jnp.zeros((1, s_len), F32) for _ in range(nblk)]
            for n in range(nblk):
                for m in range(n):
                    m_wins = jnp.where(rows[m] >= rows[n], 1.0, 0.0)
                    rank[n] = rank[n] + m_wins
                    rank[m] = rank[m] + (1.0 - m_wins)
            chosen = [jnp.where((rank[n] < n_sel) & (n < own_row), 1.0, 0.0) for n in range(nblk)]
            pick_ref[h] = jnp.concatenate(chosen + [no_rows], axis=0).T.astype(BF16)

    q = q_ref[0, 0]
    lane = lax.broadcasted_iota(jnp.int32, (tq, V7X_LANES), 1)
    qms = [jnp.where(lane // HEAD_DIM == h, q, jnp.zeros_like(q)) for h in heads]
    both = lambda x: jnp.concatenate([x, x], axis=1)
    halves = lambda x: (x[:, :V7X_LANES], x[:, V7X_LANES:])

    big = jnp.asarray(-NEG_BIG, BF16).astype(F32)
    spread_row = lax.broadcasted_iota(jnp.int32, (V7X_LANES, V7X_LANES), 0)
    tile_rows = pl.ds(pl.multiple_of(qi * tq, tq), tq)
    for h in heads:
        mx_ref[h] = jnp.full((tq, V7X_LANES), NEG_BIG, F32)

    far_bias = [rb_ref[N_BUCKETS - 1, hp * HEADS_PER_LANE_BLOCK + h] for h in heads]

    def pipeline(tasks, stages):
        state = [dict() for _ in tasks]
        for step in range(len(tasks) + len(stages) - 1):
            for k in reversed(range(len(stages))):
                t = step - k
                if 0 <= t < len(tasks):
                    stages[k](state[t], *tasks[t])

    def logits_span(first, roles):
        mx = [mx_ref[h] for h in heads]

        def products(st, i, h):
            start = pl.multiple_of((first + i) * tq, tq)
            st["qk"] = _dot_nt(qms[h], k_ref[0, 0, pl.ds(start, tq), :])
            if roles[i] != "own":
                spread_n = jnp.where(spread_row == first + i, big, 0.0).astype(BF16)
                st["lift"] = _dot(pick_ref[h, tile_rows, :], spread_n)

        def finish(st, i, h):
            start = pl.multiple_of((first + i) * tq, tq)
            if roles[i] == "own":
                bias = bt_ref[h, 0]
            elif roles[i] == "prev":
                bias = bt_ref[h, 1] + both(st.pop("lift") - big)
            else:
                bias = both(st.pop("lift") - big + far_bias[h])
            s = st.pop("qk") + bias
            s_ref[h, :, pl.ds(start, tq)] = s
            s_lo, s_hi = halves(s)
            mx[h] = jnp.maximum(mx[h], jnp.maximum(s_lo, s_hi))

        pipeline([(i, h) for i in range(len(roles)) for h in heads], [products, finish])
        for h in heads:
            mx_ref[h] = mx[h]

    def probs_span(first, nb):
        m = [both(mx_ref[h]) for h in heads]
        l = [l_ref[h] for h in heads]
        acc = [acc_ref[h] for h in heads]

        def exponentials(st, i, h):
            start = pl.multiple_of((first + i) * tq, tq)
            p = jnp.exp(s_ref[h, :, pl.ds(start, tq)] - m[h])
            p_lo, p_hi = halves(p)
            l[h] = l[h] + (p_lo + p_hi)
            st["p"] = p.astype(BF16)

        def accumulate(st, i, h):
            start = pl.multiple_of((first + i) * tq, tq)
            acc[h] = acc[h] + _dot(st.pop("p"), v_ref[0, 0, pl.ds(start, tq), :])

        pipeline([(i, h) for i in range(nb) for h in heads], [exponentials, accumulate])
        for h in heads:
            l_ref[h] = l[h]
            acc_ref[h] = acc[h]

    group = MOBA_GROUP
    own_group = qi // group
    in_group = qi % group

    def far_groups(g, carry):
        logits_span(g * group, ["far"] * group)
        return carry

    lax.fori_loop(0, own_group - 1, far_groups, 0)

    @pl.when((own_group >= 1) & (in_group == 0))
    def _():
        logits_span((own_group - 1) * group, ["far"] * (group - 1) + ["prev"])

    @pl.when((own_group >= 1) & (in_group != 0))
    def _():
        logits_span((own_group - 1) * group, ["far"] * group)

    for size in range(1, group + 1):
        @pl.when(in_group == size - 1)
        def _(size=size):
            logits_span(own_group * group, (["far"] * group + ["prev", "own"])[-size:]
                        if size >= 2 else ["own"])

    for h in heads:
        mx_ref[h] = jnp.broadcast_to(jnp.max(mx_ref[h], axis=1, keepdims=True), (tq, V7X_LANES))
        l_ref[h] = jnp.zeros((tq, V7X_LANES), F32)
        acc_ref[h] = jnp.zeros((tq, V7X_LANES), F32)

    def prob_groups(g, carry):
        probs_span(g * group, group)
        return carry

    lax.fori_loop(0, own_group, prob_groups, 0)
    for size in range(1, group + 1):
        @pl.when(in_group == size - 1)
        def _(size=size):
            probs_span(own_group * group, size)


    out = [acc_ref[h] / jnp.sum(l_ref[h], axis=1, keepdims=True) for h in heads]
    o_ref[0] = jnp.where(lane // HEAD_DIM == 0, out[0], out[1]).astype(o_ref.dtype)


def moba_attention(qkv, rel_bias, b, s):
    d = qkv.shape[-1]
    tq = MOBA_BLOCK
    assert s % tq == 0 and tq >= 2 * MAX_DISTANCE
    nblk = s // tq
    qkv4 = qkv.reshape(3, b, s, d)
    tiles = t5_bias_tiles(rel_bias, tq)
    return pl.pallas_call(
        functools.partial(_moba_kernel, tq=tq, nblk=nblk),
        grid=(b, d // V7X_LANES, nblk),
        in_specs=[pl.BlockSpec(memory_space=pltpu.SMEM),
                  pl.BlockSpec((1, 1, tq, V7X_LANES), lambda bi, hp, qi: (0, bi, qi, hp)),
                  pl.BlockSpec((1, 1, s, V7X_LANES), lambda bi, hp, qi: (0, bi, 0, hp)),
                  pl.BlockSpec((1, 1, s, V7X_LANES), lambda bi, hp, qi: (1, bi, 0, hp)),
                  pl.BlockSpec((1, 1, s, V7X_LANES), lambda bi, hp, qi: (2, bi, 0, hp)),
                  pl.BlockSpec((HEADS_PER_LANE_BLOCK, 2, tq, tq), lambda bi, hp, qi: (hp, 0, 0, 0))],
        out_specs=pl.BlockSpec((1, tq, V7X_LANES), lambda bi, hp, qi: (bi, qi, hp)),
        out_shape=jax.ShapeDtypeStruct((b, s, d), BF16),
        scratch_shapes=[pltpu.VMEM((HEADS_PER_LANE_BLOCK, tq, s), F32),
                        pltpu.VMEM((HEADS_PER_LANE_BLOCK, s, V7X_LANES), BF16),
                        pltpu.VMEM((HEADS_PER_LANE_BLOCK, tq, V7X_LANES), F32),
                        pltpu.VMEM((HEADS_PER_LANE_BLOCK, tq, V7X_LANES), F32),
                        pltpu.VMEM((HEADS_PER_LANE_BLOCK, tq, V7X_LANES), F32)],
        compiler_params=_cparams(("arbitrary", "arbitrary", "arbitrary"), 32),
        name="moba",
    )(rel_bias, qkv4, qkv4, qkv4, qkv4, tiles)


def _whole(shape):
    return pl.BlockSpec(shape, lambda i: (0,) * len(shape), pipeline_mode=pl.Buffered(1))


def _ffn_kernel(o_ref, wo_ref, h_ref, g_ref, w1_ref, w3_ref, w2_ref, out_ref, *, chunks):
    x = h_ref[...] + _dot(o_ref[...], wo_ref[...])
    hn = _rmsnorm_f32(x, g_ref[...]).astype(BF16)
    acc = x
    for c0, c1 in chunks:
        a = _dot(hn, w1_ref[:, c0:c1])
        b = _dot(hn, w3_ref[:, c0:c1])
        acc = acc + _dot((_silu(a) * b).astype(BF16), w2_ref[c0:c1, :])
    out_ref[...] = acc


def oproj_dense_ffn(o, w_o, h, g, w1, w3, w2, tm, chunk):
    t, d = h.shape
    f = w1.shape[1]
    chunks = tuple((c, min(c + chunk, f)) for c in range(0, f, chunk))
    return pl.pallas_call(
        functools.partial(_ffn_kernel, chunks=chunks),
        grid=(t // tm,),
        in_specs=[pl.BlockSpec((tm, d), lambda i: (i, 0)),
                  _whole((d, d)),
                  pl.BlockSpec((tm, d), lambda i: (i, 0)),
                  pl.BlockSpec((1, d), lambda i: (0, 0)),
                  _whole((d, f)), _whole((d, f)), _whole((f, d))],
        out_specs=pl.BlockSpec((tm, d), lambda i: (i, 0)),
        out_shape=jax.ShapeDtypeStruct((t, d), F32),
        compiler_params=_cparams(("arbitrary",), 56),
        name="oproj_dense_ffn",
    )(o, w_o, h, g.reshape(1, d), w1, w3, w2)


def _row_tile_chunks(d):
    assert d % V7X_LANES == 0
    return d // V7X_LANES


def _store_row_tiles(ref, x):
    n, d = x.shape
    c = _row_tile_chunks(d)
    for k in range(c):
        ref[pl.ds(k, n, stride=c), :] = x[:, k * V7X_LANES:(k + 1) * V7X_LANES]


def _load_row_tiles(ref, n, d):
    c = _row_tile_chunks(d)
    return [ref[pl.ds(k, n, stride=c), :] for k in range(c)]


COL_E0, COL_E1, COL_W0, COL_W1, COL_R0, COL_R1 = range(6)


def _router_kernel(o_ref, wo_ref, h_ref, g_ref, wr_ref, h1_ref, hn_ref, slab_ref, cnt_ref, carry_ref, *, tm):
    i = pl.program_id(0)

    @pl.when(i == 0)
    def _():
        carry_ref[...] = jnp.zeros_like(carry_ref)

    h1 = h_ref[...] + _dot(o_ref[...], wo_ref[...])
    h1_ref[...] = h1
    hn = _rmsnorm_f32(h1, g_ref[...])
    _store_row_tiles(hn_ref, hn)
    hi = hn.astype(BF16)
    lo = (hn - hi.astype(F32)).astype(BF16)
    w = wr_ref[...]
    w_hi = w.astype(BF16)
    w_lo = (w - w_hi.astype(F32)).astype(BF16)
    logits = _dot(hi, w_hi) + _dot(hi, w_lo) + _dot(lo, w_hi)

    lane = lax.broadcasted_iota(jnp.int32, (tm, V7X_LANES), 1)
    neg_inf = jnp.float32(-jnp.inf)
    lg = jnp.where(lane < N_EXPERTS, logits, neg_inf)
    m0 = jnp.max(lg, axis=1, keepdims=True)
    i0 = jnp.min(jnp.where(lg == m0, lane, V7X_LANES), axis=1, keepdims=True)
    lg1 = jnp.where(lane == i0, neg_inf, lg)
    m1 = jnp.max(lg1, axis=1, keepdims=True)
    i1 = jnp.min(jnp.where(lg1 == m1, lane, V7X_LANES), axis=1, keepdims=True)
    e = jnp.exp(m1 - m0)
    w0 = 1.0 / (1.0 + e)
    w1 = e / (1.0 + e)

    pick0 = lane == i0
    pick1 = lane == i1
    hot = (pick0 | pick1).astype(BF16)
    row = lax.broadcasted_iota(jnp.int32, (tm, tm), 0)
    col = lax.broadcasted_iota(jnp.int32, (tm, tm), 1)
    before = _dot((col < row).astype(BF16), hot) + carry_ref[0:1, :]
    r0 = jnp.sum(jnp.where(pick0, before, 0.0), axis=1, keepdims=True)
    r1 = jnp.sum(jnp.where(pick1, before, 0.0), axis=1, keepdims=True)
    carry_ref[0:1, :] = carry_ref[0:1, :] + jnp.sum(hot.astype(F32), axis=0, keepdims=True)

    slab = jnp.zeros((tm, V7X_LANES), F32)
    for c, val in ((COL_E0, i0.astype(F32)), (COL_E1, i1.astype(F32)), (COL_W0, w0), (COL_W1, w1),
                   (COL_R0, r0), (COL_R1, r1)):
        slab = jnp.where(lane == c, val, slab)
    slab_ref[...] = slab
    cnt_ref[...] = carry_ref[...]


def oproj_moe_router(o, w_o, h, g, w_router, tm):
    t, d = h.shape
    wr = jnp.zeros((d, V7X_LANES), F32).at[:, :N_EXPERTS].set(w_router)
    return pl.pallas_call(
        functools.partial(_router_kernel, tm=tm),
        grid=(t // tm,),
        in_specs=[pl.BlockSpec((tm, d), lambda i: (i, 0)),
                  _whole((d, d)),
                  pl.BlockSpec((tm, d), lambda i: (i, 0)),
                  pl.BlockSpec((1, d), lambda i: (0, 0)),
                  pl.BlockSpec((d, V7X_LANES), lambda i: (0, 0))],
        out_specs=[pl.BlockSpec((tm, d), lambda i: (i, 0)),
                   pl.BlockSpec((tm * d // V7X_LANES, V7X_LANES), lambda i: (i, 0)),
                   pl.BlockSpec((tm, V7X_LANES), lambda i: (i, 0)),
                   pl.BlockSpec((8, V7X_LANES), lambda i: (0, 0))],
        out_shape=[jax.ShapeDtypeStruct((t, d), F32),
                   jax.ShapeDtypeStruct((t * d // V7X_LANES, V7X_LANES), F32),
                   jax.ShapeDtypeStruct((t, V7X_LANES), F32),
                   jax.ShapeDtypeStruct((8, V7X_LANES), F32)],
        scratch_shapes=[pltpu.VMEM((8, V7X_LANES), F32)],
        compiler_params=_cparams(("arbitrary",), 40),
        name="oproj_moe_router",
    )(o, w_o, h, g.reshape(1, d), wr)


def _moe_kernel(te_ref, nu_ref, tokc_ref, tokn_ref, dst_ref, hn_hbm, w1_ref, w3_ref, w2_ref, y_hbm,
                xbuf, xb16, acc_ref, ybuf, gsem, ssem, *, tm, nf, nt, n_real):
    i = pl.program_id(0)
    j = pl.program_id(1)
    nu = nu_ref[0]
    slot = i % 2
    d = acc_ref.shape[1]
    c = _row_tile_chunks(d)

    def row_loop(start_row_copy):
        for r in range(tm):
            start_row_copy(r)

    def row(r):
        return pl.ds(r * c, c) if isinstance(r, int) else pl.ds(pl.multiple_of(r * c, c), c)

    def start_gather(tok_ref, s):
        row_loop(lambda r: pltpu.make_async_copy(
            hn_hbm.at[row(tok_ref[0, 0, r]), :], xbuf.at[s, row(r), :], gsem.at[s]).start())

    def wait_gather(s):
        pltpu.make_async_copy(hn_hbm.at[pl.ds(0, tm * c), :], xbuf.at[s], gsem.at[s]).wait()

    def start_scatter():
        row_loop(lambda r: pltpu.make_async_copy(
            ybuf.at[row(r), :], y_hbm.at[row(dst_ref[0, 0, r]), :], ssem.at[0]).start())

    def spare_rows_copy():
        return pltpu.make_async_copy(ybuf, y_hbm.at[pl.ds(n_real * c, tm * c), :], ssem.at[0])

    def wait_scatter():
        spare_rows_copy().wait()

    @pl.when((j == 0) & (i == 0))
    def _():
        ybuf[...] = jnp.zeros_like(ybuf)
        spare_rows_copy().start()
        spare_rows_copy().wait()

        @pl.when(nu > 0)
        def _():
            start_gather(tokc_ref, 0)

    @pl.when((j == 0) & (i < nu))
    def _():
        wait_gather(slot)
        for k, chunk in enumerate(_load_row_tiles(xbuf.at[slot], tm, d)):
            xb16[:, k * V7X_LANES:(k + 1) * V7X_LANES] = chunk.astype(BF16)

    @pl.when((j == 0) & (i + 1 < nu))
    def _():
        start_gather(tokn_ref, 1 - slot)

    @pl.when(i < nu)
    def _():
        x = xb16[...]
        a = _dot(x, w1_ref[0])
        b = _dot(x, w3_ref[0])
        part = _dot((_silu(a) * b).astype(BF16), w2_ref[0])

        @pl.when(j == 0)
        def _():
            acc_ref[...] = part

        @pl.when(j > 0)
        def _():
            acc_ref[...] += part

    last = j == nf - 1

    @pl.when(last & (i >= 1) & (i - 1 < nu))
    def _():
        wait_scatter()

    @pl.when(last & (i < nu))
    def _():
        _store_row_tiles(ybuf, acc_ref[...])
        start_scatter()

    @pl.when(last & (i == nt - 1) & (i < nu))
    def _():
        wait_scatter()


def moe_experts(hn, slab, counts, w1, w3, w2, tm, tf):
    t = slab.shape[0]
    d = w1.shape[1]
    c = _row_tile_chunks(d)
    f = w1.shape[2]
    nf = f // tf
    nt = (2 * t) // tm + N_EXPERTS
    rows = nt * tm

    e0 = slab[:, COL_E0].astype(jnp.int32)
    e1 = slab[:, COL_E1].astype(jnp.int32)
    r0 = slab[:, COL_R0].astype(jnp.int32)
    r1 = slab[:, COL_R1].astype(jnp.int32)
    cnt = counts[0, :N_EXPERTS].astype(jnp.int32)
    tiles = (cnt + tm - 1) // tm
    tile_end = jnp.cumsum(tiles)
    tile_start = tile_end - tiles
    nu = tile_end[-1:]
    offs = tile_start * tm
    dest0 = offs[e0] + r0
    dest1 = offs[e1] + r1
    tile_expert = jnp.minimum(
        jnp.sum(jnp.arange(nt, dtype=jnp.int32)[:, None] >= tile_end[None, :], axis=1), N_EXPERTS - 1
    ).astype(jnp.int32)
    tok_ids = jnp.arange(t, dtype=jnp.int32)
    spare = 2 * t + (jnp.arange(rows, dtype=jnp.int32) % tm)
    dst_sorted = spare.at[jnp.concatenate([dest0, dest1])].set(
        jnp.concatenate([tok_ids, t + tok_ids]), unique_indices=True, mode="promise_in_bounds")
    tok_sorted = jnp.where(dst_sorted >= 2 * t, 0, jnp.where(dst_sorted >= t, dst_sorted - t, dst_sorted))
    tok3 = tok_sorted.reshape(nt, 1, tm)
    dst3 = dst_sorted.reshape(nt, 1, tm)

    def w_in(shape, which):
        def index_map(i, j, te, nu_):
            ii = jnp.minimum(i, nu_[0] - 1)
            jj = jnp.where(i < nu_[0], j, nf - 1)
            return (te[ii], 0, jj) if which == "up" else (te[ii], jj, 0)
        return pl.BlockSpec(shape, index_map)

    smem_tile = lambda f_: pl.BlockSpec((1, 1, tm), f_, memory_space=pltpu.SMEM)
    grid_spec = pltpu.PrefetchScalarGridSpec(
        num_scalar_prefetch=2,
        grid=(nt, nf),
        in_specs=[smem_tile(lambda i, j, te, nu_: (i, 0, 0)),
                  smem_tile(lambda i, j, te, nu_: (jnp.minimum(i + 1, nt - 1), 0, 0)),
                  smem_tile(lambda i, j, te, nu_: (i, 0, 0)),
                  pl.BlockSpec(memory_space=pl.ANY),
                  w_in((1, d, tf), "up"), w_in((1, d, tf), "up"), w_in((1, tf, d), "down")],
        out_specs=pl.BlockSpec(memory_space=pl.ANY),
        scratch_shapes=[pltpu.VMEM((2, tm * c, V7X_LANES), F32),
                        pltpu.VMEM((tm, d), BF16),
                        pltpu.VMEM((tm, d), F32),
                        pltpu.VMEM((tm * c, V7X_LANES), F32),
                        pltpu.SemaphoreType.DMA((2,)),
                        pltpu.SemaphoreType.DMA((1,))],
    )
    return pl.pallas_call(
        functools.partial(_moe_kernel, tm=tm, nf=nf, nt=nt, n_real=2 * t),
        grid_spec=grid_spec,
        out_shape=jax.ShapeDtypeStruct(((2 * t + tm) * c, V7X_LANES), F32),
        compiler_params=_cparams(("arbitrary", "arbitrary"), 56),
        name="moe_experts",
    )(tile_expert, nu, tok3, tok3, dst3, hn, w1, w3, w2)


def _combine_kernel(h_ref, y0_ref, y1_ref, slab_ref, g_ref, out_ref, *, final):
    slab = slab_ref[...]
    tm, d = h_ref.shape
    y0 = jnp.concatenate(_load_row_tiles(y0_ref, tm, d), axis=1)
    y1 = jnp.concatenate(_load_row_tiles(y1_ref, tm, d), axis=1)
    x = h_ref[...] + slab[:, COL_W0:COL_W0 + 1] * y0 + slab[:, COL_W1:COL_W1 + 1] * y1
    out_ref[...] = _rmsnorm_f32(x, g_ref[...]) if final else x


def moe_combine(h, y, slab, g, tm, final):
    t, d = h.shape
    nb = t // tm
    c = _row_tile_chunks(d)
    return pl.pallas_call(
        functools.partial(_combine_kernel, final=final),
        grid=(nb,),
        in_specs=[pl.BlockSpec((tm, d), lambda i: (i, 0)),
                  pl.BlockSpec((tm * c, V7X_LANES), lambda i: (i, 0)),
                  pl.BlockSpec((tm * c, V7X_LANES), lambda i: (i + nb, 0)),
                  pl.BlockSpec((tm, V7X_LANES), lambda i: (i, 0)),
                  pl.BlockSpec((1, d), lambda i: (0, 0))],
        out_specs=pl.BlockSpec((tm, d), lambda i: (i, 0)),
        out_shape=jax.ShapeDtypeStruct((t, d), F32),
        compiler_params=_cparams(("arbitrary",), 40),
        name="moe_combine",
    )(h, y, y, slab, g.reshape(1, d))


def _norm_kernel(h_ref, g_ref, out_ref):
    out_ref[...] = _rmsnorm_f32(h_ref[...], g_ref[...])


def final_norm_only(h, g, tm):
    t, d = h.shape
    return pl.pallas_call(
        _norm_kernel,
        grid=(t // tm,),
        in_specs=[pl.BlockSpec((tm, d), lambda i: (i, 0)), pl.BlockSpec((1, d), lambda i: (0, 0))],
        out_specs=pl.BlockSpec((tm, d), lambda i: (i, 0)),
        out_shape=jax.ShapeDtypeStruct((t, d), F32),
        compiler_params=_cparams(("arbitrary",), 40),
        name="final_norm",
    )(h, g.reshape(1, d))


def _row_tile(t, want):
    tm = min(want, t)
    assert t % tm == 0
    return tm


def kernel(x, w_qkv, w_o, mixer_norm, ffn_norm, rel_bias, w1, w3, w2, router, e_w1, e_w3, e_w2, final_norm):
    b, s, d = x.shape
    assert d == N_HEADS * HEAD_DIM and s % MOBA_BLOCK == 0
    t = b * s
    depth = w_qkv.shape[0]
    h = x.reshape(t, d)
    tm_big = _row_tile(t, 1024)
    tm_mid = _row_tile(t, 512)
    normed = False
    for i in range(depth):
        qkv = qkv_proj(h, mixer_norm[i], w_qkv[i].astype(BF16), tm_big)
        if i % 2 == 0:
            o = stick_breaking_attention(qkv, b, s, MOBA_BLOCK)
        else:
            o = moba_attention(qkv, rel_bias, b, s)
        o = o.reshape(t, d)
        wo = w_o[i].astype(BF16)
        jj = i // 2
        if i % 2 == 0:
            h = oproj_dense_ffn(o, wo, h, ffn_norm[i], w1[jj].astype(BF16), w3[jj].astype(BF16),
                                w2[jj].astype(BF16), tm_mid, 1024)
        else:
            h, hn, slab, counts = oproj_moe_router(o, wo, h, ffn_norm[i], router[jj], tm_mid)
            f_e = e_w1.shape[-1]
            tf = f_e // 2 if (f_e // 2) % 256 == 0 else f_e
            y = moe_experts(hn, slab, counts, e_w1[jj].astype(BF16), e_w3[jj].astype(BF16),
                            e_w2[jj].astype(BF16), tm_mid, tf)
            last = i == depth - 1
            h = moe_combine(h, y, slab, final_norm if last else ffn_norm[i], tm_mid, last)
            normed = last
    if not normed:
        h = final_norm_only(h, final_norm, tm_mid)
    return h.reshape(b, s, d)
```

```python
import functools
import math

import jax
import jax.numpy as jnp
from jax import lax
from jax.experimental import pallas as pl
from jax.experimental.pallas import tpu as pltpu

N_HEADS = 16
HEAD_DIM = 64
MOBA_BLOCK = 256
MOBA_TOPK = 3
N_BUCKETS = 32
MAX_DISTANCE = 128
N_EXPERTS = 8
RMS_EPS = 1e-6
NEG_BIG = -1e30

V7X_LANES = 128
ROW_DMA_UNROLL = 8
SB_GROUP = 4
MOBA_GROUP = 4
V7X_VMEM_BYTES = 64 * 1024 * 1024
HEADS_PER_LANE_BLOCK = V7X_LANES // HEAD_DIM

F32 = jnp.float32
BF16 = jnp.bfloat16


def _cparams(semantics, vmem_mb):
    assert vmem_mb * 1024 * 1024 < V7X_VMEM_BYTES
    return pltpu.CompilerParams(dimension_semantics=semantics,
                                vmem_limit_bytes=vmem_mb * 1024 * 1024)


def _rmsnorm_f32(x, g):
    return x * lax.rsqrt(jnp.mean(x * x, axis=-1, keepdims=True) + RMS_EPS) * g


def _silu(a):
    return a * (1.0 / (1.0 + jnp.exp(-a)))


def _dot(a, b):
    return jnp.dot(a, b, preferred_element_type=F32)


def _dot_nt(a, b):
    return lax.dot_general(a, b, (((1,), (1,)), ((), ())), preferred_element_type=F32)


def _qkv_kernel(x_ref, g_ref, w_ref, o_ref, hn_ref):
    j = pl.program_id(1)

    @pl.when(j == 0)
    def _():
        hn_ref[...] = _rmsnorm_f32(x_ref[...], g_ref[...]).astype(BF16)

    scale = jnp.where(j == 0, HEAD_DIM ** -0.5, 1.0)
    o_ref[0] = (_dot(hn_ref[...], w_ref[...]) * scale).astype(o_ref.dtype)


def qkv_proj(h, g, w_bf16, tm):
    t, d = h.shape
    return pl.pallas_call(
        _qkv_kernel,
        grid=(t // tm, 3),
        in_specs=[pl.BlockSpec((tm, d), lambda i, j: (i, 0)),
                  pl.BlockSpec((1, d), lambda i, j: (0, 0)),
                  pl.BlockSpec((d, d), lambda i, j: (0, j))],
        out_specs=pl.BlockSpec((1, tm, d), lambda i, j: (j, i, 0)),
        out_shape=jax.ShapeDtypeStruct((3, t, d), BF16),
        scratch_shapes=[pltpu.VMEM((tm, d), BF16)],
        compiler_params=_cparams(("arbitrary", "arbitrary"), 40),
        name="qkv_proj",
    )(h, g.reshape(1, d), w_bf16)


def _sb_kernel(q_ref, k_ref, v_ref, o_ref, acc0_ref, acc1_ref, r0_ref, r1_ref, u_ref, *, tq):
    acc_ref = (acc0_ref, acc1_ref)
    r_ref = (r0_ref, r1_ref)
    qi = pl.program_id(2)
    row = lax.broadcasted_iota(jnp.int32, (tq, tq), 0)
    col = lax.broadcasted_iota(jnp.int32, (tq, tq), 1)

    @pl.when(qi == 0)
    def _():
        u_ref[...] = -(row > col).astype(BF16)

    q = q_ref[0, 0]
    lane = lax.broadcasted_iota(jnp.int32, (tq, V7X_LANES), 1)
    past = col < row
    heads = range(HEADS_PER_LANE_BLOCK)
    qms = [jnp.where(lane // HEAD_DIM == h, q, jnp.zeros_like(q)) for h in heads]

    def span(first, nb, diagonal):
        tasks = [(i, h) for i in reversed(range(nb)) for h in heads]
        state = [dict() for _ in tasks]
        r = [r_ref[h][...] for h in heads]
        acc = [None for h in heads]

        def stage(k, st, i, h):
            on_diagonal = diagonal and i == nb - 1
            start = pl.multiple_of((first + i) * tq, tq)
            if k == 0:
                st["z"] = _dot_nt(qms[h], k_ref[0, 0, pl.ds(start, tq), :])
            elif k == 1:
                z = st.pop("z")
                zb = z.astype(BF16)
                t = jnp.log(1 + jnp.exp(-jnp.abs(zb)))
                sp = jnp.maximum(zb, 0) + t
                if on_diagonal:
                    sp = jnp.where(past, sp, jnp.zeros_like(sp))
                st["sp"] = sp
                st["log_sig"] = jnp.minimum(z, 0.0) - t.astype(F32)
            elif k == 2:
                st["after"] = _dot(st["sp"], u_ref[...])
            elif k == 3:
                tot = st.pop("after") + r[h]
                w = jnp.exp(st.pop("log_sig") + tot)
                if on_diagonal:
                    w = jnp.where(past, w, 0.0)
                r[h] = tot[:, 0:1] - st.pop("sp")[:, 0:1].astype(F32)
                st["w"] = w.astype(BF16)
            else:
                part = _dot(st.pop("w"), v_ref[0, 0, pl.ds(start, tq), :])
                acc[h] = part if acc[h] is None else acc[h] + part

        n_stages = 5
        for step in range(len(tasks) + n_stages - 1):
            for k in reversed(range(n_stages)):
                ti = step - k
                if 0 <= ti < len(tasks):
                    stage(k, state[ti], *tasks[ti])
        for h in heads:
            r_ref[h][...] = r[h]
            acc_ref[h][...] += acc[h]

    for h in heads:
        r_ref[h][...] = jnp.zeros((tq, 1), F32)
        acc_ref[h][...] = jnp.zeros((tq, V7X_LANES), F32)

    n_groups = qi // SB_GROUP
    for in_group in range(SB_GROUP):
        @pl.when(qi % SB_GROUP == in_group)
        def _(in_group=in_group):
            span(n_groups * SB_GROUP, in_group + 1, True)

    def body(s, carry):
        span((n_groups - 1 - s) * SB_GROUP, SB_GROUP, False)
        return carry

    lax.fori_loop(0, n_groups, body, 0)

    o_ref[0] = jnp.where(lane // HEAD_DIM == 0, acc0_ref[...], acc1_ref[...]).astype(o_ref.dtype)


def stick_breaking_attention(qkv, b, s, tq):
    d = qkv.shape[-1]
    qkv4 = qkv.reshape(3, b, s, d)
    return pl.pallas_call(
        functools.partial(_sb_kernel, tq=tq),
        grid=(b, d // V7X_LANES, s // tq),
        in_specs=[pl.BlockSpec((1, 1, tq, V7X_LANES), lambda bi, hp, qi: (0, bi, qi, hp)),
                  pl.BlockSpec((1, 1, s, V7X_LANES), lambda bi, hp, qi: (1, bi, 0, hp)),
                  pl.BlockSpec((1, 1, s, V7X_LANES), lambda bi, hp, qi: (2, bi, 0, hp))],
        out_specs=pl.BlockSpec((1, tq, V7X_LANES), lambda bi, hp, qi: (bi, qi, hp)),
        out_shape=jax.ShapeDtypeStruct((b, s, d), BF16),
        scratch_shapes=[pltpu.VMEM((tq, V7X_LANES), F32), pltpu.VMEM((tq, V7X_LANES), F32),
                        pltpu.VMEM((tq, 1), F32), pltpu.VMEM((tq, 1), F32),
                        pltpu.VMEM((tq, tq), BF16)],
        compiler_params=_cparams(("arbitrary", "arbitrary", "arbitrary"), 32),
        name="stick_breaking",
    )(qkv4, qkv4, qkv4)


def _t5_bias_kernel(rb_ref, o_ref, *, tq):
    h = pl.program_id(0)
    row = lax.broadcasted_iota(jnp.int32, (tq, tq), 0)
    col = lax.broadcasted_iota(jnp.int32, (tq, tq), 1)
    max_exact = N_BUCKETS // 2
    for o in range(2):
        dist = o * tq + row - col
        n = jnp.maximum(dist, 0)
        nf = jnp.maximum(n, 1).astype(F32)
        large = max_exact + (jnp.log(nf / max_exact) / math.log(MAX_DISTANCE / max_exact)
                             * (N_BUCKETS - max_exact)).astype(jnp.int32)
        large = jnp.minimum(large, N_BUCKETS - 1)
        bucket = jnp.where(n < max_exact, n, large)
        bias = jnp.zeros((tq, tq), F32)
        for bkt in range(N_BUCKETS):
            bias = jnp.where(bucket == bkt, rb_ref[bkt, h], bias)
        if o == 0:
            bias = jnp.where(dist >= 0, bias, NEG_BIG)
        o_ref[0, o] = bias


def t5_bias_tiles(rel_bias, tq):
    return pl.pallas_call(
        functools.partial(_t5_bias_kernel, tq=tq),
        grid=(N_HEADS,),
        in_specs=[pl.BlockSpec(memory_space=pltpu.SMEM)],
        out_specs=pl.BlockSpec((1, 2, tq, tq), lambda h: (h, 0, 0, 0)),
        out_shape=jax.ShapeDtypeStruct((N_HEADS, 2, tq, tq), F32),
        compiler_params=_cparams(("arbitrary",), 16),
        name="t5_bias_tiles",
    )(rel_bias)


def _moba_kernel(rb_ref, q_ref, qall_ref, k_ref, v_ref, bt_ref, o_ref,
                 s_ref, qaux_ref, mx_ref, l_ref, acc_ref, *, tq, nblk):
    hp = pl.program_id(1)
    qi = pl.program_id(2)

    heads = range(HEADS_PER_LANE_BLOCK)
    s_len = nblk * tq
    n_sel = min(MOBA_TOPK, nblk - 1)
    big = jnp.asarray(-NEG_BIG, BF16).astype(F32)
    aux_lane0 = lambda h: ((h + 1) % HEADS_PER_LANE_BLOCK) * HEAD_DIM
    assert nblk + 2 <= HEAD_DIM and HEADS_PER_LANE_BLOCK == 2

    @pl.when(qi == 0)
    def _():
        km = jnp.concatenate(
            [jnp.mean(k_ref[0, 0, n * tq:(n + 1) * tq, :].astype(F32), axis=0, keepdims=True)
             for n in range(nblk)], axis=0)
        km_hi = km.astype(BF16)
        km_lo = (km - km_hi.astype(F32)).astype(BF16)
        q_all = qall_ref[0, 0]
        lane_all = lax.broadcasted_iota(jnp.int32, (s_len, V7X_LANES), 1)
        blk = lax.broadcasted_iota(jnp.int32, (nblk, s_len), 0)
        own = lax.broadcasted_iota(jnp.int32, (nblk, s_len), 1) // tq
        own_row = own[0:1, :]
        for h in heads:
            qm = jnp.where(lane_all // HEAD_DIM == h, q_all, jnp.zeros_like(q_all))
            gate = _dot_nt(km_hi, qm) + _dot_nt(km_lo, qm)
            gate = jnp.where(blk < own, gate, NEG_BIG)
            rows = [gate[n:n + 1, :] for n in range(nblk)]
            rank = [jnp.zeros((1, s_len), F32) for _ in range(nblk)]
            for n in range(nblk):
                for m in range(n):
                    m_wins = jnp.where(rows[m] >= rows[n], 1.0, 0.0)
                    rank[n] = rank[n] + m_wins
                    rank[m] = rank[m] + (1.0 - m_wins)
            aux = [jnp.where((rank[n] < n_sel) & (n < own_row), 0.0, -big) for n in range(nblk)]
            aux += [jnp.ones((1, s_len), F32)] * 2
            before = jnp.zeros((aux_lane0(h), s_len), F32)
            after = jnp.zeros((V7X_LANES - aux_lane0(h) - len(aux), s_len), F32)
            pieces = ([before] if before.shape[0] else []) + aux + [after]
            qaux_ref[h] = jnp.concatenate(pieces, axis=0).T.astype(BF16)

    q = q_ref[0, 0]
    lane = lax.broadcasted_iota(jnp.int32, (tq, V7X_LANES), 1)
    tile_rows = pl.ds(pl.multiple_of(qi * tq, tq), tq)
    q_aug = [jnp.where(lane // HEAD_DIM == h, q, qaux_ref[h, tile_rows, :]) for h in heads]
    both = lambda x: jnp.concatenate([x, x], axis=1)
    halves = lambda x: (x[:, :V7X_LANES], x[:, V7X_LANES:])
    for h in heads:
        mx_ref[h] = jnp.full((tq, V7X_LANES), NEG_BIG, F32)

    aux_rows = 16
    lane_aux = lax.broadcasted_iota(jnp.int32, (aux_rows, V7X_LANES), 1)
    key_aux_far = []
    for h in heads:
        far = jnp.full((aux_rows, V7X_LANES), rb_ref[N_BUCKETS - 1, hp * HEADS_PER_LANE_BLOCK + h], F32)
        far_hi = far.astype(BF16).astype(F32)
        far_lo = (far - far_hi).astype(BF16).astype(F32)
        key_aux_far.append(jnp.where(lane_aux == aux_lane0(h) + nblk, far_hi,
                                     jnp.where(lane_aux == aux_lane0(h) + nblk + 1, far_lo, 0.0)))

    def key_operand(h, n, role):
        start = pl.multiple_of(n * tq, tq)
        kb = k_ref[0, 0, pl.ds(start, tq), :]
        if role == "own":
            aux = jnp.zeros((aux_rows, V7X_LANES), F32)
        else:
            base = key_aux_far[h] if role == "far" else jnp.zeros((aux_rows, V7X_LANES), F32)
            aux = jnp.where(lane_aux == aux_lane0(h) + n, 1.0, base)
        aux = jnp.concatenate([aux.astype(BF16)] * (tq // aux_rows), axis=0)
        return jnp.where(lane // HEAD_DIM == h, kb, aux)

    def pipeline(tasks, stages):
        state = [dict() for _ in tasks]
        for step in range(len(tasks) + len(stages) - 1):
            for k in reversed(range(len(stages))):
                t = step - k
                if 0 <= t < len(tasks):
                    stages[k](state[t], *tasks[t])

    def logits_span(first, roles):
        mx = [mx_ref[h] for h in heads]

        def products(st, i, h):
            st["s"] = _dot_nt(q_aug[h], key_operand(h, first + i, roles[i]))

        def finish(st, i, h):
            start = pl.multiple_of((first + i) * tq, tq)
            s = st.pop("s")
            if roles[i] == "own":
                s = s + bt_ref[h, 0]
            elif roles[i] == "prev":
                s = s + bt_ref[h, 1]
            s_ref[h, :, pl.ds(start, tq)] = s
            s_lo, s_hi = halves(s)
            mx[h] = jnp.maximum(mx[h], jnp.maximum(s_lo, s_hi))

        pipeline([(i, h) for i in range(len(roles)) for h in heads], [products, finish])
        for h in heads:
            mx_ref[h] = mx[h]

    def probs_span(first, nb):
        m = [both(mx_ref[h]) for h in heads]
        l = [l_ref[h] for h in heads]
        acc = [acc_ref[h] for h in heads]

        def exponentials(st, i, h):
            start = pl.multiple_of((first + i) * tq, tq)
            p = jnp.exp(s_ref[h, :, pl.ds(start, tq)] - m[h])
            p_lo, p_hi = halves(p)
            l[h] = l[h] + (p_lo + p_hi)
            st["p"] = p.astype(BF16)

        def accumulate(st, i, h):
            start = pl.multiple_of((first + i) * tq, tq)
            acc[h] = acc[h] + _dot(st.pop("p"), v_ref[0, 0, pl.ds(start, tq), :])

        pipeline([(i, h) for i in range(nb) for h in heads], [exponentials, accumulate])
        for h in heads:
            l_ref[h] = l[h]
            acc_ref[h] = acc[h]

    group = MOBA_GROUP
    own_group = qi // group
    in_group = qi % group

    def far_groups(g, carry):
        logits_span(g * group, ["far"] * group)
        return carry

    lax.fori_loop(0, own_group - 1, far_groups, 0)

    @pl.when((own_group >= 1) & (in_group == 0))
    def _():
        logits_span((own_group - 1) * group, ["far"] * (group - 1) + ["prev"])

    @pl.when((own_group >= 1) & (in_group != 0))
    def _():
        logits_span((own_group - 1) * group, ["far"] * group)

    for size in range(1, group + 1):
        @pl.when(in_group == size - 1)
        def _(size=size):
            logits_span(own_group * group, (["far"] * group + ["prev", "own"])[-size:]
                        if size >= 2 else ["own"])

    for h in heads:
        mx_ref[h] = jnp.broadcast_to(jnp.max(mx_ref[h], axis=1, keepdims=True), (tq, V7X_LANES))
        l_ref[h] = jnp.zeros((tq, V7X_LANES), F32)
        acc_ref[h] = jnp.zeros((tq, V7X_LANES), F32)

    def prob_groups(g, carry):
        probs_span(g * group, group)
        return carry

    lax.fori_loop(0, own_group, prob_groups, 0)
    for size in range(1, group + 1):
        @pl.when(in_group == size - 1)
        def _(size=size):
            probs_span(own_group * group, size)


    out = [acc_ref[h] / jnp.sum(l_ref[h], axis=1, keepdims=True) for h in heads]
    o_ref[0] = jnp.where(lane // HEAD_DIM == 0, out[0], out[1]).astype(o_ref.dtype)


def moba_attention(qkv, rel_bias, b, s):
    d = qkv.shape[-1]
    tq = MOBA_BLOCK
    assert s % tq == 0 and tq >= 2 * MAX_DISTANCE
    nblk = s // tq
    qkv4 = qkv.reshape(3, b, s, d)
    tiles = t5_bias_tiles(rel_bias, tq)
    return pl.pallas_call(
        functools.partial(_moba_kernel, tq=tq, nblk=nblk),
        grid=(b, d // V7X_LANES, nblk),
        in_specs=[pl.BlockSpec(memory_space=pltpu.SMEM),
                  pl.BlockSpec((1, 1, tq, V7X_LANES), lambda bi, hp, qi: (0, bi, qi, hp)),
                  pl.BlockSpec((1, 1, s, V7X_LANES), lambda bi, hp, qi: (0, bi, 0, hp)),
                  pl.BlockSpec((1, 1, s, V7X_LANES), lambda bi, hp, qi: (1, bi, 0, hp)),
                  pl.BlockSpec((1, 1, s, V7X_LANES), lambda bi, hp, qi: (2, bi, 0, hp)),
                  pl.BlockSpec((HEADS_PER_LANE_BLOCK, 2, tq, tq), lambda bi, hp, qi: (hp, 0, 0, 0))],
        out_specs=pl.BlockSpec((1, tq, V7X_LANES), lambda bi, hp, qi: (bi, qi, hp)),
        out_shape=jax.ShapeDtypeStruct((b, s, d), BF16),
        scratch_shapes=[pltpu.VMEM((HEADS_PER_LANE_BLOCK, tq, s), F32),
                        pltpu.VMEM((HEADS_PER_LANE_BLOCK, s, V7X_LANES), BF16),
                        pltpu.VMEM((HEADS_PER_LANE_BLOCK, tq, V7X_LANES), F32),
                        pltpu.VMEM((HEADS_PER_LANE_BLOCK, tq, V7X_LANES), F32),
                        pltpu.VMEM((HEADS_PER_LANE_BLOCK, tq, V7X_LANES), F32)],
        compiler_params=_cparams(("arbitrary", "arbitrary", "arbitrary"), 32),
        name="moba",
    )(rel_bias, qkv4, qkv4, qkv4, qkv4, tiles)


def _whole(shape):
    return pl.BlockSpec(shape, lambda i: (0,) * len(shape), pipeline_mode=pl.Buffered(1))


def _ffn_kernel(o_ref, wo_ref, h_ref, g_ref, w1_ref, w3_ref, w2_ref, out_ref, *, chunks):
    x = h_ref[...] + _dot(o_ref[...], wo_ref[...])
    hn = _rmsnorm_f32(x, g_ref[...]).astype(BF16)
    acc = x
    for c0, c1 in chunks:
        a = _dot(hn, w1_ref[:, c0:c1])
        b = _dot(hn, w3_ref[:, c0:c1])
        acc = acc + _dot((_silu(a) * b).astype(BF16), w2_ref[c0:c1, :])
    out_ref[...] = acc


def oproj_dense_ffn(o, w_o, h, g, w1, w3, w2, tm, chunk):
    t, d = h.shape
    f = w1.shape[1]
    chunks = tuple((c, min(c + chunk, f)) for c in range(0, f, chunk))
    return pl.pallas_call(
        functools.partial(_ffn_kernel, chunks=chunks),
        grid=(t // tm,),
        in_specs=[pl.BlockSpec((tm, d), lambda i: (i, 0)),
                  _whole((d, d)),
                  pl.BlockSpec((tm, d), lambda i: (i, 0)),
                  pl.BlockSpec((1, d), lambda i: (0, 0)),
                  _whole((d, f)), _whole((d, f)), _whole((f, d))],
        out_specs=pl.BlockSpec((tm, d), lambda i: (i, 0)),
        out_shape=jax.ShapeDtypeStruct((t, d), F32),
        compiler_params=_cparams(("arbitrary",), 56),
        name="oproj_dense_ffn",
    )(o, w_o, h, g.reshape(1, d), w1, w3, w2)


def _row_tile_chunks(d):
    assert d % V7X_LANES == 0
    return d // V7X_LANES


def _store_row_tiles(ref, x):
    n, d = x.shape
    c = _row_tile_chunks(d)
    for k in range(c):
        ref[pl.ds(k, n, stride=c), :] = x[:, k * V7X_LANES:(k + 1) * V7X_LANES]


def _load_row_tiles(ref, n, d):
    c = _row_tile_chunks(d)
    return [ref[pl.ds(k, n, stride=c), :] for k in range(c)]


COL_E0, COL_E1, COL_W0, COL_W1, COL_R0, COL_R1 = range(6)


def _router_kernel(o_ref, wo_ref, h_ref, g_ref, wr_ref, h1_ref, hn_ref, slab_ref, cnt_ref, carry_ref, *, tm):
    i = pl.program_id(0)

    @pl.when(i == 0)
    def _():
        carry_ref[...] = jnp.zeros_like(carry_ref)

    h1 = h_ref[...] + _dot(o_ref[...], wo_ref[...])
    h1_ref[...] = h1
    hn = _rmsnorm_f32(h1, g_ref[...])
    _store_row_tiles(hn_ref, hn)
    hi = hn.astype(BF16)
    lo = (hn - hi.astype(F32)).astype(BF16)
    w = wr_ref[...]
    w_hi = w.astype(BF16)
    w_lo = (w - w_hi.astype(F32)).astype(BF16)
    logits = _dot(hi, w_hi) + _dot(hi, w_lo) + _dot(lo, w_hi)

    lane = lax.broadcasted_iota(jnp.int32, (tm, V7X_LANES), 1)
    neg_inf = jnp.float32(-jnp.inf)
    lg = jnp.where(lane < N_EXPERTS, logits, neg_inf)
    m0 = jnp.max(lg, axis=1, keepdims=True)
    i0 = jnp.min(jnp.where(lg == m0, lane, V7X_LANES), axis=1, keepdims=True)
    lg1 = jnp.where(lane == i0, neg_inf, lg)
    m1 = jnp.max(lg1, axis=1, keepdims=True)
    i1 = jnp.min(jnp.where(lg1 == m1, lane, V7X_LANES), axis=1, keepdims=True)
    e = jnp.exp(m1 - m0)
    w0 = 1.0 / (1.0 + e)
    w1 = e / (1.0 + e)

    pick0 = lane == i0
    pick1 = lane == i1
    hot = (pick0 | pick1).astype(BF16)
    row = lax.broadcasted_iota(jnp.int32, (tm, tm), 0)
    col = lax.broadcasted_iota(jnp.int32, (tm, tm), 1)
    before = _dot((col < row).astype(BF16), hot) + carry_ref[0:1, :]
    r0 = jnp.sum(jnp.where(pick0, before, 0.0), axis=1, keepdims=True)
    r1 = jnp.sum(jnp.where(pick1, before, 0.0), axis=1, keepdims=True)
    carry_ref[0:1, :] = carry_ref[0:1, :] + jnp.sum(hot.astype(F32), axis=0, keepdims=True)

    slab = jnp.zeros((tm, V7X_LANES), F32)
    for c, val in ((COL_E0, i0.astype(F32)), (COL_E1, i1.astype(F32)), (COL_W0, w0), (COL_W1, w1),
                   (COL_R0, r0), (COL_R1, r1)):
        slab = jnp.where(lane == c, val, slab)
    slab_ref[...] = slab
    cnt_ref[...] = carry_ref[...]


def oproj_moe_router(o, w_o, h, g, w_router, tm):
    t, d = h.shape
    wr = jnp.zeros((d, V7X_LANES), F32).at[:, :N_EXPERTS].set(w_router)
    return pl.pallas_call(
        functools.partial(_router_kernel, tm=tm),
        grid=(t // tm,),
        in_specs=[pl.BlockSpec((tm, d), lambda i: (i, 0)),
                  _whole((d, d)),
                  pl.BlockSpec((tm, d), lambda i: (i, 0)),
                  pl.BlockSpec((1, d), lambda i: (0, 0)),
                  pl.BlockSpec((d, V7X_LANES), lambda i: (0, 0))],
        out_specs=[pl.BlockSpec((tm, d), lambda i: (i, 0)),
                   pl.BlockSpec((tm * d // V7X_LANES, V7X_LANES), lambda i: (i, 0)),
                   pl.BlockSpec((tm, V7X_LANES), lambda i: (i, 0)),
                   pl.BlockSpec((8, V7X_LANES), lambda i: (0, 0))],
        out_shape=[jax.ShapeDtypeStruct((t, d), F32),
                   jax.ShapeDtypeStruct((t * d // V7X_LANES, V7X_LANES), F32),
                   jax.ShapeDtypeStruct((t, V7X_LANES), F32),
                   jax.ShapeDtypeStruct((8, V7X_LANES), F32)],
        scratch_shapes=[pltpu.VMEM((8, V7X_LANES), F32)],
        compiler_params=_cparams(("arbitrary",), 40),
        name="oproj_moe_router",
    )(o, w_o, h, g.reshape(1, d), wr)


SCALAR_UNROLL = 8


def _sorted_list_kernel(beg_ref, end_ref, d0_ref, d1_ref, out_ref, *, chunk, t, tm, rows):
    i = pl.program_id(0)

    @pl.when(i == 0)
    def _():
        for e in range(N_EXPERTS + 1):
            beg, end = beg_ref[e], end_ref[e]
            n = end - beg

            def fill_group(g, carry, end=end):
                p0 = end - SCALAR_UNROLL * (g + 1)
                r0 = lax.rem(p0, tm)
                for k in range(SCALAR_UNROLL):
                    out_ref[p0 + k] = 2 * t + r0 + k
                return carry
            lax.fori_loop(0, n // SCALAR_UNROLL, fill_group, 0)

            def fill_one(p, carry):
                out_ref[p] = 2 * t + lax.rem(p, tm)
                return carry
            lax.fori_loop(beg, beg + lax.rem(n, SCALAR_UNROLL), fill_one, 0)

    base = i * chunk

    def body(g, carry):
        for k in range(SCALAR_UNROLL):
            j = g * SCALAR_UNROLL + k
            out_ref[d0_ref[0, 0, j]] = base + j
            out_ref[d1_ref[0, 0, j]] = t + base + j
        return carry
    lax.fori_loop(0, chunk // SCALAR_UNROLL, body, 0)


def sorted_assignment_list(dest0, dest1, pad_begin, pad_end, t, tm, rows):
    chunk = min(t, 2048)
    assert t % chunk == 0 and chunk % SCALAR_UNROLL == 0 and tm % SCALAR_UNROLL == 0
    blocked = lambda a: a.reshape(t // chunk, 1, chunk)
    dest_spec = pl.BlockSpec((1, 1, chunk), lambda i, beg, end: (i, 0, 0), memory_space=pltpu.SMEM)
    grid_spec = pltpu.PrefetchScalarGridSpec(
        num_scalar_prefetch=2,
        grid=(t // chunk,),
        in_specs=[dest_spec, dest_spec],
        out_specs=pl.BlockSpec(memory_space=pltpu.SMEM),
    )
    return pl.pallas_call(
        functools.partial(_sorted_list_kernel, chunk=chunk, t=t, tm=tm, rows=rows),
        grid_spec=grid_spec,
        out_shape=jax.ShapeDtypeStruct((rows,), jnp.int32),
        compiler_params=_cparams(("arbitrary",), 16),
        name="sorted_assignment_list",
    )(pad_begin, pad_end, blocked(dest0), blocked(dest1))


def _moe_kernel(te_ref, nu_ref, tokc_ref, tokn_ref, dst_ref, hn_hbm, w1_ref, w3_ref, w2_ref, y_hbm,
                xbuf, xb16, acc_ref, ybuf, gsem, ssem, *, tm, nf, nt, n_real):
    i = pl.program_id(0)
    j = pl.program_id(1)
    nu = nu_ref[0]
    slot = i % 2
    d = acc_ref.shape[1]
    c = _row_tile_chunks(d)

    def row_loop(start_row_copy):
        for r in range(tm):
            start_row_copy(r)

    def row(r):
        return pl.ds(r * c, c) if isinstance(r, int) else pl.ds(pl.multiple_of(r * c, c), c)

    def start_gather(tok_ref, s):
        row_loop(lambda r: pltpu.make_async_copy(
            hn_hbm.at[row(tok_ref[0, 0, r]), :], xbuf.at[s, row(r), :], gsem.at[s]).start())

    def wait_gather(s):
        pltpu.make_async_copy(hn_hbm.at[pl.ds(0, tm * c), :], xbuf.at[s], gsem.at[s]).wait()

    def start_scatter():
        row_loop(lambda r: pltpu.make_async_copy(
            ybuf.at[row(r), :], y_hbm.at[row(dst_ref[0, 0, r]), :], ssem.at[0]).start())

    def spare_rows_copy():
        return pltpu.make_async_copy(ybuf, y_hbm.at[pl.ds(n_real * c, tm * c), :], ssem.at[0])

    def wait_scatter():
        spare_rows_copy().wait()

    @pl.when((j == 0) & (i == 0))
    def _():
        ybuf[...] = jnp.zeros_like(ybuf)
        spare_rows_copy().start()
        spare_rows_copy().wait()

        @pl.when(nu > 0)
        def _():
            start_gather(tokc_ref, 0)

    @pl.when((j == 0) & (i < nu))
    def _():
        wait_gather(slot)
        for k, chunk in enumerate(_load_row_tiles(xbuf.at[slot], tm, d)):
            xb16[:, k * V7X_LANES:(k + 1) * V7X_LANES] = chunk.astype(BF16)

    @pl.when((j == 0) & (i + 1 < nu))
    def _():
        start_gather(tokn_ref, 1 - slot)

    @pl.when(i < nu)
    def _():
        x = xb16[...]
        a = _dot(x, w1_ref[0])
        b = _dot(x, w3_ref[0])
        part = _dot((_silu(a) * b).astype(BF16), w2_ref[0])

        @pl.when(j == 0)
        def _():
            acc_ref[...] = part

        @pl.when(j > 0)
        def _():
            acc_ref[...] += part

    last = j == nf - 1

    @pl.when(last & (i >= 1) & (i - 1 < nu))
    def _():
        wait_scatter()

    @pl.when(last & (i < nu))
    def _():
        _store_row_tiles(ybuf, acc_ref[...])
        start_scatter()

    @pl.when(last & (i == nt - 1) & (i < nu))
    def _():
        wait_scatter()


def moe_experts(hn, slab, counts, w1, w3, w2, tm, tf):
    t = slab.shape[0]
    d = w1.shape[1]
    c = _row_tile_chunks(d)
    f = w1.shape[2]
    nf = f // tf
    nt = (2 * t) // tm + N_EXPERTS
    rows = nt * tm

    e0 = slab[:, COL_E0].astype(jnp.int32)
    e1 = slab[:, COL_E1].astype(jnp.int32)
    r0 = slab[:, COL_R0].astype(jnp.int32)
    r1 = slab[:, COL_R1].astype(jnp.int32)
    cnt = counts[0, :N_EXPERTS].astype(jnp.int32)
    tiles = (cnt + tm - 1) // tm
    tile_end = jnp.cumsum(tiles)
    tile_start = tile_end - tiles
    nu = tile_end[-1:]
    offs = tile_start * tm
    dest0 = offs[e0] + r0
    dest1 = offs[e1] + r1
    tile_expert = jnp.minimum(
        jnp.sum(jnp.arange(nt, dtype=jnp.int32)[:, None] >= tile_end[None, :], axis=1), N_EXPERTS - 1
    ).astype(jnp.int32)
    pad_begin = jnp.concatenate([offs + cnt, nu * tm]).astype(jnp.int32)
    pad_end = jnp.concatenate([tile_end * tm, jnp.full((1,), rows, jnp.int32)]).astype(jnp.int32)
    dst_sorted = sorted_assignment_list(dest0, dest1, pad_begin, pad_end, t, tm, rows)
    tok_sorted = jnp.where(dst_sorted >= 2 * t, 0, jnp.where(dst_sorted >= t, dst_sorted - t, dst_sorted))
    tok3 = tok_sorted.reshape(nt, 1, tm)
    dst3 = dst_sorted.reshape(nt, 1, tm)

    def w_in(shape, which):
        def index_map(i, j, te, nu_):
            ii = jnp.minimum(i, nu_[0] - 1)
            jj = jnp.where(i < nu_[0], j, nf - 1)
            return (te[ii], 0, jj) if which == "up" else (te[ii], jj, 0)
        return pl.BlockSpec(shape, index_map)

    smem_tile = lambda f_: pl.BlockSpec((1, 1, tm), f_, memory_space=pltpu.SMEM)
    grid_spec = pltpu.PrefetchScalarGridSpec(
        num_scalar_prefetch=2,
        grid=(nt, nf),
        in_specs=[smem_tile(lambda i, j, te, nu_: (i, 0, 0)),
                  smem_tile(lambda i, j, te, nu_: (jnp.minimum(i + 1, nt - 1), 0, 0)),
                  smem_tile(lambda i, j, te, nu_: (i, 0, 0)),
                  pl.BlockSpec(memory_space=pl.ANY),
                  w_in((1, d, tf), "up"), w_in((1, d, tf), "up"), w_in((1, tf, d), "down")],
        out_specs=pl.BlockSpec(memory_space=pl.ANY),
        scratch_shapes=[pltpu.VMEM((2, tm * c, V7X_LANES), F32),
                        pltpu.VMEM((tm, d), BF16),
                        pltpu.VMEM((tm, d), F32),
                        pltpu.VMEM((tm * c, V7X_LANES), F32),
                        pltpu.SemaphoreType.DMA((2,)),
                        pltpu.SemaphoreType.DMA((1,))],
    )
    return pl.pallas_call(
        functools.partial(_moe_kernel, tm=tm, nf=nf, nt=nt, n_real=2 * t),
        grid_spec=grid_spec,
        out_shape=jax.ShapeDtypeStruct(((2 * t + tm) * c, V7X_LANES), F32),
        compiler_params=_cparams(("arbitrary", "arbitrary"), 56),
        name="moe_experts",
    )(tile_expert, nu, tok3, tok3, dst3, hn, w1, w3, w2)


def _combine_kernel(h_ref, y0_ref, y1_ref, slab_ref, g_ref, out_ref, *, final):
    slab = slab_ref[...]
    tm, d = h_ref.shape
    y0 = jnp.concatenate(_load_row_tiles(y0_ref, tm, d), axis=1)
    y1 = jnp.concatenate(_load_row_tiles(y1_ref, tm, d), axis=1)
    x = h_ref[...] + slab[:, COL_W0:COL_W0 + 1] * y0 + slab[:, COL_W1:COL_W1 + 1] * y1
    out_ref[...] = _rmsnorm_f32(x, g_ref[...]) if final else x


def moe_combine(h, y, slab, g, tm, final):
    t, d = h.shape
    nb = t // tm
    c = _row_tile_chunks(d)
    return pl.pallas_call(
        functools.partial(_combine_kernel, final=final),
        grid=(nb,),
        in_specs=[pl.BlockSpec((tm, d), lambda i: (i, 0)),
                  pl.BlockSpec((tm * c, V7X_LANES), lambda i: (i, 0)),
                  pl.BlockSpec((tm * c, V7X_LANES), lambda i: (i + nb, 0)),
                  pl.BlockSpec((tm, V7X_LANES), lambda i: (i, 0)),
                  pl.BlockSpec((1, d), lambda i: (0, 0))],
        out_specs=pl.BlockSpec((tm, d), lambda i: (i, 0)),
        out_shape=jax.ShapeDtypeStruct((t, d), F32),
        compiler_params=_cparams(("arbitrary",), 40),
        name="moe_combine",
    )(h, y, y, slab, g.reshape(1, d))


def _norm_kernel(h_ref, g_ref, out_ref):
    out_ref[...] = _rmsnorm_f32(h_ref[...], g_ref[...])


def final_norm_only(h, g, tm):
    t, d = h.shape
    return pl.pallas_call(
        _norm_kernel,
        grid=(t // tm,),
        in_specs=[pl.BlockSpec((tm, d), lambda i: (i, 0)), pl.BlockSpec((1, d), lambda i: (0, 0))],
        out_specs=pl.BlockSpec((tm, d), lambda i: (i, 0)),
        out_shape=jax.ShapeDtypeStruct((t, d), F32),
        compiler_params=_cparams(("arbitrary",), 40),
        name="final_norm",
    )(h, g.reshape(1, d))


def _row_tile(t, want):
    tm = min(want, t)
    assert t % tm == 0
    return tm


def kernel(x, w_qkv, w_o, mixer_norm, ffn_norm, rel_bias, w1, w3, w2, router, e_w1, e_w3, e_w2, final_norm):
    b, s, d = x.shape
    assert d == N_HEADS * HEAD_DIM and s % MOBA_BLOCK == 0
    t = b * s
    depth = w_qkv.shape[0]
    h = x.reshape(t, d)
    tm_big = _row_tile(t, 1024)
    tm_mid = _row_tile(t, 512)
    normed = False
    for i in range(depth):
        qkv = qkv_proj(h, mixer_norm[i], w_qkv[i].astype(BF16), tm_big)
        if i % 2 == 0:
            o = stick_breaking_attention(qkv, b, s, MOBA_BLOCK)
        else:
            o = moba_attention(qkv, rel_bias, b, s)
        o = o.reshape(t, d)
        wo = w_o[i].astype(BF16)
        jj = i // 2
        if i % 2 == 0:
            h = oproj_dense_ffn(o, wo, h, ffn_norm[i], w1[jj].astype(BF16), w3[jj].astype(BF16),
                                w2[jj].astype(BF16), tm_mid, 1024)
        else:
            h, hn, slab, counts = oproj_moe_router(o, wo, h, ffn_norm[i], router[jj], tm_mid)
            f_e = e_w1.shape[-1]
            tf = f_e // 2 if (f_e // 2) % 256 == 0 else f_e
            y = moe_experts(hn, slab, counts, e_w1[jj].astype(BF16), e_w3[jj].astype(BF16),
                            e_w2[jj].astype(BF16), tm_mid, tf)
            last = i == depth - 1
            h = moe_combine(h, y, slab, final_norm if last else ffn_norm[i], tm_mid, last)
            normed = last
    if not normed:
        h = final_norm_only(h, final_norm, tm_mid)
    return h.reshape(b, s, d)
```

```python
import functools
import math

import jax
import jax.numpy as jnp
from jax import lax
from jax.experimental import pallas as pl
from jax.experimental.pallas import tpu as pltpu

N_HEADS = 16
HEAD_DIM = 64
MOBA_BLOCK = 256
MOBA_TOPK = 3
N_BUCKETS = 32
MAX_DISTANCE = 128
N_EXPERTS = 8
RMS_EPS = 1e-6
NEG_BIG = -1e30

V7X_LANES = 128
ROW_DMA_UNROLL = 8
SB_GROUP = 4
MOBA_GROUP = 4
V7X_VMEM_BYTES = 64 * 1024 * 1024
HEADS_PER_LANE_BLOCK = V7X_LANES // HEAD_DIM

F32 = jnp.float32
BF16 = jnp.bfloat16


def _cparams(semantics, vmem_mb):
    assert vmem_mb * 1024 * 1024 < V7X_VMEM_BYTES
    return pltpu.CompilerParams(dimension_semantics=semantics,
                                vmem_limit_bytes=vmem_mb * 1024 * 1024)


def _rmsnorm_f32(x, g):
    return x * lax.rsqrt(jnp.mean(x * x, axis=-1, keepdims=True) + RMS_EPS) * g


def _silu(a):
    return a * (1.0 / (1.0 + jnp.exp(-a)))


def _dot(a, b):
    return jnp.dot(a, b, preferred_element_type=F32)


def _dot_nt(a, b):
    return lax.dot_general(a, b, (((1,), (1,)), ((), ())), preferred_element_type=F32)


def _qkv_kernel(x_ref, g_ref, w_ref, o_ref, hn_ref):
    j = pl.program_id(1)

    @pl.when(j == 0)
    def _():
        hn_ref[...] = _rmsnorm_f32(x_ref[...], g_ref[...]).astype(BF16)

    scale = jnp.where(j == 0, HEAD_DIM ** -0.5, 1.0)
    o_ref[0] = (_dot(hn_ref[...], w_ref[...]) * scale).astype(o_ref.dtype)


def qkv_proj(h, g, w_bf16, tm):
    t, d = h.shape
    return pl.pallas_call(
        _qkv_kernel,
        grid=(t // tm, 3),
        in_specs=[pl.BlockSpec((tm, d), lambda i, j: (i, 0)),
                  pl.BlockSpec((1, d), lambda i, j: (0, 0)),
                  pl.BlockSpec((d, d), lambda i, j: (0, j))],
        out_specs=pl.BlockSpec((1, tm, d), lambda i, j: (j, i, 0)),
        out_shape=jax.ShapeDtypeStruct((3, t, d), BF16),
        scratch_shapes=[pltpu.VMEM((tm, d), BF16)],
        compiler_params=_cparams(("arbitrary", "arbitrary"), 40),
        name="qkv_proj",
    )(h, g.reshape(1, d), w_bf16)


def _sb_kernel(q_ref, k_ref, v_ref, o_ref, acc0_ref, acc1_ref, r0_ref, r1_ref, u_ref, *, tq):
    row = lax.broadcasted_iota(jnp.int32, (tq, tq), 0)
    col = lax.broadcasted_iota(jnp.int32, (tq, tq), 1)
    u_ref[...] = -(row > col).astype(BF16)

    def q_tile(qi, carry):
        _sb_tile(qi, q_ref, k_ref, v_ref, o_ref, acc0_ref, acc1_ref, r0_ref, r1_ref, u_ref, tq=tq)
        return carry

    lax.fori_loop(0, q_ref.shape[2] // tq, q_tile, 0)


def _sb_tile(qi, q_ref, k_ref, v_ref, o_ref, acc0_ref, acc1_ref, r0_ref, r1_ref, u_ref, *, tq):
    acc_ref = (acc0_ref, acc1_ref)
    r_ref = (r0_ref, r1_ref)
    row = lax.broadcasted_iota(jnp.int32, (tq, tq), 0)
    col = lax.broadcasted_iota(jnp.int32, (tq, tq), 1)
    tile_rows = pl.ds(pl.multiple_of(qi * tq, tq), tq)
    q = q_ref[0, 0, tile_rows, :]
    lane = lax.broadcasted_iota(jnp.int32, (tq, V7X_LANES), 1)
    past = col < row
    heads = range(HEADS_PER_LANE_BLOCK)
    qms = [jnp.where(lane // HEAD_DIM == h, q, jnp.zeros_like(q)) for h in heads]

    def span(first, nb, diagonal):
        tasks = [(i, h) for i in reversed(range(nb)) for h in heads]
        state = [dict() for _ in tasks]
        r = [r_ref[h][...] for h in heads]
        acc = [None for h in heads]

        def stage(k, st, i, h):
            on_diagonal = diagonal and i == nb - 1
            start = pl.multiple_of((first + i) * tq, tq)
            if k == 0:
                st["z"] = _dot_nt(qms[h], k_ref[0, 0, pl.ds(start, tq), :])
            elif k == 1:
                z = st.pop("z")
                zb = z.astype(BF16)
                t = jnp.log(1 + jnp.exp(-jnp.abs(zb)))
                sp = jnp.maximum(zb, 0) + t
                if on_diagonal:
                    sp = jnp.where(past, sp, jnp.zeros_like(sp))
                st["sp"] = sp
                st["log_sig"] = jnp.minimum(z, 0.0) - t.astype(F32)
            elif k == 2:
                st["after"] = _dot(st["sp"], u_ref[...])
            elif k == 3:
                tot = st.pop("after") + r[h]
                w = jnp.exp(st.pop("log_sig") + tot)
                if on_diagonal:
                    w = jnp.where(past, w, 0.0)
                r[h] = tot[:, 0:1] - st.pop("sp")[:, 0:1].astype(F32)
                st["w"] = w.astype(BF16)
            else:
                part = _dot(st.pop("w"), v_ref[0, 0, pl.ds(start, tq), :])
                acc[h] = part if acc[h] is None else acc[h] + part

        n_stages = 5
        for step in range(len(tasks) + n_stages - 1):
            for k in reversed(range(n_stages)):
                ti = step - k
                if 0 <= ti < len(tasks):
                    stage(k, state[ti], *tasks[ti])
        for h in heads:
            r_ref[h][...] = r[h]
            acc_ref[h][...] += acc[h]

    for h in heads:
        r_ref[h][...] = jnp.zeros((tq, 1), F32)
        acc_ref[h][...] = jnp.zeros((tq, V7X_LANES), F32)

    n_groups = qi // SB_GROUP
    for in_group in range(SB_GROUP):
        @pl.when(qi % SB_GROUP == in_group)
        def _(in_group=in_group):
            span(n_groups * SB_GROUP, in_group + 1, True)

    def body(s, carry):
        span((n_groups - 1 - s) * SB_GROUP, SB_GROUP, False)
        return carry

    lax.fori_loop(0, n_groups, body, 0)

    o_ref[0, tile_rows, :] = jnp.where(lane // HEAD_DIM == 0, acc0_ref[...], acc1_ref[...]).astype(o_ref.dtype)


def stick_breaking_attention(qkv, b, s, tq):
    d = qkv.shape[-1]
    qkv4 = qkv.reshape(3, b, s, d)
    plane = lambda p: pl.BlockSpec((1, 1, s, V7X_LANES), lambda bi, hp: (p, bi, 0, hp))
    return pl.pallas_call(
        functools.partial(_sb_kernel, tq=tq),
        grid=(b, d // V7X_LANES),
        in_specs=[plane(0), plane(1), plane(2)],
        out_specs=pl.BlockSpec((1, s, V7X_LANES), lambda bi, hp: (bi, 0, hp)),
        out_shape=jax.ShapeDtypeStruct((b, s, d), BF16),
        scratch_shapes=[pltpu.VMEM((tq, V7X_LANES), F32), pltpu.VMEM((tq, V7X_LANES), F32),
                        pltpu.VMEM((tq, 1), F32), pltpu.VMEM((tq, 1), F32),
                        pltpu.VMEM((tq, tq), BF16)],
        compiler_params=_cparams(("arbitrary", "arbitrary"), 32),
        name="stick_breaking",
    )(qkv4, qkv4, qkv4)


def _t5_bias_kernel(rb_ref, o_ref, *, tq):
    h = pl.program_id(0)
    row = lax.broadcasted_iota(jnp.int32, (tq, tq), 0)
    col = lax.broadcasted_iota(jnp.int32, (tq, tq), 1)
    max_exact = N_BUCKETS // 2
    for o in range(2):
        dist = o * tq + row - col
        n = jnp.maximum(dist, 0)
        nf = jnp.maximum(n, 1).astype(F32)
        large = max_exact + (jnp.log(nf / max_exact) / math.log(MAX_DISTANCE / max_exact)
                             * (N_BUCKETS - max_exact)).astype(jnp.int32)
        large = jnp.minimum(large, N_BUCKETS - 1)
        bucket = jnp.where(n < max_exact, n, large)
        bias = jnp.zeros((tq, tq), F32)
        for bkt in range(N_BUCKETS):
            bias = jnp.where(bucket == bkt, rb_ref[bkt, h], bias)
        if o == 0:
            bias = jnp.where(dist >= 0, bias, NEG_BIG)
        o_ref[0, o] = bias


def t5_bias_tiles(rel_bias, tq):
    return pl.pallas_call(
        functools.partial(_t5_bias_kernel, tq=tq),
        grid=(N_HEADS,),
        in_specs=[pl.BlockSpec(memory_space=pltpu.SMEM)],
        out_specs=pl.BlockSpec((1, 2, tq, tq), lambda h: (h, 0, 0, 0)),
        out_shape=jax.ShapeDtypeStruct((N_HEADS, 2, tq, tq), F32),
        compiler_params=_cparams(("arbitrary",), 16),
        name="t5_bias_tiles",
    )(rel_bias)


def _moba_aux_lane0(h):
    assert HEADS_PER_LANE_BLOCK == 2
    return ((h + 1) % HEADS_PER_LANE_BLOCK) * HEAD_DIM


def _moba_kernel(rb_ref, qall_ref, k_ref, v_ref, bt_ref, o_ref,
                 s_ref, qaux_ref, mx_ref, l_ref, acc_ref, *, tq, nblk):
    heads = range(HEADS_PER_LANE_BLOCK)
    s_len = nblk * tq
    n_sel = min(MOBA_TOPK, nblk - 1)
    big = jnp.asarray(-NEG_BIG, BF16).astype(F32)
    aux_lane0 = _moba_aux_lane0
    assert nblk + 2 <= HEAD_DIM

    def block_choice():
        km = jnp.concatenate(
            [jnp.mean(k_ref[0, 0, n * tq:(n + 1) * tq, :].astype(F32), axis=0, keepdims=True)
             for n in range(nblk)], axis=0)
        km_hi = km.astype(BF16)
        km_lo = (km - km_hi.astype(F32)).astype(BF16)
        q_all = qall_ref[0, 0]
        lane_all = lax.broadcasted_iota(jnp.int32, (s_len, V7X_LANES), 1)
        blk = lax.broadcasted_iota(jnp.int32, (nblk, s_len), 0)
        own = lax.broadcasted_iota(jnp.int32, (nblk, s_len), 1) // tq
        own_row = own[0:1, :]
        for h in heads:
            qm = jnp.where(lane_all // HEAD_DIM == h, q_all, jnp.zeros_like(q_all))
            gate = _dot_nt(km_hi, qm) + _dot_nt(km_lo, qm)
            gate = jnp.where(blk < own, gate, NEG_BIG)
            rows = [gate[n:n + 1, :] for n in range(nblk)]
            rank = [jnp.zeros((1, s_len), F32) for _ in range(nblk)]
            for n in range(nblk):
                for m in range(n):
                    m_wins = jnp.where(rows[m] >= rows[n], 1.0, 0.0)
                    rank[n] = rank[n] + m_wins
                    rank[m] = rank[m] + (1.0 - m_wins)
            aux = [jnp.where((rank[n] < n_sel) & (n < own_row), 0.0, -big) for n in range(nblk)]
            aux += [jnp.ones((1, s_len), F32)] * 2
            before = jnp.zeros((aux_lane0(h), s_len), F32)
            after = jnp.zeros((V7X_LANES - aux_lane0(h) - len(aux), s_len), F32)
            pieces = ([before] if before.shape[0] else []) + aux + [after]
            qaux_ref[h] = jnp.concatenate(pieces, axis=0).T.astype(BF16)

    block_choice()

    def q_tile(qi, carry):
        _moba_tile(qi, rb_ref, qall_ref, k_ref, v_ref, bt_ref, o_ref,
                   s_ref, qaux_ref, mx_ref, l_ref, acc_ref, tq=tq, nblk=nblk)
        return carry

    lax.fori_loop(0, nblk, q_tile, 0)


def _moba_tile(qi, rb_ref, qall_ref, k_ref, v_ref, bt_ref, o_ref,
               s_ref, qaux_ref, mx_ref, l_ref, acc_ref, *, tq, nblk):
    hp = pl.program_id(1)
    heads = range(HEADS_PER_LANE_BLOCK)
    aux_lane0 = _moba_aux_lane0

    tile_rows = pl.ds(pl.multiple_of(qi * tq, tq), tq)
    q = qall_ref[0, 0, tile_rows, :]
    lane = lax.broadcasted_iota(jnp.int32, (tq, V7X_LANES), 1)
    q_aug = [jnp.where(lane // HEAD_DIM == h, q, qaux_ref[h, tile_rows, :]) for h in heads]
    both = lambda x: jnp.concatenate([x, x], axis=1)
    halves = lambda x: (x[:, :V7X_LANES], x[:, V7X_LANES:])
    for h in heads:
        mx_ref[h] = jnp.full((tq, V7X_LANES), NEG_BIG, F32)

    aux_rows = 16
    lane_aux = lax.broadcasted_iota(jnp.int32, (aux_rows, V7X_LANES), 1)
    key_aux_far = []
    for h in heads:
        far = jnp.full((aux_rows, V7X_LANES), rb_ref[N_BUCKETS - 1, hp * HEADS_PER_LANE_BLOCK + h], F32)
        far_hi = far.astype(BF16).astype(F32)
        far_lo = (far - far_hi).astype(BF16).astype(F32)
        key_aux_far.append(jnp.where(lane_aux == aux_lane0(h) + nblk, far_hi,
                                     jnp.where(lane_aux == aux_lane0(h) + nblk + 1, far_lo, 0.0)))

    def key_operand(h, n, role):
        start = pl.multiple_of(n * tq, tq)
        kb = k_ref[0, 0, pl.ds(start, tq), :]
        if role == "own":
            aux = jnp.zeros((aux_rows, V7X_LANES), F32)
        else:
            base = key_aux_far[h] if role == "far" else jnp.zeros((aux_rows, V7X_LANES), F32)
            aux = jnp.where(lane_aux == aux_lane0(h) + n, 1.0, base)
        aux = jnp.concatenate([aux.astype(BF16)] * (tq // aux_rows), axis=0)
        return jnp.where(lane // HEAD_DIM == h, kb, aux)

    def pipeline(tasks, stages):
        state = [dict() for _ in tasks]
        for step in range(len(tasks) + len(stages) - 1):
            for k in reversed(range(len(stages))):
                t = step - k
                if 0 <= t < len(tasks):
                    stages[k](state[t], *tasks[t])

    def logits_span(first, roles):
        mx = [mx_ref[h] for h in heads]

        def products(st, i, h):
            st["s"] = _dot_nt(q_aug[h], key_operand(h, first + i, roles[i]))

        def finish(st, i, h):
            start = pl.multiple_of((first + i) * tq, tq)
            s = st.pop("s")
            if roles[i] == "own":
                s = s + bt_ref[h, 0]
            elif roles[i] == "prev":
                s = s + bt_ref[h, 1]
            s_ref[h, :, pl.ds(start, tq)] = s
            s_lo, s_hi = halves(s)
            mx[h] = jnp.maximum(mx[h], jnp.maximum(s_lo, s_hi))

        pipeline([(i, h) for i in range(len(roles)) for h in heads], [products, finish])
        for h in heads:
            mx_ref[h] = mx[h]

    def probs_span(first, nb):
        m = [both(mx_ref[h]) for h in heads]
        l = [l_ref[h] for h in heads]
        acc = [acc_ref[h] for h in heads]

        def exponentials(st, i, h):
            start = pl.multiple_of((first + i) * tq, tq)
            p = jnp.exp(s_ref[h, :, pl.ds(start, tq)] - m[h])
            p_lo, p_hi = halves(p)
            l[h] = l[h] + (p_lo + p_hi)
            st["p"] = p.astype(BF16)

        def accumulate(st, i, h):
            start = pl.multiple_of((first + i) * tq, tq)
            acc[h] = acc[h] + _dot(st.pop("p"), v_ref[0, 0, pl.ds(start, tq), :])

        pipeline([(i, h) for i in range(nb) for h in heads], [exponentials, accumulate])
        for h in heads:
            l_ref[h] = l[h]
            acc_ref[h] = acc[h]

    group = MOBA_GROUP
    own_group = qi // group
    in_group = qi % group

    def far_groups(g, carry):
        logits_span(g * group, ["far"] * group)
        return carry

    lax.fori_loop(0, own_group - 1, far_groups, 0)

    @pl.when((own_group >= 1) & (in_group == 0))
    def _():
        logits_span((own_group - 1) * group, ["far"] * (group - 1) + ["prev"])

    @pl.when((own_group >= 1) & (in_group != 0))
    def _():
        logits_span((own_group - 1) * group, ["far"] * group)

    for size in range(1, group + 1):
        @pl.when(in_group == size - 1)
        def _(size=size):
            logits_span(own_group * group, (["far"] * group + ["prev", "own"])[-size:]
                        if size >= 2 else ["own"])

    for h in heads:
        mx_ref[h] = jnp.broadcast_to(jnp.max(mx_ref[h], axis=1, keepdims=True), (tq, V7X_LANES))
        l_ref[h] = jnp.zeros((tq, V7X_LANES), F32)
        acc_ref[h] = jnp.zeros((tq, V7X_LANES), F32)

    def prob_groups(g, carry):
        probs_span(g * group, group)
        return carry

    lax.fori_loop(0, own_group, prob_groups, 0)
    for size in range(1, group + 1):
        @pl.when(in_group == size - 1)
        def _(size=size):
            probs_span(own_group * group, size)


    out = [acc_ref[h] / jnp.sum(l_ref[h], axis=1, keepdims=True) for h in heads]
    o_ref[0, tile_rows, :] = jnp.where(lane // HEAD_DIM == 0, out[0], out[1]).astype(o_ref.dtype)


def moba_attention(qkv, rel_bias, b, s):
    d = qkv.shape[-1]
    tq = MOBA_BLOCK
    assert s % tq == 0 and tq >= 2 * MAX_DISTANCE
    nblk = s // tq
    qkv4 = qkv.reshape(3, b, s, d)
    tiles = t5_bias_tiles(rel_bias, tq)
    return pl.pallas_call(
        functools.partial(_moba_kernel, tq=tq, nblk=nblk),
        grid=(b, d // V7X_LANES),
        in_specs=[pl.BlockSpec(memory_space=pltpu.SMEM),
                  pl.BlockSpec((1, 1, s, V7X_LANES), lambda bi, hp: (0, bi, 0, hp)),
                  pl.BlockSpec((1, 1, s, V7X_LANES), lambda bi, hp: (1, bi, 0, hp)),
                  pl.BlockSpec((1, 1, s, V7X_LANES), lambda bi, hp: (2, bi, 0, hp)),
                  pl.BlockSpec((HEADS_PER_LANE_BLOCK, 2, tq, tq), lambda bi, hp: (hp, 0, 0, 0))],
        out_specs=pl.BlockSpec((1, s, V7X_LANES), lambda bi, hp: (bi, 0, hp)),
        out_shape=jax.ShapeDtypeStruct((b, s, d), BF16),
        scratch_shapes=[pltpu.VMEM((HEADS_PER_LANE_BLOCK, tq, s), F32),
                        pltpu.VMEM((HEADS_PER_LANE_BLOCK, s, V7X_LANES), BF16),
                        pltpu.VMEM((HEADS_PER_LANE_BLOCK, tq, V7X_LANES), F32),
                        pltpu.VMEM((HEADS_PER_LANE_BLOCK, tq, V7X_LANES), F32),
                        pltpu.VMEM((HEADS_PER_LANE_BLOCK, tq, V7X_LANES), F32)],
        compiler_params=_cparams(("arbitrary", "arbitrary"), 32),
        name="moba",
    )(rel_bias, qkv4, qkv4, qkv4, tiles)


def _whole(shape):
    return pl.BlockSpec(shape, lambda i: (0,) * len(shape), pipeline_mode=pl.Buffered(1))


def _ffn_kernel(o_ref, wo_ref, h_ref, g_ref, w1_ref, w3_ref, w2_ref, out_ref, *, chunks):
    x = h_ref[...] + _dot(o_ref[...], wo_ref[...])
    hn = _rmsnorm_f32(x, g_ref[...]).astype(BF16)
    acc = x
    for c0, c1 in chunks:
        a = _dot(hn, w1_ref[:, c0:c1])
        b = _dot(hn, w3_ref[:, c0:c1])
        acc = acc + _dot((_silu(a) * b).astype(BF16), w2_ref[c0:c1, :])
    out_ref[...] = acc


def oproj_dense_ffn(o, w_o, h, g, w1, w3, w2, tm, chunk):
    t, d = h.shape
    f = w1.shape[1]
    chunks = tuple((c, min(c + chunk, f)) for c in range(0, f, chunk))
    return pl.pallas_call(
        functools.partial(_ffn_kernel, chunks=chunks),
        grid=(t // tm,),
        in_specs=[pl.BlockSpec((tm, d), lambda i: (i, 0)),
                  _whole((d, d)),
                  pl.BlockSpec((tm, d), lambda i: (i, 0)),
                  pl.BlockSpec((1, d), lambda i: (0, 0)),
                  _whole((d, f)), _whole((d, f)), _whole((f, d))],
        out_specs=pl.BlockSpec((tm, d), lambda i: (i, 0)),
        out_shape=jax.ShapeDtypeStruct((t, d), F32),
        compiler_params=_cparams(("arbitrary",), 56),
        name="oproj_dense_ffn",
    )(o, w_o, h, g.reshape(1, d), w1, w3, w2)


def _row_tile_chunks(d):
    assert d % V7X_LANES == 0
    return d // V7X_LANES


def _store_row_tiles(ref, x):
    n, d = x.shape
    c = _row_tile_chunks(d)
    for k in range(c):
        ref[pl.ds(k, n, stride=c), :] = x[:, k * V7X_LANES:(k + 1) * V7X_LANES]


def _load_row_tiles(ref, n, d):
    c = _row_tile_chunks(d)
    return [ref[pl.ds(k, n, stride=c), :] for k in range(c)]


COL_E0, COL_E1, COL_W0, COL_W1, COL_R0, COL_R1 = range(6)


def _router_kernel(o_ref, wo_ref, h_ref, g_ref, wr_ref, h1_ref, hn_ref, slab_ref, cnt_ref, carry_ref, *, tm):
    i = pl.program_id(0)

    @pl.when(i == 0)
    def _():
        carry_ref[...] = jnp.zeros_like(carry_ref)

    h1 = h_ref[...] + _dot(o_ref[...], wo_ref[...])
    h1_ref[...] = h1
    hn = _rmsnorm_f32(h1, g_ref[...])
    _store_row_tiles(hn_ref, hn)
    hi = hn.astype(BF16)
    lo = (hn - hi.astype(F32)).astype(BF16)
    w = wr_ref[...]
    w_hi = w.astype(BF16)
    w_lo = (w - w_hi.astype(F32)).astype(BF16)
    logits = _dot(hi, w_hi) + _dot(hi, w_lo) + _dot(lo, w_hi)

    lane = lax.broadcasted_iota(jnp.int32, (tm, V7X_LANES), 1)
    neg_inf = jnp.float32(-jnp.inf)
    lg = jnp.where(lane < N_EXPERTS, logits, neg_inf)
    m0 = jnp.max(lg, axis=1, keepdims=True)
    i0 = jnp.min(jnp.where(lg == m0, lane, V7X_LANES), axis=1, keepdims=True)
    lg1 = jnp.where(lane == i0, neg_inf, lg)
    m1 = jnp.max(lg1, axis=1, keepdims=True)
    i1 = jnp.min(jnp.where(lg1 == m1, lane, V7X_LANES), axis=1, keepdims=True)
    e = jnp.exp(m1 - m0)
    w0 = 1.0 / (1.0 + e)
    w1 = e / (1.0 + e)

    pick0 = lane == i0
    pick1 = lane == i1
    hot = (pick0 | pick1).astype(BF16)
    row = lax.broadcasted_iota(jnp.int32, (tm, tm), 0)
    col = lax.broadcasted_iota(jnp.int32, (tm, tm), 1)
    before = _dot((col < row).astype(BF16), hot) + carry_ref[0:1, :]
    r0 = jnp.sum(jnp.where(pick0, before, 0.0), axis=1, keepdims=True)
    r1 = jnp.sum(jnp.where(pick1, before, 0.0), axis=1, keepdims=True)
    carry_ref[0:1, :] = carry_ref[0:1, :] + jnp.sum(hot.astype(F32), axis=0, keepdims=True)

    slab = jnp.zeros((tm, V7X_LANES), F32)
    for c, val in ((COL_E0, i0.astype(F32)), (COL_E1, i1.astype(F32)), (COL_W0, w0), (COL_W1, w1),
                   (COL_R0, r0), (COL_R1, r1)):
        slab = jnp.where(lane == c, val, slab)
    slab_ref[...] = slab
    cnt_ref[...] = carry_ref[...]


def oproj_moe_router(o, w_o, h, g, w_router, tm):
    t, d = h.shape
    wr = jnp.zeros((d, V7X_LANES), F32).at[:, :N_EXPERTS].set(w_router)
    return pl.pallas_call(
        functools.partial(_router_kernel, tm=tm),
        grid=(t // tm,),
        in_specs=[pl.BlockSpec((tm, d), lambda i: (i, 0)),
                  _whole((d, d)),
                  pl.BlockSpec((tm, d), lambda i: (i, 0)),
                  pl.BlockSpec((1, d), lambda i: (0, 0)),
                  pl.BlockSpec((d, V7X_LANES), lambda i: (0, 0))],
        out_specs=[pl.BlockSpec((tm, d), lambda i: (i, 0)),
                   pl.BlockSpec((tm * d // V7X_LANES, V7X_LANES), lambda i: (i, 0)),
                   pl.BlockSpec((tm, V7X_LANES), lambda i: (i, 0)),
                   pl.BlockSpec((8, V7X_LANES), lambda i: (0, 0))],
        out_shape=[jax.ShapeDtypeStruct((t, d), F32),
                   jax.ShapeDtypeStruct((t * d // V7X_LANES, V7X_LANES), F32),
                   jax.ShapeDtypeStruct((t, V7X_LANES), F32),
                   jax.ShapeDtypeStruct((8, V7X_LANES), F32)],
        scratch_shapes=[pltpu.VMEM((8, V7X_LANES), F32)],
        compiler_params=_cparams(("arbitrary",), 40),
        name="oproj_moe_router",
    )(o, w_o, h, g.reshape(1, d), wr)


SCALAR_UNROLL = 8


def _sorted_list_kernel(beg_ref, end_ref, d0_ref, d1_ref, out_ref, *, chunk, t, tm, rows):
    i = pl.program_id(0)

    @pl.when(i == 0)
    def _():
        for e in range(N_EXPERTS + 1):
            beg, end = beg_ref[e], end_ref[e]
            n = end - beg

            def fill_group(g, carry, end=end):
                p0 = end - SCALAR_UNROLL * (g + 1)
                r0 = lax.rem(p0, tm)
                for k in range(SCALAR_UNROLL):
                    out_ref[p0 + k] = 2 * t + r0 + k
                return carry
            lax.fori_loop(0, n // SCALAR_UNROLL, fill_group, 0)

            def fill_one(p, carry):
                out_ref[p] = 2 * t + lax.rem(p, tm)
                return carry
            lax.fori_loop(beg, beg + lax.rem(n, SCALAR_UNROLL), fill_one, 0)

    base = i * chunk

    def body(g, carry):
        for k in range(SCALAR_UNROLL):
            j = g * SCALAR_UNROLL + k
            out_ref[d0_ref[0, 0, j]] = base + j
            out_ref[d1_ref[0, 0, j]] = t + base + j
        return carry
    lax.fori_loop(0, chunk // SCALAR_UNROLL, body, 0)


def sorted_assignment_list(dest0, dest1, pad_begin, pad_end, t, tm, rows):
    chunk = min(t, 2048)
    assert t % chunk == 0 and chunk % SCALAR_UNROLL == 0 and tm % SCALAR_UNROLL == 0
    blocked = lambda a: a.reshape(t // chunk, 1, chunk)
    dest_spec = pl.BlockSpec((1, 1, chunk), lambda i, beg, end: (i, 0, 0), memory_space=pltpu.SMEM)
    grid_spec = pltpu.PrefetchScalarGridSpec(
        num_scalar_prefetch=2,
        grid=(t // chunk,),
        in_specs=[dest_spec, dest_spec],
        out_specs=pl.BlockSpec(memory_space=pltpu.SMEM),
    )
    return pl.pallas_call(
        functools.partial(_sorted_list_kernel, chunk=chunk, t=t, tm=tm, rows=rows),
        grid_spec=grid_spec,
        out_shape=jax.ShapeDtypeStruct((rows,), jnp.int32),
        compiler_params=_cparams(("arbitrary",), 16),
        name="sorted_assignment_list",
    )(pad_begin, pad_end, blocked(dest0), blocked(dest1))


def _moe_kernel(te_ref, nu_ref, tokc_ref, tokn_ref, dst_ref, hn_hbm, w1_ref, w3_ref, w2_ref, y_hbm,
                xbuf, xb16, acc_ref, ybuf, gsem, ssem, *, tm, nf, nt, n_real):
    i = pl.program_id(0)
    j = pl.program_id(1)
    nu = nu_ref[0]
    slot = i % 2
    d = acc_ref.shape[1]
    c = _row_tile_chunks(d)

    def row_loop(start_row_copy):
        for r in range(tm):
            start_row_copy(r)

    def row(r):
        return pl.ds(r * c, c) if isinstance(r, int) else pl.ds(pl.multiple_of(r * c, c), c)

    def start_gather(tok_ref, s):
        row_loop(lambda r: pltpu.make_async_copy(
            hn_hbm.at[row(tok_ref[0, 0, r]), :], xbuf.at[s, row(r), :], gsem.at[s]).start())

    def wait_gather(s):
        pltpu.make_async_copy(hn_hbm.at[pl.ds(0, tm * c), :], xbuf.at[s], gsem.at[s]).wait()

    def start_scatter():
        row_loop(lambda r: pltpu.make_async_copy(
            ybuf.at[row(r), :], y_hbm.at[row(dst_ref[0, 0, r]), :], ssem.at[0]).start())

    def spare_rows_copy():
        return pltpu.make_async_copy(ybuf, y_hbm.at[pl.ds(n_real * c, tm * c), :], ssem.at[0])

    def wait_scatter():
        spare_rows_copy().wait()

    @pl.when((j == 0) & (i == 0))
    def _():
        ybuf[...] = jnp.zeros_like(ybuf)
        spare_rows_copy().start()
        spare_rows_copy().wait()

        @pl.when(nu > 0)
        def _():
            start_gather(tokc_ref, 0)

    @pl.when((j == 0) & (i < nu))
    def _():
        wait_gather(slot)
        for k, chunk in enumerate(_load_row_tiles(xbuf.at[slot], tm, d)):
            xb16[:, k * V7X_LANES:(k + 1) * V7X_LANES] = chunk.astype(BF16)

    @pl.when((j == 0) & (i + 1 < nu))
    def _():
        start_gather(tokn_ref, 1 - slot)

    @pl.when(i < nu)
    def _():
        x = xb16[...]
        a = _dot(x, w1_ref[0])
        b = _dot(x, w3_ref[0])
        part = _dot((_silu(a) * b).astype(BF16), w2_ref[0])

        @pl.when(j == 0)
        def _():
            acc_ref[...] = part

        @pl.when(j > 0)
        def _():
            acc_ref[...] += part

    last = j == nf - 1

    @pl.when(last & (i >= 1) & (i - 1 < nu))
    def _():
        wait_scatter()

    @pl.when(last & (i < nu))
    def _():
        _store_row_tiles(ybuf, acc_ref[...])
        start_scatter()

    @pl.when(last & (i == nt - 1) & (i < nu))
    def _():
        wait_scatter()


def moe_experts(hn, slab, counts, w1, w3, w2, tm, tf):
    t = slab.shape[0]
    d = w1.shape[1]
    c = _row_tile_chunks(d)
    f = w1.shape[2]
    nf = f // tf
    nt = (2 * t) // tm + N_EXPERTS
    rows = nt * tm

    e0 = slab[:, COL_E0].astype(jnp.int32)
    e1 = slab[:, COL_E1].astype(jnp.int32)
    r0 = slab[:, COL_R0].astype(jnp.int32)
    r1 = slab[:, COL_R1].astype(jnp.int32)
    cnt = counts[0, :N_EXPERTS].astype(jnp.int32)
    tiles = (cnt + tm - 1) // tm
    tile_end = jnp.cumsum(tiles)
    tile_start = tile_end - tiles
    nu = tile_end[-1:]
    offs = tile_start * tm
    dest0 = offs[e0] + r0
    dest1 = offs[e1] + r1
    tile_expert = jnp.minimum(
        jnp.sum(jnp.arange(nt, dtype=jnp.int32)[:, None] >= tile_end[None, :], axis=1), N_EXPERTS - 1
    ).astype(jnp.int32)
    pad_begin = jnp.concatenate([offs + cnt, nu * tm]).astype(jnp.int32)
    pad_end = jnp.concatenate([tile_end * tm, jnp.full((1,), rows, jnp.int32)]).astype(jnp.int32)
    dst_sorted = sorted_assignment_list(dest0, dest1, pad_begin, pad_end, t, tm, rows)
    tok_sorted = jnp.where(dst_sorted >= 2 * t, 0, jnp.where(dst_sorted >= t, dst_sorted - t, dst_sorted))
    tok3 = tok_sorted.reshape(nt, 1, tm)
    dst3 = dst_sorted.reshape(nt, 1, tm)

    def w_in(shape, which):
        def index_map(i, j, te, nu_):
            ii = jnp.minimum(i, nu_[0] - 1)
            jj = jnp.where(i < nu_[0], j, nf - 1)
            return (te[ii], 0, jj) if which == "up" else (te[ii], jj, 0)
        return pl.BlockSpec(shape, index_map)

    smem_tile = lambda f_: pl.BlockSpec((1, 1, tm), f_, memory_space=pltpu.SMEM)
    grid_spec = pltpu.PrefetchScalarGridSpec(
        num_scalar_prefetch=2,
        grid=(nt, nf),
        in_specs=[smem_tile(lambda i, j, te, nu_: (i, 0, 0)),
                  smem_tile(lambda i, j, te, nu_: (jnp.minimum(i + 1, nt - 1), 0, 0)),
                  smem_tile(lambda i, j, te, nu_: (i, 0, 0)),
                  pl.BlockSpec(memory_space=pl.ANY),
                  w_in((1, d, tf), "up"), w_in((1, d, tf), "up"), w_in((1, tf, d), "down")],
        out_specs=pl.BlockSpec(memory_space=pl.ANY),
        scratch_shapes=[pltpu.VMEM((2, tm * c, V7X_LANES), F32),
                        pltpu.VMEM((tm, d), BF16),
                        pltpu.VMEM((tm, d), F32),
                        pltpu.VMEM((tm * c, V7X_LANES), F32),
                        pltpu.SemaphoreType.DMA((2,)),
                        pltpu.SemaphoreType.DMA((1,))],
    )
    return pl.pallas_call(
        functools.partial(_moe_kernel, tm=tm, nf=nf, nt=nt, n_real=2 * t),
        grid_spec=grid_spec,
        out_shape=jax.ShapeDtypeStruct(((2 * t + tm) * c, V7X_LANES), F32),
        compiler_params=_cparams(("arbitrary", "arbitrary"), 56),
        name="moe_experts",
    )(tile_expert, nu, tok3, tok3, dst3, hn, w1, w3, w2)


def _combine_kernel(h_ref, y0_ref, y1_ref, slab_ref, g_ref, out_ref, *, final):
    slab = slab_ref[...]
    tm, d = h_ref.shape
    y0 = jnp.concatenate(_load_row_tiles(y0_ref, tm, d), axis=1)
    y1 = jnp.concatenate(_load_row_tiles(y1_ref, tm, d), axis=1)
    x = h_ref[...] + slab[:, COL_W0:COL_W0 + 1] * y0 + slab[:, COL_W1:COL_W1 + 1] * y1
    out_ref[...] = _rmsnorm_f32(x, g_ref[...]) if final else x


def moe_combine(h, y, slab, g, tm, final):
    t, d = h.shape
    nb = t // tm
    c = _row_tile_chunks(d)
    return pl.pallas_call(
        functools.partial(_combine_kernel, final=final),
        grid=(nb,),
        in_specs=[pl.BlockSpec((tm, d), lambda i: (i, 0)),
                  pl.BlockSpec((tm * c, V7X_LANES), lambda i: (i, 0)),
                  pl.BlockSpec((tm * c, V7X_LANES), lambda i: (i + nb, 0)),
                  pl.BlockSpec((tm, V7X_LANES), lambda i: (i, 0)),
                  pl.BlockSpec((1, d), lambda i: (0, 0))],
        out_specs=pl.BlockSpec((tm, d), lambda i: (i, 0)),
        out_shape=jax.ShapeDtypeStruct((t, d), F32),
        compiler_params=_cparams(("arbitrary",), 40),
        name="moe_combine",
    )(h, y, y, slab, g.reshape(1, d))


def _norm_kernel(h_ref, g_ref, out_ref):
    out_ref[...] = _rmsnorm_f32(h_ref[...], g_ref[...])


def final_norm_only(h, g, tm):
    t, d = h.shape
    return pl.pallas_call(
        _norm_kernel,
        grid=(t // tm,),
        in_specs=[pl.BlockSpec((tm, d), lambda i: (i, 0)), pl.BlockSpec((1, d), lambda i: (0, 0))],
        out_specs=pl.BlockSpec((tm, d), lambda i: (i, 0)),
        out_shape=jax.ShapeDtypeStruct((t, d), F32),
        compiler_params=_cparams(("arbitrary",), 40),
        name="final_norm",
    )(h, g.reshape(1, d))


def _row_tile(t, want):
    tm = min(want, t)
    assert t % tm == 0
    return tm


def kernel(x, w_qkv, w_o, mixer_norm, ffn_norm, rel_bias, w1, w3, w2, router, e_w1, e_w3, e_w2, final_norm):
    b, s, d = x.shape
    assert d == N_HEADS * HEAD_DIM and s % MOBA_BLOCK == 0
    t = b * s
    depth = w_qkv.shape[0]
    h = x.reshape(t, d)
    tm_big = _row_tile(t, 1024)
    tm_mid = _row_tile(t, 512)
    normed = False
    for i in range(depth):
        qkv = qkv_proj(h, mixer_norm[i], w_qkv[i].astype(BF16), tm_big)
        if i % 2 == 0:
            o = stick_breaking_attention(qkv, b, s, MOBA_BLOCK)
        else:
            o = moba_attention(qkv, rel_bias, b, s)
        o = o.reshape(t, d)
        wo = w_o[i].astype(BF16)
        jj = i // 2
        if i % 2 == 0:
            h = oproj_dense_ffn(o, wo, h, ffn_norm[i], w1[jj].astype(BF16), w3[jj].astype(BF16),
                                w2[jj].astype(BF16), tm_mid, 1024)
        else:
            h, hn, slab, counts = oproj_moe_router(o, wo, h, ffn_norm[i], router[jj], tm_mid)
            f_e = e_w1.shape[-1]
            tf = f_e // 2 if (f_e // 2) % 256 == 0 else f_e
            y = moe_experts(hn, slab, counts, e_w1[jj].astype(BF16), e_w3[jj].astype(BF16),
                            e_w2[jj].astype(BF16), tm_mid, tf)
            last = i == depth - 1
            h = moe_combine(h, y, slab, final_norm if last else ffn_norm[i], tm_mid, last)
            normed = last
    if not normed:
        h = final_norm_only(h, final_norm, tm_mid)
    return h.reshape(b, s, d)
```

```python
import functools
import math

import jax
import jax.numpy as jnp
from jax import lax
from jax.experimental import pallas as pl
from jax.experimental.pallas import tpu as pltpu

N_HEADS = 16
HEAD_DIM = 64
MOBA_BLOCK = 256
MOBA_TOPK = 3
N_BUCKETS = 32
MAX_DISTANCE = 128
N_EXPERTS = 8
RMS_EPS = 1e-6
NEG_BIG = -1e30

V7X_LANES = 128
ROW_DMA_UNROLL = 8
SB_GROUP = 4
MOBA_GROUP = 4
ROUTER_ROWS = 128
V7X_VMEM_BYTES = 64 * 1024 * 1024
HEADS_PER_LANE_BLOCK = V7X_LANES // HEAD_DIM

F32 = jnp.float32
BF16 = jnp.bfloat16


def _cparams(semantics, vmem_mb):
    assert vmem_mb * 1024 * 1024 < V7X_VMEM_BYTES
    return pltpu.CompilerParams(dimension_semantics=semantics,
                                vmem_limit_bytes=vmem_mb * 1024 * 1024)


def _rmsnorm_f32(x, g):
    return x * lax.rsqrt(jnp.mean(x * x, axis=-1, keepdims=True) + RMS_EPS) * g


def _silu(a):
    return a * (1.0 / (1.0 + jnp.exp(-a)))


def _dot(a, b):
    return jnp.dot(a, b, preferred_element_type=F32)


def _software_pipeline(tasks, stages):
    state = [dict() for _ in tasks]
    for step in range(len(tasks) + len(stages) - 1):
        for k in reversed(range(len(stages))):
            t = step - k
            if 0 <= t < len(tasks):
                stages[k](state[t], *tasks[t])


def _dot_nt(a, b):
    return lax.dot_general(a, b, (((1,), (1,)), ((), ())), preferred_element_type=F32)


def _qkv_kernel(x_ref, g_ref, w_ref, o_ref, hn_ref):
    j = pl.program_id(1)

    @pl.when(j == 0)
    def _():
        hn_ref[...] = _rmsnorm_f32(x_ref[...], g_ref[...]).astype(BF16)

    scale = jnp.where(j == 0, HEAD_DIM ** -0.5, 1.0)
    o_ref[0] = (_dot(hn_ref[...], w_ref[...]) * scale).astype(o_ref.dtype)


def qkv_proj(h, g, w_bf16, tm):
    t, d = h.shape
    return pl.pallas_call(
        _qkv_kernel,
        grid=(t // tm, 3),
        in_specs=[pl.BlockSpec((tm, d), lambda i, j: (i, 0)),
                  pl.BlockSpec((1, d), lambda i, j: (0, 0)),
                  pl.BlockSpec((d, d), lambda i, j: (0, j))],
        out_specs=pl.BlockSpec((1, tm, d), lambda i, j: (j, i, 0)),
        out_shape=jax.ShapeDtypeStruct((3, t, d), BF16),
        scratch_shapes=[pltpu.VMEM((tm, d), BF16)],
        compiler_params=_cparams(("arbitrary", "arbitrary"), 40),
        name="qkv_proj",
    )(h, g.reshape(1, d), w_bf16)


def _sb_kernel(q_ref, k_ref, v_ref, o_ref, acc0_ref, acc1_ref, r0_ref, r1_ref, u_ref, *, tq):
    row = lax.broadcasted_iota(jnp.int32, (tq, tq), 0)
    col = lax.broadcasted_iota(jnp.int32, (tq, tq), 1)
    u_ref[...] = -(row > col).astype(BF16)

    def q_tile(qi, carry):
        _sb_tile(qi, q_ref, k_ref, v_ref, o_ref, acc0_ref, acc1_ref, r0_ref, r1_ref, u_ref, tq=tq)
        return carry

    lax.fori_loop(0, q_ref.shape[2] // tq, q_tile, 0)


def _sb_tile(qi, q_ref, k_ref, v_ref, o_ref, acc0_ref, acc1_ref, r0_ref, r1_ref, u_ref, *, tq):
    acc_ref = (acc0_ref, acc1_ref)
    r_ref = (r0_ref, r1_ref)
    row = lax.broadcasted_iota(jnp.int32, (tq, tq), 0)
    col = lax.broadcasted_iota(jnp.int32, (tq, tq), 1)
    tile_rows = pl.ds(pl.multiple_of(qi * tq, tq), tq)
    q = q_ref[0, 0, tile_rows, :]
    lane = lax.broadcasted_iota(jnp.int32, (tq, V7X_LANES), 1)
    past = col < row
    heads = range(HEADS_PER_LANE_BLOCK)
    qms = [jnp.where(lane // HEAD_DIM == h, q, jnp.zeros_like(q)) for h in heads]

    def span(first, nb, diagonal):
        tasks = [(i, h) for i in reversed(range(nb)) for h in heads]
        state = [dict() for _ in tasks]
        r = [r_ref[h][...] for h in heads]
        acc = [None for h in heads]

        def stage(k, st, i, h):
            on_diagonal = diagonal and i == nb - 1
            start = pl.multiple_of((first + i) * tq, tq)
            if k == 0:
                st["z"] = _dot_nt(qms[h], k_ref[0, 0, pl.ds(start, tq), :])
            elif k == 1:
                z = st.pop("z")
                zb = z.astype(BF16)
                t = jnp.log(1 + jnp.exp(-jnp.abs(zb)))
                sp = jnp.maximum(zb, 0) + t
                if on_diagonal:
                    sp = jnp.where(past, sp, jnp.zeros_like(sp))
                st["sp"] = sp
                st["log_sig"] = jnp.minimum(z, 0.0) - t.astype(F32)
            elif k == 2:
                st["after"] = _dot(st["sp"], u_ref[...])
            elif k == 3:
                tot = st.pop("after") + r[h]
                w = jnp.exp(st.pop("log_sig") + tot)
                if on_diagonal:
                    w = jnp.where(past, w, 0.0)
                r[h] = tot[:, 0:1] - st.pop("sp")[:, 0:1].astype(F32)
                st["w"] = w.astype(BF16)
            else:
                part = _dot(st.pop("w"), v_ref[0, 0, pl.ds(start, tq), :])
                acc[h] = part if acc[h] is None else acc[h] + part

        n_stages = 5
        for step in range(len(tasks) + n_stages - 1):
            for k in reversed(range(n_stages)):
                ti = step - k
                if 0 <= ti < len(tasks):
                    stage(k, state[ti], *tasks[ti])
        for h in heads:
            r_ref[h][...] = r[h]
            acc_ref[h][...] += acc[h]

    for h in heads:
        r_ref[h][...] = jnp.zeros((tq, 1), F32)
        acc_ref[h][...] = jnp.zeros((tq, V7X_LANES), F32)

    n_groups = qi // SB_GROUP
    for in_group in range(SB_GROUP):
        @pl.when(qi % SB_GROUP == in_group)
        def _(in_group=in_group):
            span(n_groups * SB_GROUP, in_group + 1, True)

    def body(s, carry):
        span((n_groups - 1 - s) * SB_GROUP, SB_GROUP, False)
        return carry

    lax.fori_loop(0, n_groups, body, 0)

    o_ref[0, tile_rows, :] = jnp.where(lane // HEAD_DIM == 0, acc0_ref[...], acc1_ref[...]).astype(o_ref.dtype)


def stick_breaking_attention(qkv, b, s, tq):
    d = qkv.shape[-1]
    qkv4 = qkv.reshape(3, b, s, d)
    plane = lambda p: pl.BlockSpec((1, 1, s, V7X_LANES), lambda bi, hp: (p, bi, 0, hp))
    return pl.pallas_call(
        functools.partial(_sb_kernel, tq=tq),
        grid=(b, d // V7X_LANES),
        in_specs=[plane(0), plane(1), plane(2)],
        out_specs=pl.BlockSpec((1, s, V7X_LANES), lambda bi, hp: (bi, 0, hp)),
        out_shape=jax.ShapeDtypeStruct((b, s, d), BF16),
        scratch_shapes=[pltpu.VMEM((tq, V7X_LANES), F32), pltpu.VMEM((tq, V7X_LANES), F32),
                        pltpu.VMEM((tq, 1), F32), pltpu.VMEM((tq, 1), F32),
                        pltpu.VMEM((tq, tq), BF16)],
        compiler_params=_cparams(("arbitrary", "arbitrary"), 32),
        name="stick_breaking",
    )(qkv4, qkv4, qkv4)


def _t5_bias_kernel(rb_ref, o_ref, *, tq):
    h = pl.program_id(0)
    row = lax.broadcasted_iota(jnp.int32, (tq, tq), 0)
    col = lax.broadcasted_iota(jnp.int32, (tq, tq), 1)
    max_exact = N_BUCKETS // 2
    for o in range(2):
        dist = o * tq + row - col
        n = jnp.maximum(dist, 0)
        nf = jnp.maximum(n, 1).astype(F32)
        scaled = jnp.log(nf / max_exact) / math.log(MAX_DISTANCE / max_exact) * (N_BUCKETS - max_exact)
        large = max_exact + jnp.where(scaled < 0, jnp.ceil(scaled), jnp.floor(scaled)).astype(jnp.int32)
        large = jnp.minimum(large, N_BUCKETS - 1)
        bucket = jnp.where(n < max_exact, n, large)
        bias = jnp.zeros((tq, tq), F32)
        for bkt in range(N_BUCKETS):
            bias = jnp.where(bucket == bkt, rb_ref[bkt, h], bias)
        if o == 0:
            bias = jnp.where(dist >= 0, bias, NEG_BIG)
        o_ref[0, o] = bias


def t5_bias_tiles(rel_bias, tq):
    return pl.pallas_call(
        functools.partial(_t5_bias_kernel, tq=tq),
        grid=(N_HEADS,),
        in_specs=[pl.BlockSpec(memory_space=pltpu.SMEM)],
        out_specs=pl.BlockSpec((1, 2, tq, tq), lambda h: (h, 0, 0, 0)),
        out_shape=jax.ShapeDtypeStruct((N_HEADS, 2, tq, tq), F32),
        compiler_params=_cparams(("arbitrary",), 16),
        name="t5_bias_tiles",
    )(rel_bias)


def _moba_aux_lane0(h):
    assert HEADS_PER_LANE_BLOCK == 2
    return ((h + 1) % HEADS_PER_LANE_BLOCK) * HEAD_DIM


def _moba_kernel(rb_ref, qall_ref, k_ref, v_ref, bt_ref, o_ref,
                 s_ref, qaux_ref, mx_ref, l_ref, acc_ref, *, tq, nblk):
    heads = range(HEADS_PER_LANE_BLOCK)
    s_len = nblk * tq
    n_sel = min(MOBA_TOPK, nblk - 1)
    big = jnp.asarray(-NEG_BIG, BF16).astype(F32)
    aux_lane0 = _moba_aux_lane0
    assert nblk + 2 <= HEAD_DIM

    def block_choice():
        km = jnp.concatenate(
            [jnp.mean(k_ref[0, 0, n * tq:(n + 1) * tq, :].astype(F32), axis=0, keepdims=True)
             for n in range(nblk)], axis=0)
        km_hi = km.astype(BF16)
        km_lo = (km - km_hi.astype(F32)).astype(BF16)
        q_all = qall_ref[0, 0]
        lane_all = lax.broadcasted_iota(jnp.int32, (s_len, V7X_LANES), 1)
        blk = lax.broadcasted_iota(jnp.int32, (nblk, s_len), 0)
        own = lax.broadcasted_iota(jnp.int32, (nblk, s_len), 1) // tq
        own_row = own[0:1, :]
        for h in heads:
            qm = jnp.where(lane_all // HEAD_DIM == h, q_all, jnp.zeros_like(q_all))
            gate = _dot_nt(km_hi, qm) + _dot_nt(km_lo, qm)
            gate = jnp.where(blk < own, gate, NEG_BIG)
            rows = [gate[n:n + 1, :] for n in range(nblk)]
            rank = [jnp.zeros((1, s_len), F32) for _ in range(nblk)]
            for n in range(nblk):
                for m in range(n):
                    m_wins = jnp.where(rows[m] >= rows[n], 1.0, 0.0)
                    rank[n] = rank[n] + m_wins
                    rank[m] = rank[m] + (1.0 - m_wins)
            aux = [jnp.where((rank[n] < n_sel) & (n < own_row), 0.0, -big) for n in range(nblk)]
            aux += [jnp.ones((1, s_len), F32)] * 2
            before = jnp.zeros((aux_lane0(h), s_len), F32)
            after = jnp.zeros((V7X_LANES - aux_lane0(h) - len(aux), s_len), F32)
            pieces = ([before] if before.shape[0] else []) + aux + [after]
            qaux_ref[h] = jnp.concatenate(pieces, axis=0).T.astype(BF16)

    block_choice()

    def q_tile(qi, carry):
        _moba_tile(qi, rb_ref, qall_ref, k_ref, v_ref, bt_ref, o_ref,
                   s_ref, qaux_ref, mx_ref, l_ref, acc_ref, tq=tq, nblk=nblk)
        return carry

    lax.fori_loop(0, nblk, q_tile, 0)


def _moba_tile(qi, rb_ref, qall_ref, k_ref, v_ref, bt_ref, o_ref,
               s_ref, qaux_ref, mx_ref, l_ref, acc_ref, *, tq, nblk):
    hp = pl.program_id(1)
    heads = range(HEADS_PER_LANE_BLOCK)
    aux_lane0 = _moba_aux_lane0

    tile_rows = pl.ds(pl.multiple_of(qi * tq, tq), tq)
    q = qall_ref[0, 0, tile_rows, :]
    lane = lax.broadcasted_iota(jnp.int32, (tq, V7X_LANES), 1)
    q_aug = [jnp.where(lane // HEAD_DIM == h, q, qaux_ref[h, tile_rows, :]) for h in heads]
    both = lambda x: jnp.concatenate([x, x], axis=1)
    halves = lambda x: (x[:, :V7X_LANES], x[:, V7X_LANES:])
    for h in heads:
        mx_ref[h] = jnp.full((tq, V7X_LANES), NEG_BIG, F32)

    aux_rows = 16
    lane_aux = lax.broadcasted_iota(jnp.int32, (aux_rows, V7X_LANES), 1)
    key_aux_far = []
    for h in heads:
        far = jnp.full((aux_rows, V7X_LANES), rb_ref[N_BUCKETS - 1, hp * HEADS_PER_LANE_BLOCK + h], F32)
        far_hi = far.astype(BF16).astype(F32)
        far_lo = (far - far_hi).astype(BF16).astype(F32)
        key_aux_far.append(jnp.where(lane_aux == aux_lane0(h) + nblk, far_hi,
                                     jnp.where(lane_aux == aux_lane0(h) + nblk + 1, far_lo, 0.0)))

    def key_operand(h, n, role):
        start = pl.multiple_of(n * tq, tq)
        kb = k_ref[0, 0, pl.ds(start, tq), :]
        if role == "own":
            aux = jnp.zeros((aux_rows, V7X_LANES), F32)
        else:
            base = key_aux_far[h] if role == "far" else jnp.zeros((aux_rows, V7X_LANES), F32)
            aux = jnp.where(lane_aux == aux_lane0(h) + n, 1.0, base)
        aux = jnp.concatenate([aux.astype(BF16)] * (tq // aux_rows), axis=0)
        return jnp.where(lane // HEAD_DIM == h, kb, aux)

    pipeline = _software_pipeline

    def logits_span(first, roles, last=False):
        mx = [mx_ref[h] for h in heads]

        def products(st, i, h):
            st["s"] = _dot_nt(q_aug[h], key_operand(h, first + i, roles[i]))

        def finish(st, i, h):
            start = pl.multiple_of((first + i) * tq, tq)
            s = st.pop("s")
            if roles[i] == "own":
                s = s + bt_ref[h, 0]
            elif roles[i] == "prev":
                s = s + bt_ref[h, 1]
            s_ref[h, :, pl.ds(start, tq)] = s
            s_lo, s_hi = halves(s)
            mx[h] = jnp.maximum(mx[h], jnp.maximum(s_lo, s_hi))

        pipeline([(i, h) for i in range(len(roles)) for h in heads], [products, finish])
        for h in heads:
            if last:
                mx_ref[h] = jnp.broadcast_to(jnp.max(mx[h], axis=1, keepdims=True), (tq, V7X_LANES))
                l_ref[h] = jnp.zeros((tq, V7X_LANES), F32)
                acc_ref[h] = jnp.zeros((tq, V7X_LANES), F32)
            else:
                mx_ref[h] = mx[h]

    def probs_span(first, nb, last=False):
        m = [both(mx_ref[h]) for h in heads]
        l = [l_ref[h] for h in heads]
        acc = [acc_ref[h] for h in heads]

        def exponentials(st, i, h):
            start = pl.multiple_of((first + i) * tq, tq)
            p = jnp.exp(s_ref[h, :, pl.ds(start, tq)] - m[h])
            p_lo, p_hi = halves(p)
            l[h] = l[h] + (p_lo + p_hi)
            st["p"] = p.astype(BF16)

        def accumulate(st, i, h):
            start = pl.multiple_of((first + i) * tq, tq)
            acc[h] = acc[h] + _dot(st.pop("p"), v_ref[0, 0, pl.ds(start, tq), :])

        pipeline([(i, h) for i in range(nb) for h in heads], [exponentials, accumulate])
        if last:
            out = [acc[h] / jnp.sum(l[h], axis=1, keepdims=True) for h in heads]
            o_ref[0, tile_rows, :] = jnp.where(lane // HEAD_DIM == 0, out[0], out[1]).astype(o_ref.dtype)
        else:
            for h in heads:
                l_ref[h] = l[h]
                acc_ref[h] = acc[h]

    group = MOBA_GROUP
    own_group = qi // group
    in_group = qi % group

    def far_groups(g, carry):
        logits_span(g * group, ["far"] * group)
        return carry

    lax.fori_loop(0, own_group - 1, far_groups, 0)

    @pl.when((own_group >= 1) & (in_group == 0))
    def _():
        logits_span((own_group - 1) * group, ["far"] * (group - 1) + ["prev"])

    @pl.when((own_group >= 1) & (in_group != 0))
    def _():
        logits_span((own_group - 1) * group, ["far"] * group)

    for size in range(1, group + 1):
        @pl.when(in_group == size - 1)
        def _(size=size):
            logits_span(own_group * group, (["far"] * group + ["prev", "own"])[-size:]
                        if size >= 2 else ["own"], last=True)

    def prob_groups(g, carry):
        probs_span(g * group, group)
        return carry

    lax.fori_loop(0, own_group, prob_groups, 0)
    for size in range(1, group + 1):
        @pl.when(in_group == size - 1)
        def _(size=size):
            probs_span(own_group * group, size, last=True)


def moba_attention(qkv, rel_bias, b, s):
    d = qkv.shape[-1]
    tq = MOBA_BLOCK
    assert s % tq == 0 and tq >= 2 * MAX_DISTANCE
    nblk = s // tq
    qkv4 = qkv.reshape(3, b, s, d)
    tiles = t5_bias_tiles(rel_bias, tq)
    return pl.pallas_call(
        functools.partial(_moba_kernel, tq=tq, nblk=nblk),
        grid=(b, d // V7X_LANES),
        in_specs=[pl.BlockSpec(memory_space=pltpu.SMEM),
                  pl.BlockSpec((1, 1, s, V7X_LANES), lambda bi, hp: (0, bi, 0, hp)),
                  pl.BlockSpec((1, 1, s, V7X_LANES), lambda bi, hp: (1, bi, 0, hp)),
                  pl.BlockSpec((1, 1, s, V7X_LANES), lambda bi, hp: (2, bi, 0, hp)),
                  pl.BlockSpec((HEADS_PER_LANE_BLOCK, 2, tq, tq), lambda bi, hp: (hp, 0, 0, 0))],
        out_specs=pl.BlockSpec((1, s, V7X_LANES), lambda bi, hp: (bi, 0, hp)),
        out_shape=jax.ShapeDtypeStruct((b, s, d), BF16),
        scratch_shapes=[pltpu.VMEM((HEADS_PER_LANE_BLOCK, tq, s), F32),
                        pltpu.VMEM((HEADS_PER_LANE_BLOCK, s, V7X_LANES), BF16),
                        pltpu.VMEM((HEADS_PER_LANE_BLOCK, tq, V7X_LANES), F32),
                        pltpu.VMEM((HEADS_PER_LANE_BLOCK, tq, V7X_LANES), F32),
                        pltpu.VMEM((HEADS_PER_LANE_BLOCK, tq, V7X_LANES), F32)],
        compiler_params=_cparams(("arbitrary", "arbitrary"), 32),
        name="moba",
    )(rel_bias, qkv4, qkv4, qkv4, tiles)


def _whole(shape):
    return pl.BlockSpec(shape, lambda i: (0,) * len(shape), pipeline_mode=pl.Buffered(1))


def _ffn_kernel(o_ref, wo_ref, h_ref, g_ref, w1_ref, w3_ref, w2_ref, out_ref, *, chunks):
    x = h_ref[...] + _dot(o_ref[...], wo_ref[...])
    hn = _rmsnorm_f32(x, g_ref[...]).astype(BF16)
    acc = x
    for c0, c1 in chunks:
        a = _dot(hn, w1_ref[:, c0:c1])
        b = _dot(hn, w3_ref[:, c0:c1])
        acc = acc + _dot((_silu(a) * b).astype(BF16), w2_ref[c0:c1, :])
    out_ref[...] = acc


def oproj_dense_ffn(o, w_o, h, g, w1, w3, w2, tm, chunk):
    t, d = h.shape
    f = w1.shape[1]
    chunks = tuple((c, min(c + chunk, f)) for c in range(0, f, chunk))
    return pl.pallas_call(
        functools.partial(_ffn_kernel, chunks=chunks),
        grid=(t // tm,),
        in_specs=[pl.BlockSpec((tm, d), lambda i: (i, 0)),
                  _whole((d, d)),
                  pl.BlockSpec((tm, d), lambda i: (i, 0)),
                  pl.BlockSpec((1, d), lambda i: (0, 0)),
                  _whole((d, f)), _whole((d, f)), _whole((f, d))],
        out_specs=pl.BlockSpec((tm, d), lambda i: (i, 0)),
        out_shape=jax.ShapeDtypeStruct((t, d), F32),
        compiler_params=_cparams(("arbitrary",), 56),
        name="oproj_dense_ffn",
    )(o, w_o, h, g.reshape(1, d), w1, w3, w2)


def _row_tile_chunks(d):
    assert d % V7X_LANES == 0
    return d // V7X_LANES


def _store_row_tiles(ref, x, row0=0):
    n, d = x.shape
    c = _row_tile_chunks(d)
    for k in range(c):
        ref[pl.ds(row0 * c + k, n, stride=c), :] = x[:, k * V7X_LANES:(k + 1) * V7X_LANES]


def _load_row_tiles(ref, n, d):
    c = _row_tile_chunks(d)
    return [ref[pl.ds(k, n, stride=c), :] for k in range(c)]


COL_E0, COL_E1, COL_W0, COL_W1, COL_R0, COL_R1 = range(6)


def _router_kernel(o_ref, wo_ref, h_ref, g_ref, wr_ref, h1_ref, hn_ref, slab_ref, cnt_ref, carry_ref, *, tm):
    i = pl.program_id(0)

    @pl.when(i == 0)
    def _():
        carry_ref[...] = jnp.zeros_like(carry_ref)

    rs = min(ROUTER_ROWS, tm)
    assert tm % rs == 0
    g = g_ref[...]
    w = wr_ref[...]
    w_hi = w.astype(BF16)
    w_lo = (w - w_hi.astype(F32)).astype(BF16)
    lane = lax.broadcasted_iota(jnp.int32, (rs, V7X_LANES), 1)
    earlier = (lax.broadcasted_iota(jnp.int32, (rs, rs), 1)
               < lax.broadcasted_iota(jnp.int32, (rs, rs), 0)).astype(BF16)
    neg_inf = jnp.float32(-jnp.inf)
    count = [carry_ref[0:1, :]]

    def project(st, r):
        rows = pl.ds(r * rs, rs)
        h1 = h_ref[rows, :] + _dot(o_ref[rows, :], wo_ref[...])
        h1_ref[rows, :] = h1
        st["h1"] = h1

    def normalise(st, r):
        hn = _rmsnorm_f32(st.pop("h1"), g)
        _store_row_tiles(hn_ref, hn, r * rs)
        st["hi"] = hn.astype(BF16)
        st["lo"] = (hn - st["hi"].astype(F32)).astype(BF16)

    def logits(st, r):
        hi, lo = st.pop("hi"), st.pop("lo")
        st["logits"] = _dot(hi, w_hi) + _dot(hi, w_lo) + _dot(lo, w_hi)

    def top2(st, r):
        lg = jnp.where(lane < N_EXPERTS, st.pop("logits"), neg_inf)
        rank_of = jnp.zeros((rs, V7X_LANES), F32)
        for s in range(1, N_EXPERTS):
            lower = pltpu.roll(lg, s, axis=1)
            higher = pltpu.roll(lg, V7X_LANES - s, axis=1)
            rank_of = rank_of + jnp.where(lower >= lg, 1.0, 0.0) + jnp.where(higher > lg, 1.0, 0.0)
        first = (rank_of == 0.0) & (lane < N_EXPERTS)
        second = (rank_of == 1.0) & (lane < N_EXPERTS)
        pick = lambda sel, x: jnp.sum(jnp.where(sel, x, 0.0), axis=1, keepdims=True)
        lane_f = lane.astype(F32)
        m0, m1 = pick(first, lg), pick(second, lg)
        i0, i1 = pick(first, lane_f).astype(jnp.int32), pick(second, lane_f).astype(jnp.int32)
        e = jnp.exp(m1 - m0)
        st.update(i0=i0, i1=i1, w0=1.0 / (1.0 + e), w1=e / (1.0 + e),
                  hot=((lane == i0) | (lane == i1)).astype(BF16))

    def rank(st, r):
        hot = st.pop("hot")
        before = _dot(earlier, hot) + count[0]
        i0, i1 = st.pop("i0"), st.pop("i1")
        r0 = jnp.sum(jnp.where(lane == i0, before, 0.0), axis=1, keepdims=True)
        r1 = jnp.sum(jnp.where(lane == i1, before, 0.0), axis=1, keepdims=True)
        count[0] = count[0] + jnp.sum(hot.astype(F32), axis=0, keepdims=True)
        slab = jnp.zeros((rs, V7X_LANES), F32)
        for c, val in ((COL_E0, i0.astype(F32)), (COL_E1, i1.astype(F32)), (COL_W0, st.pop("w0")),
                       (COL_W1, st.pop("w1")), (COL_R0, r0), (COL_R1, r1)):
            slab = jnp.where(lane == c, val, slab)
        slab_ref[pl.ds(r * rs, rs), :] = slab

    _software_pipeline([(r,) for r in range(tm // rs)], [project, normalise, logits, top2, rank])
    carry_ref[0:1, :] = count[0]
    cnt_ref[...] = carry_ref[...]


def oproj_moe_router(o, w_o, h, g, w_router, tm):
    t, d = h.shape
    wr = jnp.zeros((d, V7X_LANES), F32).at[:, :N_EXPERTS].set(w_router)
    return pl.pallas_call(
        functools.partial(_router_kernel, tm=tm),
        grid=(t // tm,),
        in_specs=[pl.BlockSpec((tm, d), lambda i: (i, 0)),
                  _whole((d, d)),
                  pl.BlockSpec((tm, d), lambda i: (i, 0)),
                  pl.BlockSpec((1, d), lambda i: (0, 0)),
                  pl.BlockSpec((d, V7X_LANES), lambda i: (0, 0))],
        out_specs=[pl.BlockSpec((tm, d), lambda i: (i, 0)),
                   pl.BlockSpec((tm * d // V7X_LANES, V7X_LANES), lambda i: (i, 0)),
                   pl.BlockSpec((tm, V7X_LANES), lambda i: (i, 0)),
                   pl.BlockSpec((8, V7X_LANES), lambda i: (0, 0))],
        out_shape=[jax.ShapeDtypeStruct((t, d), F32),
                   jax.ShapeDtypeStruct((t * d // V7X_LANES, V7X_LANES), F32),
                   jax.ShapeDtypeStruct((t, V7X_LANES), F32),
                   jax.ShapeDtypeStruct((8, V7X_LANES), F32)],
        scratch_shapes=[pltpu.VMEM((8, V7X_LANES), F32)],
        compiler_params=_cparams(("arbitrary",), 40),
        name="oproj_moe_router",
    )(o, w_o, h, g.reshape(1, d), wr)


SCALAR_UNROLL = 8


def _sorted_list_kernel(beg_ref, end_ref, d0_ref, d1_ref, out_ref, *, chunk, t, tm, rows):
    i = pl.program_id(0)

    @pl.when(i == 0)
    def _():
        for e in range(N_EXPERTS + 1):
            beg, end = beg_ref[e], end_ref[e]
            n = end - beg

            def fill_group(g, carry, end=end):
                p0 = end - SCALAR_UNROLL * (g + 1)
                r0 = lax.rem(p0, tm)
                for k in range(SCALAR_UNROLL):
                    out_ref[p0 + k] = 2 * t + r0 + k
                return carry
            lax.fori_loop(0, n // SCALAR_UNROLL, fill_group, 0)

            def fill_one(p, carry):
                out_ref[p] = 2 * t + lax.rem(p, tm)
                return carry
            lax.fori_loop(beg, beg + lax.rem(n, SCALAR_UNROLL), fill_one, 0)

    base = i * chunk

    def body(g, carry):
        for k in range(SCALAR_UNROLL):
            j = g * SCALAR_UNROLL + k
            out_ref[d0_ref[0, 0, j]] = base + j
            out_ref[d1_ref[0, 0, j]] = t + base + j
        return carry
    lax.fori_loop(0, chunk // SCALAR_UNROLL, body, 0)


def sorted_assignment_list(dest0, dest1, pad_begin, pad_end, t, tm, rows):
    chunk = min(t, 2048)
    assert t % chunk == 0 and chunk % SCALAR_UNROLL == 0 and tm % SCALAR_UNROLL == 0
    blocked = lambda a: a.reshape(t // chunk, 1, chunk)
    dest_spec = pl.BlockSpec((1, 1, chunk), lambda i, beg, end: (i, 0, 0), memory_space=pltpu.SMEM)
    grid_spec = pltpu.PrefetchScalarGridSpec(
        num_scalar_prefetch=2,
        grid=(t // chunk,),
        in_specs=[dest_spec, dest_spec],
        out_specs=pl.BlockSpec(memory_space=pltpu.SMEM),
    )
    return pl.pallas_call(
        functools.partial(_sorted_list_kernel, chunk=chunk, t=t, tm=tm, rows=rows),
        grid_spec=grid_spec,
        out_shape=jax.ShapeDtypeStruct((rows,), jnp.int32),
        compiler_params=_cparams(("arbitrary",), 16),
        name="sorted_assignment_list",
    )(pad_begin, pad_end, blocked(dest0), blocked(dest1))


def _moe_kernel(te_ref, nu_ref, tokc_ref, tokn_ref, dst_ref, hn_hbm, w1_ref, w3_ref, w2_ref, y_hbm,
                xbuf, xb16, acc_ref, ybuf, gsem, ssem, *, tm, nf, nt, n_real):
    i = pl.program_id(0)
    j = pl.program_id(1)
    nu = nu_ref[0]
    slot = i % 2
    d = acc_ref.shape[1]
    c = _row_tile_chunks(d)

    def row_loop(start_row_copy):
        for r in range(tm):
            start_row_copy(r)

    def row(r):
        return pl.ds(r * c, c) if isinstance(r, int) else pl.ds(pl.multiple_of(r * c, c), c)

    def start_gather(tok_ref, s):
        row_loop(lambda r: pltpu.make_async_copy(
            hn_hbm.at[row(tok_ref[0, 0, r]), :], xbuf.at[s, row(r), :], gsem.at[s]).start())

    def wait_gather(s):
        pltpu.make_async_copy(hn_hbm.at[pl.ds(0, tm * c), :], xbuf.at[s], gsem.at[s]).wait()

    def start_scatter():
        row_loop(lambda r: pltpu.make_async_copy(
            ybuf.at[row(r), :], y_hbm.at[row(dst_ref[0, 0, r]), :], ssem.at[0]).start())

    def spare_rows_copy():
        return pltpu.make_async_copy(ybuf, y_hbm.at[pl.ds(n_real * c, tm * c), :], ssem.at[0])

    def wait_scatter():
        spare_rows_copy().wait()

    @pl.when((j == 0) & (i == 0))
    def _():
        ybuf[...] = jnp.zeros_like(ybuf)
        spare_rows_copy().start()
        spare_rows_copy().wait()

        @pl.when(nu > 0)
        def _():
            start_gather(tokc_ref, 0)

    @pl.when((j == 0) & (i < nu))
    def _():
        wait_gather(slot)
        for k, chunk in enumerate(_load_row_tiles(xbuf.at[slot], tm, d)):
            xb16[:, k * V7X_LANES:(k + 1) * V7X_LANES] = chunk.astype(BF16)

    @pl.when((j == 0) & (i + 1 < nu))
    def _():
        start_gather(tokn_ref, 1 - slot)

    @pl.when(i < nu)
    def _():
        x = xb16[...]
        a = _dot(x, w1_ref[0])
        b = _dot(x, w3_ref[0])
        part = _dot((_silu(a) * b).astype(BF16), w2_ref[0])

        @pl.when(j == 0)
        def _():
            acc_ref[...] = part

        @pl.when(j > 0)
        def _():
            acc_ref[...] += part

    last = j == nf - 1

    @pl.when(last & (i >= 1) & (i - 1 < nu))
    def _():
        wait_scatter()

    @pl.when(last & (i < nu))
    def _():
        _store_row_tiles(ybuf, acc_ref[...])
        start_scatter()

    @pl.when(last & (i == nt - 1) & (i < nu))
    def _():
        wait_scatter()


def moe_experts(hn, slab, counts, w1, w3, w2, tm, tf):
    t = slab.shape[0]
    d = w1.shape[1]
    c = _row_tile_chunks(d)
    f = w1.shape[2]
    nf = f // tf
    nt = (2 * t) // tm + N_EXPERTS
    rows = nt * tm

    e0 = slab[:, COL_E0].astype(jnp.int32)
    e1 = slab[:, COL_E1].astype(jnp.int32)
    r0 = slab[:, COL_R0].astype(jnp.int32)
    r1 = slab[:, COL_R1].astype(jnp.int32)
    cnt = counts[0, :N_EXPERTS].astype(jnp.int32)
    tiles = (cnt + tm - 1) // tm
    tile_end = jnp.cumsum(tiles)
    tile_start = tile_end - tiles
    nu = tile_end[-1:]
    offs = tile_start * tm
    dest0 = offs[e0] + r0
    dest1 = offs[e1] + r1
    tile_expert = jnp.minimum(
        jnp.sum(jnp.arange(nt, dtype=jnp.int32)[:, None] >= tile_end[None, :], axis=1), N_EXPERTS - 1
    ).astype(jnp.int32)
    pad_begin = jnp.concatenate([offs + cnt, nu * tm]).astype(jnp.int32)
    pad_end = jnp.concatenate([tile_end * tm, jnp.full((1,), rows, jnp.int32)]).astype(jnp.int32)
    dst_sorted = sorted_assignment_list(dest0, dest1, pad_begin, pad_end, t, tm, rows)
    tok_sorted = jnp.where(dst_sorted >= 2 * t, 0, jnp.where(dst_sorted >= t, dst_sorted - t, dst_sorted))
    tok3 = tok_sorted.reshape(nt, 1, tm)
    dst3 = dst_sorted.reshape(nt, 1, tm)

    def w_in(shape, which):
        def index_map(i, j, te, nu_):
            ii = jnp.minimum(i, nu_[0] - 1)
            jj = jnp.where(i < nu_[0], j, nf - 1)
            return (te[ii], 0, jj) if which == "up" else (te[ii], jj, 0)
        return pl.BlockSpec(shape, index_map)

    smem_tile = lambda f_: pl.BlockSpec((1, 1, tm), f_, memory_space=pltpu.SMEM)
    grid_spec = pltpu.PrefetchScalarGridSpec(
        num_scalar_prefetch=2,
        grid=(nt, nf),
        in_specs=[smem_tile(lambda i, j, te, nu_: (i, 0, 0)),
                  smem_tile(lambda i, j, te, nu_: (jnp.minimum(i + 1, nt - 1), 0, 0)),
                  smem_tile(lambda i, j, te, nu_: (i, 0, 0)),
                  pl.BlockSpec(memory_space=pl.ANY),
                  w_in((1, d, tf), "up"), w_in((1, d, tf), "up"), w_in((1, tf, d), "down")],
        out_specs=pl.BlockSpec(memory_space=pl.ANY),
        scratch_shapes=[pltpu.VMEM((2, tm * c, V7X_LANES), F32),
                        pltpu.VMEM((tm, d), BF16),
                        pltpu.VMEM((tm, d), F32),
                        pltpu.VMEM((tm * c, V7X_LANES), F32),
                        pltpu.SemaphoreType.DMA((2,)),
                        pltpu.SemaphoreType.DMA((1,))],
    )
    return pl.pallas_call(
        functools.partial(_moe_kernel, tm=tm, nf=nf, nt=nt, n_real=2 * t),
        grid_spec=grid_spec,
        out_shape=jax.ShapeDtypeStruct(((2 * t + tm) * c, V7X_LANES), F32),
        compiler_params=_cparams(("arbitrary", "arbitrary"), 56),
        name="moe_experts",
    )(tile_expert, nu, tok3, tok3, dst3, hn, w1, w3, w2)


def _combine_kernel(h_ref, y0_ref, y1_ref, slab_ref, g_ref, out_ref, *, final):
    slab = slab_ref[...]
    tm, d = h_ref.shape
    y0 = jnp.concatenate(_load_row_tiles(y0_ref, tm, d), axis=1)
    y1 = jnp.concatenate(_load_row_tiles(y1_ref, tm, d), axis=1)
    x = h_ref[...] + slab[:, COL_W0:COL_W0 + 1] * y0 + slab[:, COL_W1:COL_W1 + 1] * y1
    out_ref[...] = _rmsnorm_f32(x, g_ref[...]) if final else x


def moe_combine(h, y, slab, g, tm, final):
    t, d = h.shape
    nb = t // tm
    c = _row_tile_chunks(d)
    return pl.pallas_call(
        functools.partial(_combine_kernel, final=final),
        grid=(nb,),
        in_specs=[pl.BlockSpec((tm, d), lambda i: (i, 0)),
                  pl.BlockSpec((tm * c, V7X_LANES), lambda i: (i, 0)),
                  pl.BlockSpec((tm * c, V7X_LANES), lambda i: (i + nb, 0)),
                  pl.BlockSpec((tm, V7X_LANES), lambda i: (i, 0)),
                  pl.BlockSpec((1, d), lambda i: (0, 0))],
        out_specs=pl.BlockSpec((tm, d), lambda i: (i, 0)),
        out_shape=jax.ShapeDtypeStruct((t, d), F32),
        compiler_params=_cparams(("arbitrary",), 40),
        name="moe_combine",
    )(h, y, y, slab, g.reshape(1, d))


def _norm_kernel(h_ref, g_ref, out_ref):
    out_ref[...] = _rmsnorm_f32(h_ref[...], g_ref[...])


def final_norm_only(h, g, tm):
    t, d = h.shape
    return pl.pallas_call(
        _norm_kernel,
        grid=(t // tm,),
        in_specs=[pl.BlockSpec((tm, d), lambda i: (i, 0)), pl.BlockSpec((1, d), lambda i: (0, 0))],
        out_specs=pl.BlockSpec((tm, d), lambda i: (i, 0)),
        out_shape=jax.ShapeDtypeStruct((t, d), F32),
        compiler_params=_cparams(("arbitrary",), 40),
        name="final_norm",
    )(h, g.reshape(1, d))


def _row_tile(t, want):
    tm = min(want, t)
    assert t % tm == 0
    return tm


def kernel(x, w_qkv, w_o, mixer_norm, ffn_norm, rel_bias, w1, w3, w2, router, e_w1, e_w3, e_w2, final_norm):
    b, s, d = x.shape
    assert d == N_HEADS * HEAD_DIM and s % MOBA_BLOCK == 0
    t = b * s
    depth = w_qkv.shape[0]
    h = x.reshape(t, d)
    tm_big = _row_tile(t, 1024)
    tm_mid = _row_tile(t, 512)
    normed = False
    for i in range(depth):
        qkv = qkv_proj(h, mixer_norm[i], w_qkv[i].astype(BF16), tm_big)
        if i % 2 == 0:
            o = stick_breaking_attention(qkv, b, s, MOBA_BLOCK)
        else:
            o = moba_attention(qkv, rel_bias, b, s)
        o = o.reshape(t, d)
        wo = w_o[i].astype(BF16)
        jj = i // 2
        if i % 2 == 0:
            h = oproj_dense_ffn(o, wo, h, ffn_norm[i], w1[jj].astype(BF16), w3[jj].astype(BF16),
                                w2[jj].astype(BF16), tm_mid, 1024)
        else:
            h, hn, slab, counts = oproj_moe_router(o, wo, h, ffn_norm[i], router[jj], tm_mid)
            f_e = e_w1.shape[-1]
            tf = f_e // 2 if (f_e // 2) % 256 == 0 else f_e
            y = moe_experts(hn, slab, counts, e_w1[jj].astype(BF16), e_w3[jj].astype(BF16),
                            e_w2[jj].astype(BF16), tm_mid, tf)
            last = i == depth - 1
            h = moe_combine(h, y, slab, final_norm if last else ffn_norm[i], tm_mid, last)
            normed = last
    if not normed:
        h = final_norm_only(h, final_norm, tm_mid)
    return h.reshape(b, s, d)
```

```python
import functools
import math

import jax
import jax.numpy as jnp
from jax import lax
from jax.experimental import pallas as pl
from jax.experimental.pallas import tpu as pltpu

N_HEADS = 16
HEAD_DIM = 64
MOBA_BLOCK = 256
MOBA_TOPK = 3
N_BUCKETS = 32
MAX_DISTANCE = 128
N_EXPERTS = 8
RMS_EPS = 1e-6
NEG_BIG = -1e30

V7X_LANES = 128
SB_GROUP = 4
MOBA_GROUP = 4
ROUTER_ROWS = 128
V7X_VMEM_BYTES = 64 * 1024 * 1024
HEADS_PER_LANE_BLOCK = V7X_LANES // HEAD_DIM

F32 = jnp.float32
BF16 = jnp.bfloat16


def _cparams(semantics, vmem_mb):
    assert vmem_mb * 1024 * 1024 < V7X_VMEM_BYTES
    return pltpu.CompilerParams(dimension_semantics=semantics,
                                vmem_limit_bytes=vmem_mb * 1024 * 1024)


def _rmsnorm_f32(x, g):
    return x * lax.rsqrt(jnp.mean(x * x, axis=-1, keepdims=True) + RMS_EPS) * g


def _silu(a):
    return a * (1.0 / (1.0 + jnp.exp(-a)))


def _dot(a, b):
    return jnp.dot(a, b, preferred_element_type=F32)


def _software_pipeline(tasks, stages):
    state = [dict() for _ in tasks]
    for step in range(len(tasks) + len(stages) - 1):
        for k in reversed(range(len(stages))):
            t = step - k
            if 0 <= t < len(tasks):
                stages[k](state[t], *tasks[t])


def _dot_nt(a, b):
    return lax.dot_general(a, b, (((1,), (1,)), ((), ())), preferred_element_type=F32)


def _qkv_kernel(x_ref, g_ref, w_ref, o_ref, hn_ref):
    j = pl.program_id(1)

    @pl.when(j == 0)
    def _():
        hn_ref[...] = _rmsnorm_f32(x_ref[...], g_ref[...]).astype(BF16)

    scale = jnp.where(j == 0, HEAD_DIM ** -0.5, 1.0)
    o_ref[0] = (_dot(hn_ref[...], w_ref[...]) * scale).astype(o_ref.dtype)


def qkv_proj(h, g, w_bf16, tm):
    t, d = h.shape
    return pl.pallas_call(
        _qkv_kernel,
        grid=(t // tm, 3),
        in_specs=[pl.BlockSpec((tm, d), lambda i, j: (i, 0)),
                  pl.BlockSpec((1, d), lambda i, j: (0, 0)),
                  pl.BlockSpec((d, d), lambda i, j: (0, j))],
        out_specs=pl.BlockSpec((1, tm, d), lambda i, j: (j, i, 0)),
        out_shape=jax.ShapeDtypeStruct((3, t, d), BF16),
        scratch_shapes=[pltpu.VMEM((tm, d), BF16)],
        compiler_params=_cparams(("arbitrary", "arbitrary"), 40),
        name="qkv_proj",
    )(h, g.reshape(1, d), w_bf16)


def _sb_kernel(q_ref, k_ref, v_ref, o_ref, acc0_ref, acc1_ref, r0_ref, r1_ref, u_ref, *, tq):
    row = lax.broadcasted_iota(jnp.int32, (tq, tq), 0)
    col = lax.broadcasted_iota(jnp.int32, (tq, tq), 1)
    u_ref[...] = -(row > col).astype(BF16)

    def q_tile(qi, carry):
        _sb_tile(qi, q_ref, k_ref, v_ref, o_ref, acc0_ref, acc1_ref, r0_ref, r1_ref, u_ref, tq=tq)
        return carry

    lax.fori_loop(0, q_ref.shape[2] // tq, q_tile, 0)


def _sb_tile(qi, q_ref, k_ref, v_ref, o_ref, acc0_ref, acc1_ref, r0_ref, r1_ref, u_ref, *, tq):
    acc_ref = (acc0_ref, acc1_ref)
    r_ref = (r0_ref, r1_ref)
    row = lax.broadcasted_iota(jnp.int32, (tq, tq), 0)
    col = lax.broadcasted_iota(jnp.int32, (tq, tq), 1)
    tile_rows = pl.ds(pl.multiple_of(qi * tq, tq), tq)
    q = q_ref[0, 0, tile_rows, :]
    lane = lax.broadcasted_iota(jnp.int32, (tq, V7X_LANES), 1)
    past = col < row
    heads = range(HEADS_PER_LANE_BLOCK)
    qms = [jnp.where(lane // HEAD_DIM == h, q, jnp.zeros_like(q)) for h in heads]

    def span(first, nb, diagonal):
        tasks = [(i, h) for i in reversed(range(nb)) for h in heads]
        state = [dict() for _ in tasks]
        r = [r_ref[h][...] for h in heads]
        acc = [None for h in heads]

        def stage(k, st, i, h):
            on_diagonal = diagonal and i == nb - 1
            start = pl.multiple_of((first + i) * tq, tq)
            if k == 0:
                st["z"] = _dot_nt(qms[h], k_ref[0, 0, pl.ds(start, tq), :])
            elif k == 1:
                z = st.pop("z")
                zb = z.astype(BF16)
                t = jnp.log(1 + jnp.exp(-jnp.abs(zb)))
                sp = jnp.maximum(zb, 0) + t
                if on_diagonal:
                    sp = jnp.where(past, sp, jnp.zeros_like(sp))
                st["sp"] = sp
                st["log_sig"] = jnp.minimum(z, 0.0) - t.astype(F32)
            elif k == 2:
                st["after"] = _dot(st["sp"], u_ref[...])
            elif k == 3:
                tot = st.pop("after") + r[h]
                w = jnp.exp(st.pop("log_sig") + tot)
                if on_diagonal:
                    w = jnp.where(past, w, 0.0)
                r[h] = tot[:, 0:1] - st.pop("sp")[:, 0:1].astype(F32)
                st["w"] = w.astype(BF16)
            else:
                part = _dot(st.pop("w"), v_ref[0, 0, pl.ds(start, tq), :])
                acc[h] = part if acc[h] is None else acc[h] + part

        n_stages = 5
        for step in range(len(tasks) + n_stages - 1):
            for k in reversed(range(n_stages)):
                ti = step - k
                if 0 <= ti < len(tasks):
                    stage(k, state[ti], *tasks[ti])
        for h in heads:
            r_ref[h][...] = r[h]
            acc_ref[h][...] += acc[h]

    for h in heads:
        r_ref[h][...] = jnp.zeros((tq, 1), F32)
        acc_ref[h][...] = jnp.zeros((tq, V7X_LANES), F32)

    n_blocks = qi + 1
    n_full = jnp.maximum(n_blocks // SB_GROUP - 1, 0)
    diag_size = n_blocks - n_full * SB_GROUP
    for size in range(1, 2 * SB_GROUP):
        @pl.when(diag_size == size)
        def _(size=size):
            span(n_blocks - size, size, True)

    def body(s, carry):
        span((n_full - 1 - s) * SB_GROUP, SB_GROUP, False)
        return carry

    lax.fori_loop(0, n_full, body, 0)

    o_ref[0, tile_rows, :] = jnp.where(lane // HEAD_DIM == 0, acc0_ref[...], acc1_ref[...]).astype(o_ref.dtype)


def stick_breaking_attention(qkv, b, s, tq):
    d = qkv.shape[-1]
    qkv4 = qkv.reshape(3, b, s, d)
    plane = lambda p: pl.BlockSpec((1, 1, s, V7X_LANES), lambda bi, hp: (p, bi, 0, hp))
    return pl.pallas_call(
        functools.partial(_sb_kernel, tq=tq),
        grid=(b, d // V7X_LANES),
        in_specs=[plane(0), plane(1), plane(2)],
        out_specs=pl.BlockSpec((1, s, V7X_LANES), lambda bi, hp: (bi, 0, hp)),
        out_shape=jax.ShapeDtypeStruct((b, s, d), BF16),
        scratch_shapes=[pltpu.VMEM((tq, V7X_LANES), F32), pltpu.VMEM((tq, V7X_LANES), F32),
                        pltpu.VMEM((tq, 1), F32), pltpu.VMEM((tq, 1), F32),
                        pltpu.VMEM((tq, tq), BF16)],
        compiler_params=_cparams(("arbitrary", "arbitrary"), 32),
        name="stick_breaking",
    )(qkv4, qkv4, qkv4)


def _t5_bias_kernel(rb_ref, o_ref, *, tq):
    h = pl.program_id(0)
    row = lax.broadcasted_iota(jnp.int32, (tq, tq), 0)
    col = lax.broadcasted_iota(jnp.int32, (tq, tq), 1)
    max_exact = N_BUCKETS // 2
    for o in range(2):
        dist = o * tq + row - col
        n = jnp.maximum(dist, 0)
        nf = jnp.maximum(n, 1).astype(F32)
        scaled = jnp.log(nf / max_exact) / math.log(MAX_DISTANCE / max_exact) * (N_BUCKETS - max_exact)
        large = max_exact + jnp.where(scaled < 0, jnp.ceil(scaled), jnp.floor(scaled)).astype(jnp.int32)
        large = jnp.minimum(large, N_BUCKETS - 1)
        bucket = jnp.where(n < max_exact, n, large)
        bias = jnp.zeros((tq, tq), F32)
        for bkt in range(N_BUCKETS):
            bias = jnp.where(bucket == bkt, rb_ref[bkt, h], bias)
        if o == 0:
            bias = jnp.where(dist >= 0, bias, NEG_BIG)
        o_ref[0, o] = bias


def t5_bias_tiles(rel_bias, tq):
    return pl.pallas_call(
        functools.partial(_t5_bias_kernel, tq=tq),
        grid=(N_HEADS,),
        in_specs=[pl.BlockSpec(memory_space=pltpu.SMEM)],
        out_specs=pl.BlockSpec((1, 2, tq, tq), lambda h: (h, 0, 0, 0)),
        out_shape=jax.ShapeDtypeStruct((N_HEADS, 2, tq, tq), F32),
        compiler_params=_cparams(("arbitrary",), 16),
        name="t5_bias_tiles",
    )(rel_bias)


def _moba_aux_lane0(h):
    assert HEADS_PER_LANE_BLOCK == 2
    return ((h + 1) % HEADS_PER_LANE_BLOCK) * HEAD_DIM


def _moba_kernel(rb_ref, qall_ref, k_ref, v_ref, bt_ref, o_ref,
                 s_ref, qaux_ref, mx_ref, l_ref, acc_ref, *, tq, nblk):
    heads = range(HEADS_PER_LANE_BLOCK)
    s_len = nblk * tq
    n_sel = min(MOBA_TOPK, nblk - 1)
    big = jnp.asarray(-NEG_BIG, BF16).astype(F32)
    aux_lane0 = _moba_aux_lane0
    assert nblk + 2 <= HEAD_DIM

    def block_choice():
        km = jnp.concatenate(
            [jnp.mean(k_ref[0, 0, n * tq:(n + 1) * tq, :].astype(F32), axis=0, keepdims=True)
             for n in range(nblk)], axis=0)
        km_hi = km.astype(BF16)
        km_lo = (km - km_hi.astype(F32)).astype(BF16)
        q_all = qall_ref[0, 0]
        lane_all = lax.broadcasted_iota(jnp.int32, (s_len, V7X_LANES), 1)
        blk = lax.broadcasted_iota(jnp.int32, (nblk, s_len), 0)
        own = lax.broadcasted_iota(jnp.int32, (nblk, s_len), 1) // tq
        own_row = own[0:1, :]
        for h in heads:
            qm = jnp.where(lane_all // HEAD_DIM == h, q_all, jnp.zeros_like(q_all))
            gate = _dot_nt(km_hi, qm) + _dot_nt(km_lo, qm)
            gate = jnp.where(blk < own, gate, NEG_BIG)
            rows = [gate[n:n + 1, :] for n in range(nblk)]
            rank = [jnp.zeros((1, s_len), F32) for _ in range(nblk)]
            for n in range(nblk):
                for m in range(n):
                    m_wins = jnp.where(rows[m] >= rows[n], 1.0, 0.0)
                    rank[n] = rank[n] + m_wins
                    rank[m] = rank[m] + (1.0 - m_wins)
            aux = [jnp.where((rank[n] < n_sel) & (n < own_row), 0.0, -big) for n in range(nblk)]
            aux += [jnp.ones((1, s_len), F32)] * 2
            before = jnp.zeros((aux_lane0(h), s_len), F32)
            after = jnp.zeros((V7X_LANES - aux_lane0(h) - len(aux), s_len), F32)
            pieces = ([before] if before.shape[0] else []) + aux + [after]
            qaux_ref[h] = jnp.concatenate(pieces, axis=0).T.astype(BF16)

    block_choice()

    def q_tile(qi, carry):
        _moba_tile(qi, rb_ref, qall_ref, k_ref, v_ref, bt_ref, o_ref,
                   s_ref, qaux_ref, mx_ref, l_ref, acc_ref, tq=tq, nblk=nblk)
        return carry

    lax.fori_loop(0, nblk, q_tile, 0)


def _moba_tile(qi, rb_ref, qall_ref, k_ref, v_ref, bt_ref, o_ref,
               s_ref, qaux_ref, mx_ref, l_ref, acc_ref, *, tq, nblk):
    hp = pl.program_id(1)
    heads = range(HEADS_PER_LANE_BLOCK)
    aux_lane0 = _moba_aux_lane0

    tile_rows = pl.ds(pl.multiple_of(qi * tq, tq), tq)
    q = qall_ref[0, 0, tile_rows, :]
    lane = lax.broadcasted_iota(jnp.int32, (tq, V7X_LANES), 1)
    q_aug = [jnp.where(lane // HEAD_DIM == h, q, qaux_ref[h, tile_rows, :]) for h in heads]
    both = lambda x: jnp.concatenate([x, x], axis=1)
    halves = lambda x: (x[:, :V7X_LANES], x[:, V7X_LANES:])
    for h in heads:
        mx_ref[h] = jnp.full((tq, V7X_LANES), NEG_BIG, F32)

    aux_rows = 16
    lane_aux = lax.broadcasted_iota(jnp.int32, (aux_rows, V7X_LANES), 1)
    key_aux_far = []
    for h in heads:
        far = jnp.full((aux_rows, V7X_LANES), rb_ref[N_BUCKETS - 1, hp * HEADS_PER_LANE_BLOCK + h], F32)
        far_hi = far.astype(BF16).astype(F32)
        far_lo = (far - far_hi).astype(BF16).astype(F32)
        key_aux_far.append(jnp.where(lane_aux == aux_lane0(h) + nblk, far_hi,
                                     jnp.where(lane_aux == aux_lane0(h) + nblk + 1, far_lo, 0.0)))

    def key_operand(h, n, role):
        start = pl.multiple_of(n * tq, tq)
        kb = k_ref[0, 0, pl.ds(start, tq), :]
        if role == "own":
            aux = jnp.zeros((aux_rows, V7X_LANES), F32)
        else:
            base = key_aux_far[h] if role == "far" else jnp.zeros((aux_rows, V7X_LANES), F32)
            aux = jnp.where(lane_aux == aux_lane0(h) + n, 1.0, base)
        aux = jnp.concatenate([aux.astype(BF16)] * (tq // aux_rows), axis=0)
        return jnp.where(lane // HEAD_DIM == h, kb, aux)

    pipeline = _software_pipeline

    def logits_span(first, roles, last=False):
        mx = [mx_ref[h] for h in heads]

        def products(st, i, h):
            st["s"] = _dot_nt(q_aug[h], key_operand(h, first + i, roles[i]))

        def finish(st, i, h):
            start = pl.multiple_of((first + i) * tq, tq)
            s = st.pop("s")
            if roles[i] == "own":
                s = s + bt_ref[h, 0]
            elif roles[i] == "prev":
                s = s + bt_ref[h, 1]
            s_ref[h, :, pl.ds(start, tq)] = s
            s_lo, s_hi = halves(s)
            mx[h] = jnp.maximum(mx[h], jnp.maximum(s_lo, s_hi))

        pipeline([(i, h) for i in range(len(roles)) for h in heads], [products, finish])
        for h in heads:
            if last:
                mx_ref[h] = jnp.broadcast_to(jnp.max(mx[h], axis=1, keepdims=True), (tq, V7X_LANES))
                l_ref[h] = jnp.zeros((tq, V7X_LANES), F32)
                acc_ref[h] = jnp.zeros((tq, V7X_LANES), F32)
            else:
                mx_ref[h] = mx[h]

    def probs_span(first, nb, last=False):
        m = [both(mx_ref[h]) for h in heads]
        l = [l_ref[h] for h in heads]
        acc = [acc_ref[h] for h in heads]

        def exponentials(st, i, h):
            start = pl.multiple_of((first + i) * tq, tq)
            p = jnp.exp(s_ref[h, :, pl.ds(start, tq)] - m[h])
            p_lo, p_hi = halves(p)
            l[h] = l[h] + (p_lo + p_hi)
            st["p"] = p.astype(BF16)

        def accumulate(st, i, h):
            start = pl.multiple_of((first + i) * tq, tq)
            acc[h] = acc[h] + _dot(st.pop("p"), v_ref[0, 0, pl.ds(start, tq), :])

        pipeline([(i, h) for i in range(nb) for h in heads], [exponentials, accumulate])
        if last:
            out = [acc[h] / jnp.sum(l[h], axis=1, keepdims=True) for h in heads]
            o_ref[0, tile_rows, :] = jnp.where(lane // HEAD_DIM == 0, out[0], out[1]).astype(o_ref.dtype)
        else:
            for h in heads:
                l_ref[h] = l[h]
                acc_ref[h] = acc[h]

    group = MOBA_GROUP
    own_group = qi // group
    in_group = qi % group

    def far_groups(g, carry):
        logits_span(g * group, ["far"] * group)
        return carry

    lax.fori_loop(0, own_group - 1, far_groups, 0)

    @pl.when((own_group >= 1) & (in_group == 0))
    def _():
        logits_span((own_group - 1) * group, ["far"] * (group - 1) + ["prev"])

    @pl.when((own_group >= 1) & (in_group != 0))
    def _():
        logits_span((own_group - 1) * group, ["far"] * group)

    for size in range(1, group + 1):
        @pl.when(in_group == size - 1)
        def _(size=size):
            logits_span(own_group * group, (["far"] * group + ["prev", "own"])[-size:]
                        if size >= 2 else ["own"], last=True)

    def prob_groups(g, carry):
        probs_span(g * group, group)
        return carry

    lax.fori_loop(0, own_group, prob_groups, 0)
    for size in range(1, group + 1):
        @pl.when(in_group == size - 1)
        def _(size=size):
            probs_span(own_group * group, size, last=True)


def moba_attention(qkv, rel_bias, b, s):
    d = qkv.shape[-1]
    tq = MOBA_BLOCK
    assert s % tq == 0 and tq >= 2 * MAX_DISTANCE
    nblk = s // tq
    qkv4 = qkv.reshape(3, b, s, d)
    tiles = t5_bias_tiles(rel_bias, tq)
    return pl.pallas_call(
        functools.partial(_moba_kernel, tq=tq, nblk=nblk),
        grid=(b, d // V7X_LANES),
        in_specs=[pl.BlockSpec(memory_space=pltpu.SMEM),
                  pl.BlockSpec((1, 1, s, V7X_LANES), lambda bi, hp: (0, bi, 0, hp)),
                  pl.BlockSpec((1, 1, s, V7X_LANES), lambda bi, hp: (1, bi, 0, hp)),
                  pl.BlockSpec((1, 1, s, V7X_LANES), lambda bi, hp: (2, bi, 0, hp)),
                  pl.BlockSpec((HEADS_PER_LANE_BLOCK, 2, tq, tq), lambda bi, hp: (hp, 0, 0, 0))],
        out_specs=pl.BlockSpec((1, s, V7X_LANES), lambda bi, hp: (bi, 0, hp)),
        out_shape=jax.ShapeDtypeStruct((b, s, d), BF16),
        scratch_shapes=[pltpu.VMEM((HEADS_PER_LANE_BLOCK, tq, s), F32),
                        pltpu.VMEM((HEADS_PER_LANE_BLOCK, s, V7X_LANES), BF16),
                        pltpu.VMEM((HEADS_PER_LANE_BLOCK, tq, V7X_LANES), F32),
                        pltpu.VMEM((HEADS_PER_LANE_BLOCK, tq, V7X_LANES), F32),
                        pltpu.VMEM((HEADS_PER_LANE_BLOCK, tq, V7X_LANES), F32)],
        compiler_params=_cparams(("arbitrary", "arbitrary"), 32),
        name="moba",
    )(rel_bias, qkv4, qkv4, qkv4, tiles)


def _whole(shape):
    return pl.BlockSpec(shape, lambda i: (0,) * len(shape), pipeline_mode=pl.Buffered(1))


def _ffn_kernel(o_ref, wo_ref, h_ref, g_ref, w1_ref, w3_ref, w2_ref, out_ref, *, chunks):
    x = h_ref[...] + _dot(o_ref[...], wo_ref[...])
    hn = _rmsnorm_f32(x, g_ref[...]).astype(BF16)
    acc = x
    for c0, c1 in chunks:
        a = _dot(hn, w1_ref[:, c0:c1])
        b = _dot(hn, w3_ref[:, c0:c1])
        acc = acc + _dot((_silu(a) * b).astype(BF16), w2_ref[c0:c1, :])
    out_ref[...] = acc


def oproj_dense_ffn(o, w_o, h, g, w1, w3, w2, tm, chunk):
    t, d = h.shape
    f = w1.shape[1]
    chunks = tuple((c, min(c + chunk, f)) for c in range(0, f, chunk))
    return pl.pallas_call(
        functools.partial(_ffn_kernel, chunks=chunks),
        grid=(t // tm,),
        in_specs=[pl.BlockSpec((tm, d), lambda i: (i, 0)),
                  _whole((d, d)),
                  pl.BlockSpec((tm, d), lambda i: (i, 0)),
                  pl.BlockSpec((1, d), lambda i: (0, 0)),
                  _whole((d, f)), _whole((d, f)), _whole((f, d))],
        out_specs=pl.BlockSpec((tm, d), lambda i: (i, 0)),
        out_shape=jax.ShapeDtypeStruct((t, d), F32),
        compiler_params=_cparams(("arbitrary",), 56),
        name="oproj_dense_ffn",
    )(o, w_o, h, g.reshape(1, d), w1, w3, w2)


def _row_tile_chunks(d):
    assert d % V7X_LANES == 0
    return d // V7X_LANES


def _store_row_tiles(ref, x, row0=0):
    n, d = x.shape
    c = _row_tile_chunks(d)
    for k in range(c):
        ref[pl.ds(row0 * c + k, n, stride=c), :] = x[:, k * V7X_LANES:(k + 1) * V7X_LANES]


def _load_row_tiles(ref, n, d):
    c = _row_tile_chunks(d)
    return [ref[pl.ds(k, n, stride=c), :] for k in range(c)]


COL_E0, COL_E1, COL_W0, COL_W1, COL_R0, COL_R1 = range(6)


def _router_kernel(o_ref, wo_ref, h_ref, g_ref, wr_ref, h1_ref, hn_ref, slab_ref, cnt_ref, carry_ref, *, tm):
    i = pl.program_id(0)

    @pl.when(i == 0)
    def _():
        carry_ref[...] = jnp.zeros_like(carry_ref)

    rs = min(ROUTER_ROWS, tm)
    assert tm % rs == 0
    g = g_ref[...]
    w = wr_ref[...]
    w_hi = w.astype(BF16)
    w_lo = (w - w_hi.astype(F32)).astype(BF16)
    lane = lax.broadcasted_iota(jnp.int32, (rs, V7X_LANES), 1)
    earlier = (lax.broadcasted_iota(jnp.int32, (rs, rs), 1)
               < lax.broadcasted_iota(jnp.int32, (rs, rs), 0)).astype(BF16)
    neg_inf = jnp.float32(-jnp.inf)
    count = [carry_ref[0:1, :]]

    def project(st, r):
        rows = pl.ds(r * rs, rs)
        h1 = h_ref[rows, :] + _dot(o_ref[rows, :], wo_ref[...])
        h1_ref[rows, :] = h1
        st["h1"] = h1

    def normalise(st, r):
        hn = _rmsnorm_f32(st.pop("h1"), g)
        _store_row_tiles(hn_ref, hn, r * rs)
        st["hi"] = hn.astype(BF16)
        st["lo"] = (hn - st["hi"].astype(F32)).astype(BF16)

    def logits(st, r):
        hi, lo = st.pop("hi"), st.pop("lo")
        st["logits"] = _dot(hi, w_hi) + _dot(hi, w_lo) + _dot(lo, w_hi)

    def top2(st, r):
        lg = jnp.where(lane < N_EXPERTS, st.pop("logits"), neg_inf)
        rank_of = jnp.zeros((rs, V7X_LANES), F32)
        for s in range(1, N_EXPERTS):
            lower = pltpu.roll(lg, s, axis=1)
            higher = pltpu.roll(lg, V7X_LANES - s, axis=1)
            rank_of = rank_of + jnp.where(lower >= lg, 1.0, 0.0) + jnp.where(higher > lg, 1.0, 0.0)
        first = (rank_of == 0.0) & (lane < N_EXPERTS)
        second = (rank_of == 1.0) & (lane < N_EXPERTS)
        pick = lambda sel, x: jnp.sum(jnp.where(sel, x, 0.0), axis=1, keepdims=True)
        lane_f = lane.astype(F32)
        m0, m1 = pick(first, lg), pick(second, lg)
        i0, i1 = pick(first, lane_f).astype(jnp.int32), pick(second, lane_f).astype(jnp.int32)
        e = jnp.exp(m1 - m0)
        st.update(i0=i0, i1=i1, w0=1.0 / (1.0 + e), w1=e / (1.0 + e),
                  hot=((lane == i0) | (lane == i1)).astype(BF16))

    def rank(st, r):
        hot = st.pop("hot")
        before = _dot(earlier, hot) + count[0]
        i0, i1 = st.pop("i0"), st.pop("i1")
        r0 = jnp.sum(jnp.where(lane == i0, before, 0.0), axis=1, keepdims=True)
        r1 = jnp.sum(jnp.where(lane == i1, before, 0.0), axis=1, keepdims=True)
        count[0] = count[0] + jnp.sum(hot.astype(F32), axis=0, keepdims=True)
        slab = jnp.zeros((rs, V7X_LANES), F32)
        for c, val in ((COL_E0, i0.astype(F32)), (COL_E1, i1.astype(F32)), (COL_W0, st.pop("w0")),
                       (COL_W1, st.pop("w1")), (COL_R0, r0), (COL_R1, r1)):
            slab = jnp.where(lane == c, val, slab)
        slab_ref[pl.ds(r * rs, rs), :] = slab

    _software_pipeline([(r,) for r in range(tm // rs)], [project, normalise, logits, top2, rank])
    carry_ref[0:1, :] = count[0]
    cnt_ref[...] = carry_ref[...]


def oproj_moe_router(o, w_o, h, g, w_router, tm):
    t, d = h.shape
    wr = jnp.zeros((d, V7X_LANES), F32).at[:, :N_EXPERTS].set(w_router)
    return pl.pallas_call(
        functools.partial(_router_kernel, tm=tm),
        grid=(t // tm,),
        in_specs=[pl.BlockSpec((tm, d), lambda i: (i, 0)),
                  _whole((d, d)),
                  pl.BlockSpec((tm, d), lambda i: (i, 0)),
                  pl.BlockSpec((1, d), lambda i: (0, 0)),
                  pl.BlockSpec((d, V7X_LANES), lambda i: (0, 0))],
        out_specs=[pl.BlockSpec((tm, d), lambda i: (i, 0)),
                   pl.BlockSpec((tm * d // V7X_LANES, V7X_LANES), lambda i: (i, 0)),
                   pl.BlockSpec((tm, V7X_LANES), lambda i: (i, 0)),
                   pl.BlockSpec((8, V7X_LANES), lambda i: (0, 0))],
        out_shape=[jax.ShapeDtypeStruct((t, d), F32),
                   jax.ShapeDtypeStruct((t * d // V7X_LANES, V7X_LANES), F32),
                   jax.ShapeDtypeStruct((t, V7X_LANES), F32),
                   jax.ShapeDtypeStruct((8, V7X_LANES), F32)],
        scratch_shapes=[pltpu.VMEM((8, V7X_LANES), F32)],
        compiler_params=_cparams(("arbitrary",), 40),
        name="oproj_moe_router",
    )(o, w_o, h, g.reshape(1, d), wr)


SCALAR_UNROLL = 16


def _sorted_list_kernel(beg_ref, end_ref, d0_ref, d1_ref, out_ref, *, chunk, t, tm, rows):
    i = pl.program_id(0)

    @pl.when(i == 0)
    def _():
        for e in range(N_EXPERTS + 1):
            beg, end = beg_ref[e], end_ref[e]
            n = end - beg

            def fill_group(g, carry, end=end):
                p0 = end - SCALAR_UNROLL * (g + 1)
                r0 = lax.rem(p0, tm)
                for k in range(SCALAR_UNROLL):
                    out_ref[p0 + k] = 2 * t + r0 + k
                return carry
            lax.fori_loop(0, n // SCALAR_UNROLL, fill_group, 0)

            def fill_one(p, carry):
                out_ref[p] = 2 * t + lax.rem(p, tm)
                return carry
            lax.fori_loop(beg, beg + lax.rem(n, SCALAR_UNROLL), fill_one, 0)

    base = i * chunk

    def body(g, carry):
        for k in range(SCALAR_UNROLL):
            j = g * SCALAR_UNROLL + k
            out_ref[d0_ref[0, 0, j]] = base + j
            out_ref[d1_ref[0, 0, j]] = t + base + j
        return carry
    lax.fori_loop(0, chunk // SCALAR_UNROLL, body, 0)


def sorted_assignment_list(dest0, dest1, pad_begin, pad_end, t, tm, rows):
    chunk = min(t, 2048)
    assert t % chunk == 0 and chunk % SCALAR_UNROLL == 0 and tm % SCALAR_UNROLL == 0
    blocked = lambda a: a.reshape(t // chunk, 1, chunk)
    dest_spec = pl.BlockSpec((1, 1, chunk), lambda i, beg, end: (i, 0, 0), memory_space=pltpu.SMEM)
    grid_spec = pltpu.PrefetchScalarGridSpec(
        num_scalar_prefetch=2,
        grid=(t // chunk,),
        in_specs=[dest_spec, dest_spec],
        out_specs=pl.BlockSpec(memory_space=pltpu.SMEM),
    )
    return pl.pallas_call(
        functools.partial(_sorted_list_kernel, chunk=chunk, t=t, tm=tm, rows=rows),
        grid_spec=grid_spec,
        out_shape=jax.ShapeDtypeStruct((rows,), jnp.int32),
        compiler_params=_cparams(("arbitrary",), 16),
        name="sorted_assignment_list",
    )(pad_begin, pad_end, blocked(dest0), blocked(dest1))


def _moe_kernel(te_ref, nu_ref, tokc_ref, tokn_ref, dst_ref, hn_hbm, w1_ref, w3_ref, w2_ref, y_hbm,
                xbuf, xb16, acc_ref, ybuf, gsem, ssem, *, tm, nf, nt, n_real):
    i = pl.program_id(0)
    j = pl.program_id(1)
    nu = nu_ref[0]
    slot = i % 2
    d = acc_ref.shape[1]
    c = _row_tile_chunks(d)

    def row_loop(start_row_copy):
        for r in range(tm):
            start_row_copy(r)

    def row(r):
        return pl.ds(r * c, c) if isinstance(r, int) else pl.ds(pl.multiple_of(r * c, c), c)

    def start_gather(tok_ref, s):
        row_loop(lambda r: pltpu.make_async_copy(
            hn_hbm.at[row(tok_ref[0, 0, r]), :], xbuf.at[s, row(r), :], gsem.at[s]).start())

    def wait_gather(s):
        pltpu.make_async_copy(hn_hbm.at[pl.ds(0, tm * c), :], xbuf.at[s], gsem.at[s]).wait()

    def start_scatter():
        row_loop(lambda r: pltpu.make_async_copy(
            ybuf.at[row(r), :], y_hbm.at[row(dst_ref[0, 0, r]), :], ssem.at[0]).start())

    def spare_rows_copy():
        return pltpu.make_async_copy(ybuf, y_hbm.at[pl.ds(n_real * c, tm * c), :], ssem.at[0])

    def wait_scatter():
        spare_rows_copy().wait()

    @pl.when((j == 0) & (i == 0))
    def _():
        ybuf[...] = jnp.zeros_like(ybuf)
        spare_rows_copy().start()
        spare_rows_copy().wait()

        @pl.when(nu > 0)
        def _():
            start_gather(tokc_ref, 0)

    @pl.when((j == 0) & (i < nu))
    def _():
        wait_gather(slot)
        for k, chunk in enumerate(_load_row_tiles(xbuf.at[slot], tm, d)):
            xb16[:, k * V7X_LANES:(k + 1) * V7X_LANES] = chunk.astype(BF16)

    @pl.when((j == 0) & (i + 1 < nu))
    def _():
        start_gather(tokn_ref, 1 - slot)

    @pl.when(i < nu)
    def _():
        x = xb16[...]
        a = _dot(x, w1_ref[0])
        b = _dot(x, w3_ref[0])
        part = _dot((_silu(a) * b).astype(BF16), w2_ref[0])

        @pl.when(j == 0)
        def _():
            acc_ref[...] = part

        @pl.when(j > 0)
        def _():
            acc_ref[...] += part

    last = j == nf - 1

    @pl.when(last & (i >= 1) & (i - 1 < nu))
    def _():
        wait_scatter()

    @pl.when(last & (i < nu))
    def _():
        _store_row_tiles(ybuf, acc_ref[...])
        start_scatter()

    @pl.when(last & (i == nt - 1) & (i < nu))
    def _():
        wait_scatter()


def moe_experts(hn, slab, counts, w1, w3, w2, tm, tf):
    t = slab.shape[0]
    d = w1.shape[1]
    c = _row_tile_chunks(d)
    f = w1.shape[2]
    nf = f // tf
    nt = (2 * t) // tm + N_EXPERTS
    rows = nt * tm

    e0 = slab[:, COL_E0].astype(jnp.int32)
    e1 = slab[:, COL_E1].astype(jnp.int32)
    r0 = slab[:, COL_R0].astype(jnp.int32)
    r1 = slab[:, COL_R1].astype(jnp.int32)
    cnt = counts[0, :N_EXPERTS].astype(jnp.int32)
    tiles = (cnt + tm - 1) // tm
    tile_end = jnp.cumsum(tiles)
    tile_start = tile_end - tiles
    nu = tile_end[-1:]
    offs = tile_start * tm
    dest0 = offs[e0] + r0
    dest1 = offs[e1] + r1
    tile_expert = jnp.minimum(
        jnp.sum(jnp.arange(nt, dtype=jnp.int32)[:, None] >= tile_end[None, :], axis=1), N_EXPERTS - 1
    ).astype(jnp.int32)
    pad_begin = jnp.concatenate([offs + cnt, nu * tm]).astype(jnp.int32)
    pad_end = jnp.concatenate([tile_end * tm, jnp.full((1,), rows, jnp.int32)]).astype(jnp.int32)
    dst_sorted = sorted_assignment_list(dest0, dest1, pad_begin, pad_end, t, tm, rows)
    tok_sorted = jnp.where(dst_sorted >= 2 * t, 0, jnp.where(dst_sorted >= t, dst_sorted - t, dst_sorted))
    tok3 = tok_sorted.reshape(nt, 1, tm)
    dst3 = dst_sorted.reshape(nt, 1, tm)

    def w_in(shape, which):
        def index_map(i, j, te, nu_):
            ii = jnp.minimum(i, nu_[0] - 1)
            jj = jnp.where(i < nu_[0], j, nf - 1)
            return (te[ii], 0, jj) if which == "up" else (te[ii], jj, 0)
        return pl.BlockSpec(shape, index_map)

    smem_tile = lambda f_: pl.BlockSpec((1, 1, tm), f_, memory_space=pltpu.SMEM)
    grid_spec = pltpu.PrefetchScalarGridSpec(
        num_scalar_prefetch=2,
        grid=(nt, nf),
        in_specs=[smem_tile(lambda i, j, te, nu_: (i, 0, 0)),
                  smem_tile(lambda i, j, te, nu_: (jnp.minimum(i + 1, nt - 1), 0, 0)),
                  smem_tile(lambda i, j, te, nu_: (i, 0, 0)),
                  pl.BlockSpec(memory_space=pl.ANY),
                  w_in((1, d, tf), "up"), w_in((1, d, tf), "up"), w_in((1, tf, d), "down")],
        out_specs=pl.BlockSpec(memory_space=pl.ANY),
        scratch_shapes=[pltpu.VMEM((2, tm * c, V7X_LANES), F32),
                        pltpu.VMEM((tm, d), BF16),
                        pltpu.VMEM((tm, d), F32),
                        pltpu.VMEM((tm * c, V7X_LANES), F32),
                        pltpu.SemaphoreType.DMA((2,)),
                        pltpu.SemaphoreType.DMA((1,))],
    )
    return pl.pallas_call(
        functools.partial(_moe_kernel, tm=tm, nf=nf, nt=nt, n_real=2 * t),
        grid_spec=grid_spec,
        out_shape=jax.ShapeDtypeStruct(((2 * t + tm) * c, V7X_LANES), F32),
        compiler_params=_cparams(("arbitrary", "arbitrary"), 56),
        name="moe_experts",
    )(tile_expert, nu, tok3, tok3, dst3, hn, w1, w3, w2)


def _combine_kernel(h_ref, y0_ref, y1_ref, slab_ref, g_ref, out_ref, *, final):
    slab = slab_ref[...]
    tm, d = h_ref.shape
    y0 = jnp.concatenate(_load_row_tiles(y0_ref, tm, d), axis=1)
    y1 = jnp.concatenate(_load_row_tiles(y1_ref, tm, d), axis=1)
    x = h_ref[...] + slab[:, COL_W0:COL_W0 + 1] * y0 + slab[:, COL_W1:COL_W1 + 1] * y1
    out_ref[...] = _rmsnorm_f32(x, g_ref[...]) if final else x


def moe_combine(h, y, slab, g, tm, final):
    t, d = h.shape
    nb = t // tm
    c = _row_tile_chunks(d)
    return pl.pallas_call(
        functools.partial(_combine_kernel, final=final),
        grid=(nb,),
        in_specs=[pl.BlockSpec((tm, d), lambda i: (i, 0)),
                  pl.BlockSpec((tm * c, V7X_LANES), lambda i: (i, 0)),
                  pl.BlockSpec((tm * c, V7X_LANES), lambda i: (i + nb, 0)),
                  pl.BlockSpec((tm, V7X_LANES), lambda i: (i, 0)),
                  pl.BlockSpec((1, d), lambda i: (0, 0))],
        out_specs=pl.BlockSpec((tm, d), lambda i: (i, 0)),
        out_shape=jax.ShapeDtypeStruct((t, d), F32),
        compiler_params=_cparams(("arbitrary",), 40),
        name="moe_combine",
    )(h, y, y, slab, g.reshape(1, d))


def _norm_kernel(h_ref, g_ref, out_ref):
    out_ref[...] = _rmsnorm_f32(h_ref[...], g_ref[...])


def final_norm_only(h, g, tm):
    t, d = h.shape
    return pl.pallas_call(
        _norm_kernel,
        grid=(t // tm,),
        in_specs=[pl.BlockSpec((tm, d), lambda i: (i, 0)), pl.BlockSpec((1, d), lambda i: (0, 0))],
        out_specs=pl.BlockSpec((tm, d), lambda i: (i, 0)),
        out_shape=jax.ShapeDtypeStruct((t, d), F32),
        compiler_params=_cparams(("arbitrary",), 40),
        name="final_norm",
    )(h, g.reshape(1, d))


V7X_MXU_DIM = 256


def _tile_sizes(t, f_expert):
    def rows(want):
        tm = min(want, t)
        assert t % tm == 0
        return tm
    half = f_expert // 2
    return dict(qkv=rows(1024), tokens=rows(512), ffn_chunk=1024,
                expert_chunk=half if half % V7X_MXU_DIM == 0 else f_expert)


def kernel(x, w_qkv, w_o, mixer_norm, ffn_norm, rel_bias, w1, w3, w2, router, e_w1, e_w3, e_w2, final_norm):
    b, s, d = x.shape
    assert d == N_HEADS * HEAD_DIM and s % MOBA_BLOCK == 0
    t = b * s
    depth = w_qkv.shape[0]
    h = x.reshape(t, d)
    tiles = _tile_sizes(t, e_w1.shape[-1])
    tm_big, tm_mid = tiles["qkv"], tiles["tokens"]
    normed = False
    for i in range(depth):
        qkv = qkv_proj(h, mixer_norm[i], w_qkv[i].astype(BF16), tm_big)
        if i % 2 == 0:
            o = stick_breaking_attention(qkv, b, s, MOBA_BLOCK)
        else:
            o = moba_attention(qkv, rel_bias, b, s)
        o = o.reshape(t, d)
        wo = w_o[i].astype(BF16)
        jj = i // 2
        if i % 2 == 0:
            h = oproj_dense_ffn(o, wo, h, ffn_norm[i], w1[jj].astype(BF16), w3[jj].astype(BF16),
                                w2[jj].astype(BF16), tm_mid, tiles["ffn_chunk"])
        else:
            h, hn, slab, counts = oproj_moe_router(o, wo, h, ffn_norm[i], router[jj], tm_mid)
            y = moe_experts(hn, slab, counts, e_w1[jj].astype(BF16), e_w3[jj].astype(BF16),
                            e_w2[jj].astype(BF16), tm_mid, tiles["expert_chunk"])
            last = i == depth - 1
            h = moe_combine(h, y, slab, final_norm if last else ffn_norm[i], tm_mid, last)
            normed = last
    if not normed:
        h = final_norm_only(h, final_norm, tm_mid)
    return h.reshape(b, s, d)
```

```python
import functools
import math

import jax
import jax.numpy as jnp
from jax import lax
from jax.experimental import pallas as pl
from jax.experimental.pallas import tpu as pltpu

N_HEADS = 16
HEAD_DIM = 64
MOBA_BLOCK = 256
MOBA_TOPK = 3
N_BUCKETS = 32
MAX_DISTANCE = 128
N_EXPERTS = 8
RMS_EPS = 1e-6
NEG_BIG = -1e30

V7X_LANES = 128
SB_GROUP = 4
MOBA_GROUP = 4
ROUTER_ROWS = 128
V7X_VMEM_BYTES = 64 * 1024 * 1024
HEADS_PER_LANE_BLOCK = V7X_LANES // HEAD_DIM

F32 = jnp.float32
BF16 = jnp.bfloat16


def _cparams(semantics, vmem_mb):
    assert vmem_mb * 1024 * 1024 < V7X_VMEM_BYTES
    return pltpu.CompilerParams(dimension_semantics=semantics,
                                vmem_limit_bytes=vmem_mb * 1024 * 1024)


def _rmsnorm_f32(x, g):
    return x * lax.rsqrt(jnp.mean(x * x, axis=-1, keepdims=True) + RMS_EPS) * g


def _silu(a):
    return a * (1.0 / (1.0 + jnp.exp(-a)))


def _dot(a, b):
    return jnp.dot(a, b, preferred_element_type=F32)


def _software_pipeline(tasks, stages):
    state = [dict() for _ in tasks]
    for step in range(len(tasks) + len(stages) - 1):
        for k in reversed(range(len(stages))):
            t = step - k
            if 0 <= t < len(tasks):
                stages[k](state[t], *tasks[t])


def _dot_nt(a, b):
    return lax.dot_general(a, b, (((1,), (1,)), ((), ())), preferred_element_type=F32)


def _qkv_kernel(x_ref, g_ref, w_ref, o_ref, hn_ref):
    j = pl.program_id(1)

    @pl.when(j == 0)
    def _():
        hn_ref[...] = _rmsnorm_f32(x_ref[...], g_ref[...]).astype(BF16)

    scale = jnp.where(j == 0, HEAD_DIM ** -0.5, 1.0)
    o_ref[0] = (_dot(hn_ref[...], w_ref[...]) * scale).astype(o_ref.dtype)


def qkv_proj(h, g, w_bf16, tm):
    t, d = h.shape
    return pl.pallas_call(
        _qkv_kernel,
        grid=(t // tm, 3),
        in_specs=[pl.BlockSpec((tm, d), lambda i, j: (i, 0)),
                  pl.BlockSpec((1, d), lambda i, j: (0, 0)),
                  pl.BlockSpec((d, d), lambda i, j: (0, j))],
        out_specs=pl.BlockSpec((1, tm, d), lambda i, j: (j, i, 0)),
        out_shape=jax.ShapeDtypeStruct((3, t, d), BF16),
        scratch_shapes=[pltpu.VMEM((tm, d), BF16)],
        compiler_params=_cparams(("arbitrary", "arbitrary"), 40),
        name="qkv_proj",
    )(h, g.reshape(1, d), w_bf16)


def _sb_kernel(q_ref, k_ref, v_ref, o_ref, acc0_ref, acc1_ref, r0_ref, r1_ref, u_ref, *, tq):
    row = lax.broadcasted_iota(jnp.int32, (tq, tq), 0)
    col = lax.broadcasted_iota(jnp.int32, (tq, tq), 1)
    u_ref[...] = -(row > col).astype(BF16)

    def q_tile(qi, carry):
        _sb_tile(qi, q_ref, k_ref, v_ref, o_ref, acc0_ref, acc1_ref, r0_ref, r1_ref, u_ref, tq=tq)
        return carry

    lax.fori_loop(0, q_ref.shape[2] // tq, q_tile, 0)


def _sb_tile(qi, q_ref, k_ref, v_ref, o_ref, acc0_ref, acc1_ref, r0_ref, r1_ref, u_ref, *, tq):
    acc_ref = (acc0_ref, acc1_ref)
    r_ref = (r0_ref, r1_ref)
    row = lax.broadcasted_iota(jnp.int32, (tq, tq), 0)
    col = lax.broadcasted_iota(jnp.int32, (tq, tq), 1)
    tile_rows = pl.ds(pl.multiple_of(qi * tq, tq), tq)
    q = q_ref[0, 0, tile_rows, :]
    lane = lax.broadcasted_iota(jnp.int32, (tq, V7X_LANES), 1)
    past = col < row
    heads = range(HEADS_PER_LANE_BLOCK)
    qms = [jnp.where(lane // HEAD_DIM == h, q, jnp.zeros_like(q)) for h in heads]

    def span(first, nb, diagonal):
        tasks = [(i, h) for i in reversed(range(nb)) for h in heads]
        state = [dict() for _ in tasks]
        r = [r_ref[h][...] for h in heads]
        acc = [None for h in heads]

        def stage(k, st, i, h):
            on_diagonal = diagonal and i == nb - 1
            start = pl.multiple_of((first + i) * tq, tq)
            if k == 0:
                st["z"] = _dot_nt(qms[h], k_ref[0, 0, pl.ds(start, tq), :])
            elif k == 1:
                z = st.pop("z")
                zb = z.astype(BF16)
                t = jnp.log(1 + jnp.exp(-jnp.abs(zb)))
                sp = jnp.maximum(zb, 0) + t
                if on_diagonal:
                    sp = jnp.where(past, sp, jnp.zeros_like(sp))
                st["sp"] = sp
                st["log_sig"] = jnp.minimum(z, 0.0) - t.astype(F32)
            elif k == 2:
                st["after"] = _dot(st["sp"], u_ref[...])
            elif k == 3:
                tot = st.pop("after") + r[h]
                w = jnp.exp(st.pop("log_sig") + tot)
                if on_diagonal:
                    w = jnp.where(past, w, 0.0)
                r[h] = tot[:, 0:1] - st.pop("sp")[:, 0:1].astype(F32)
                st["w"] = w.astype(BF16)
            else:
                part = _dot(st.pop("w"), v_ref[0, 0, pl.ds(start, tq), :])
                acc[h] = part if acc[h] is None else acc[h] + part

        n_stages = 5
        for step in range(len(tasks) + n_stages - 1):
            for k in reversed(range(n_stages)):
                ti = step - k
                if 0 <= ti < len(tasks):
                    stage(k, state[ti], *tasks[ti])
        for h in heads:
            r_ref[h][...] = r[h]
            acc_ref[h][...] += acc[h]

    for h in heads:
        r_ref[h][...] = jnp.zeros((tq, 1), F32)
        acc_ref[h][...] = jnp.zeros((tq, V7X_LANES), F32)

    n_blocks = qi + 1
    n_full = jnp.maximum(n_blocks // SB_GROUP - 1, 0)
    diag_size = n_blocks - n_full * SB_GROUP
    for size in range(1, 2 * SB_GROUP):
        @pl.when(diag_size == size)
        def _(size=size):
            span(n_blocks - size, size, True)

    def body(s, carry):
        span((n_full - 1 - s) * SB_GROUP, SB_GROUP, False)
        return carry

    lax.fori_loop(0, n_full, body, 0)

    o_ref[0, tile_rows, :] = jnp.where(lane // HEAD_DIM == 0, acc0_ref[...], acc1_ref[...]).astype(o_ref.dtype)


def stick_breaking_attention(qkv, b, s, tq):
    d = qkv.shape[-1]
    qkv4 = qkv.reshape(3, b, s, d)
    plane = lambda p: pl.BlockSpec((1, 1, s, V7X_LANES), lambda bi, hp: (p, bi, 0, hp))
    return pl.pallas_call(
        functools.partial(_sb_kernel, tq=tq),
        grid=(b, d // V7X_LANES),
        in_specs=[plane(0), plane(1), plane(2)],
        out_specs=pl.BlockSpec((1, s, V7X_LANES), lambda bi, hp: (bi, 0, hp)),
        out_shape=jax.ShapeDtypeStruct((b, s, d), BF16),
        scratch_shapes=[pltpu.VMEM((tq, V7X_LANES), F32), pltpu.VMEM((tq, V7X_LANES), F32),
                        pltpu.VMEM((tq, 1), F32), pltpu.VMEM((tq, 1), F32),
                        pltpu.VMEM((tq, tq), BF16)],
        compiler_params=_cparams(("arbitrary", "arbitrary"), 32),
        name="stick_breaking",
    )(qkv4, qkv4, qkv4)


def _t5_bias_kernel(rb_ref, o_ref, *, tq):
    h = pl.program_id(0)
    row = lax.broadcasted_iota(jnp.int32, (tq, tq), 0)
    col = lax.broadcasted_iota(jnp.int32, (tq, tq), 1)
    max_exact = N_BUCKETS // 2
    for o in range(2):
        dist = o * tq + row - col
        n = jnp.maximum(dist, 0)
        nf = jnp.maximum(n, 1).astype(F32)
        scaled = jnp.log(nf / max_exact) / math.log(MAX_DISTANCE / max_exact) * (N_BUCKETS - max_exact)
        large = max_exact + jnp.where(scaled < 0, jnp.ceil(scaled), jnp.floor(scaled)).astype(jnp.int32)
        large = jnp.minimum(large, N_BUCKETS - 1)
        bucket = jnp.where(n < max_exact, n, large)
        bias = jnp.zeros((tq, tq), F32)
        for bkt in range(N_BUCKETS):
            bias = jnp.where(bucket == bkt, rb_ref[bkt, h], bias)
        if o == 0:
            bias = jnp.where(dist >= 0, bias, NEG_BIG)
        o_ref[0, o] = bias


def t5_bias_tiles(rel_bias, tq):
    return pl.pallas_call(
        functools.partial(_t5_bias_kernel, tq=tq),
        grid=(N_HEADS,),
        in_specs=[pl.BlockSpec(memory_space=pltpu.SMEM)],
        out_specs=pl.BlockSpec((1, 2, tq, tq), lambda h: (h, 0, 0, 0)),
        out_shape=jax.ShapeDtypeStruct((N_HEADS, 2, tq, tq), F32),
        compiler_params=_cparams(("arbitrary",), 16),
        name="t5_bias_tiles",
    )(rel_bias)


def _moba_aux_lane0(h):
    assert HEADS_PER_LANE_BLOCK == 2
    return ((h + 1) % HEADS_PER_LANE_BLOCK) * HEAD_DIM


def _moba_kernel(rb_ref, qall_ref, k_ref, v_ref, bt_ref, o_ref,
                 s_ref, qaux_ref, mx_ref, l_ref, acc_ref, *, tq, nblk):
    heads = range(HEADS_PER_LANE_BLOCK)
    s_len = nblk * tq
    n_sel = min(MOBA_TOPK, nblk - 1)
    big = jnp.asarray(-NEG_BIG, BF16).astype(F32)
    aux_lane0 = _moba_aux_lane0
    assert nblk + 2 <= HEAD_DIM

    def block_choice():
        km = jnp.concatenate(
            [jnp.mean(k_ref[0, 0, n * tq:(n + 1) * tq, :].astype(F32), axis=0, keepdims=True)
             for n in range(nblk)], axis=0)
        km_hi = km.astype(BF16)
        km_lo = (km - km_hi.astype(F32)).astype(BF16)
        q_all = qall_ref[0, 0]
        lane_all = lax.broadcasted_iota(jnp.int32, (s_len, V7X_LANES), 1)
        blk = lax.broadcasted_iota(jnp.int32, (nblk, s_len), 0)
        own = lax.broadcasted_iota(jnp.int32, (nblk, s_len), 1) // tq
        own_row = own[0:1, :]
        for h in heads:
            qm = jnp.where(lane_all // HEAD_DIM == h, q_all, jnp.zeros_like(q_all))
            gate = _dot_nt(km_hi, qm) + _dot_nt(km_lo, qm)
            gate = jnp.where(blk < own, gate, NEG_BIG)
            rows = [gate[n:n + 1, :] for n in range(nblk)]
            rank = [jnp.zeros((1, s_len), F32) for _ in range(nblk)]
            for n in range(nblk):
                for m in range(n):
                    m_wins = jnp.where(rows[m] >= rows[n], 1.0, 0.0)
                    rank[n] = rank[n] + m_wins
                    rank[m] = rank[m] + (1.0 - m_wins)
            aux = [jnp.where((rank[n] < n_sel) & (n < own_row), 0.0, -big) for n in range(nblk)]
            aux += [jnp.ones((1, s_len), F32)] * 2
            before = jnp.zeros((aux_lane0(h), s_len), F32)
            after = jnp.zeros((V7X_LANES - aux_lane0(h) - len(aux), s_len), F32)
            pieces = ([before] if before.shape[0] else []) + aux + [after]
            qaux_ref[h] = jnp.concatenate(pieces, axis=0).T.astype(BF16)

    block_choice()

    def q_tile(qi, carry):
        _moba_tile(qi, rb_ref, qall_ref, k_ref, v_ref, bt_ref, o_ref,
                   s_ref, qaux_ref, mx_ref, l_ref, acc_ref, tq=tq, nblk=nblk)
        return carry

    lax.fori_loop(0, nblk, q_tile, 0)


def _moba_tile(qi, rb_ref, qall_ref, k_ref, v_ref, bt_ref, o_ref,
               s_ref, qaux_ref, mx_ref, l_ref, acc_ref, *, tq, nblk):
    hp = pl.program_id(1)
    heads = range(HEADS_PER_LANE_BLOCK)
    aux_lane0 = _moba_aux_lane0

    tile_rows = pl.ds(pl.multiple_of(qi * tq, tq), tq)
    q = qall_ref[0, 0, tile_rows, :]
    lane = lax.broadcasted_iota(jnp.int32, (tq, V7X_LANES), 1)
    q_aug = [jnp.where(lane // HEAD_DIM == h, q, qaux_ref[h, tile_rows, :]) for h in heads]
    both = lambda x: jnp.concatenate([x, x], axis=1)
    halves = lambda x: (x[:, :V7X_LANES], x[:, V7X_LANES:])
    for h in heads:
        mx_ref[h] = jnp.full((tq, V7X_LANES), NEG_BIG, F32)

    aux_rows = 16
    lane_aux = lax.broadcasted_iota(jnp.int32, (aux_rows, V7X_LANES), 1)
    key_aux_far = []
    for h in heads:
        far = jnp.full((aux_rows, V7X_LANES), rb_ref[N_BUCKETS - 1, hp * HEADS_PER_LANE_BLOCK + h], F32)
        far_hi = far.astype(BF16).astype(F32)
        far_lo = (far - far_hi).astype(BF16).astype(F32)
        key_aux_far.append(jnp.where(lane_aux == aux_lane0(h) + nblk, far_hi,
                                     jnp.where(lane_aux == aux_lane0(h) + nblk + 1, far_lo, 0.0)))

    def key_operand(h, n, role):
        start = pl.multiple_of(n * tq, tq)
        kb = k_ref[0, 0, pl.ds(start, tq), :]
        if role == "own":
            aux = jnp.zeros((aux_rows, V7X_LANES), F32)
        else:
            base = key_aux_far[h] if role == "far" else jnp.zeros((aux_rows, V7X_LANES), F32)
            aux = jnp.where(lane_aux == aux_lane0(h) + n, 1.0, base)
        aux = jnp.concatenate([aux.astype(BF16)] * (tq // aux_rows), axis=0)
        return jnp.where(lane // HEAD_DIM == h, kb, aux)

    pipeline = _software_pipeline

    def logits_span(first, roles, last=False):
        mx = [mx_ref[h] for h in heads]

        def products(st, i, h):
            st["s"] = _dot_nt(q_aug[h], key_operand(h, first + i, roles[i]))

        def finish(st, i, h):
            start = pl.multiple_of((first + i) * tq, tq)
            s = st.pop("s")
            if roles[i] == "own":
                s = s + bt_ref[h, 0]
            elif roles[i] == "prev":
                s = s + bt_ref[h, 1]
            s_ref[h, :, pl.ds(start, tq)] = s
            s_lo, s_hi = halves(s)
            mx[h] = jnp.maximum(mx[h], jnp.maximum(s_lo, s_hi))

        pipeline([(i, h) for i in range(len(roles)) for h in heads], [products, finish])
        for h in heads:
            if last:
                mx_ref[h] = jnp.broadcast_to(jnp.max(mx[h], axis=1, keepdims=True), (tq, V7X_LANES))
                l_ref[h] = jnp.zeros((tq, V7X_LANES), F32)
                acc_ref[h] = jnp.zeros((tq, V7X_LANES), F32)
            else:
                mx_ref[h] = mx[h]

    def probs_span(first, nb, last=False):
        m = [both(mx_ref[h]) for h in heads]
        l = [l_ref[h] for h in heads]
        acc = [acc_ref[h] for h in heads]

        def exponentials(st, i, h):
            start = pl.multiple_of((first + i) * tq, tq)
            p = jnp.exp(s_ref[h, :, pl.ds(start, tq)] - m[h])
            p_lo, p_hi = halves(p)
            l[h] = l[h] + (p_lo + p_hi)
            st["p"] = p.astype(BF16)

        def accumulate(st, i, h):
            start = pl.multiple_of((first + i) * tq, tq)
            acc[h] = acc[h] + _dot(st.pop("p"), v_ref[0, 0, pl.ds(start, tq), :])

        pipeline([(i, h) for i in range(nb) for h in heads], [exponentials, accumulate])
        if last:
            out = [acc[h] / jnp.sum(l[h], axis=1, keepdims=True) for h in heads]
            o_ref[0, tile_rows, :] = jnp.where(lane // HEAD_DIM == 0, out[0], out[1]).astype(o_ref.dtype)
        else:
            for h in heads:
                l_ref[h] = l[h]
                acc_ref[h] = acc[h]

    group = MOBA_GROUP
    n_blocks = qi + 1
    n_full = jnp.maximum(n_blocks // group - 1, 0)
    last_size = n_blocks - n_full * group

    def far_groups(g, carry):
        logits_span(g * group, ["far"] * group)
        return carry

    lax.fori_loop(0, n_full, far_groups, 0)
    for size in range(1, 2 * group):
        @pl.when(last_size == size)
        def _(size=size):
            logits_span(n_blocks - size, (["far"] * (2 * group) + ["prev", "own"])[-size:]
                        if size >= 2 else ["own"], last=True)

    def prob_groups(g, carry):
        probs_span(g * group, group)
        return carry

    lax.fori_loop(0, n_full, prob_groups, 0)
    for size in range(1, 2 * group):
        @pl.when(last_size == size)
        def _(size=size):
            probs_span(n_blocks - size, size, last=True)


def moba_attention(qkv, rel_bias, b, s):
    d = qkv.shape[-1]
    tq = MOBA_BLOCK
    assert s % tq == 0 and tq >= 2 * MAX_DISTANCE
    nblk = s // tq
    qkv4 = qkv.reshape(3, b, s, d)
    tiles = t5_bias_tiles(rel_bias, tq)
    return pl.pallas_call(
        functools.partial(_moba_kernel, tq=tq, nblk=nblk),
        grid=(b, d // V7X_LANES),
        in_specs=[pl.BlockSpec(memory_space=pltpu.SMEM),
                  pl.BlockSpec((1, 1, s, V7X_LANES), lambda bi, hp: (0, bi, 0, hp)),
                  pl.BlockSpec((1, 1, s, V7X_LANES), lambda bi, hp: (1, bi, 0, hp)),
                  pl.BlockSpec((1, 1, s, V7X_LANES), lambda bi, hp: (2, bi, 0, hp)),
                  pl.BlockSpec((HEADS_PER_LANE_BLOCK, 2, tq, tq), lambda bi, hp: (hp, 0, 0, 0))],
        out_specs=pl.BlockSpec((1, s, V7X_LANES), lambda bi, hp: (bi, 0, hp)),
        out_shape=jax.ShapeDtypeStruct((b, s, d), BF16),
        scratch_shapes=[pltpu.VMEM((HEADS_PER_LANE_BLOCK, tq, s), F32),
                        pltpu.VMEM((HEADS_PER_LANE_BLOCK, s, V7X_LANES), BF16),
                        pltpu.VMEM((HEADS_PER_LANE_BLOCK, tq, V7X_LANES), F32),
                        pltpu.VMEM((HEADS_PER_LANE_BLOCK, tq, V7X_LANES), F32),
                        pltpu.VMEM((HEADS_PER_LANE_BLOCK, tq, V7X_LANES), F32)],
        compiler_params=_cparams(("arbitrary", "arbitrary"), 32),
        name="moba",
    )(rel_bias, qkv4, qkv4, qkv4, tiles)


def _whole(shape):
    return pl.BlockSpec(shape, lambda i: (0,) * len(shape), pipeline_mode=pl.Buffered(1))


def _ffn_kernel(o_ref, wo_ref, h_ref, g_ref, w1_ref, w3_ref, w2_ref, out_ref, *, chunks):
    x = h_ref[...] + _dot(o_ref[...], wo_ref[...])
    hn = _rmsnorm_f32(x, g_ref[...]).astype(BF16)
    acc = x
    for c0, c1 in chunks:
        a = _dot(hn, w1_ref[:, c0:c1])
        b = _dot(hn, w3_ref[:, c0:c1])
        acc = acc + _dot((_silu(a) * b).astype(BF16), w2_ref[c0:c1, :])
    out_ref[...] = acc


def oproj_dense_ffn(o, w_o, h, g, w1, w3, w2, tm, chunk):
    t, d = h.shape
    f = w1.shape[1]
    chunks = tuple((c, min(c + chunk, f)) for c in range(0, f, chunk))
    return pl.pallas_call(
        functools.partial(_ffn_kernel, chunks=chunks),
        grid=(t // tm,),
        in_specs=[pl.BlockSpec((tm, d), lambda i: (i, 0)),
                  _whole((d, d)),
                  pl.BlockSpec((tm, d), lambda i: (i, 0)),
                  pl.BlockSpec((1, d), lambda i: (0, 0)),
                  _whole((d, f)), _whole((d, f)), _whole((f, d))],
        out_specs=pl.BlockSpec((tm, d), lambda i: (i, 0)),
        out_shape=jax.ShapeDtypeStruct((t, d), F32),
        compiler_params=_cparams(("arbitrary",), 56),
        name="oproj_dense_ffn",
    )(o, w_o, h, g.reshape(1, d), w1, w3, w2)


def _row_tile_chunks(d):
    assert d % V7X_LANES == 0
    return d // V7X_LANES


def _store_row_tiles(ref, x, row0=0):
    n, d = x.shape
    c = _row_tile_chunks(d)
    for k in range(c):
        ref[pl.ds(row0 * c + k, n, stride=c), :] = x[:, k * V7X_LANES:(k + 1) * V7X_LANES]


def _load_row_tiles(ref, n, d):
    c = _row_tile_chunks(d)
    return [ref[pl.ds(k, n, stride=c), :] for k in range(c)]


COL_E0, COL_E1, COL_W0, COL_W1, COL_R0, COL_R1 = range(6)


def _router_kernel(o_ref, wo_ref, h_ref, g_ref, wr_ref, h1_ref, hn_ref, slab_ref, cnt_ref, carry_ref, *, tm):
    i = pl.program_id(0)

    @pl.when(i == 0)
    def _():
        carry_ref[...] = jnp.zeros_like(carry_ref)

    rs = min(ROUTER_ROWS, tm)
    assert tm % rs == 0
    g = g_ref[...]
    w = wr_ref[...]
    w_hi = w.astype(BF16)
    w_lo = (w - w_hi.astype(F32)).astype(BF16)
    lane = lax.broadcasted_iota(jnp.int32, (rs, V7X_LANES), 1)
    earlier = (lax.broadcasted_iota(jnp.int32, (rs, rs), 1)
               < lax.broadcasted_iota(jnp.int32, (rs, rs), 0)).astype(BF16)
    neg_inf = jnp.float32(-jnp.inf)
    count = [carry_ref[0:1, :]]

    def project(st, r):
        rows = pl.ds(r * rs, rs)
        h1 = h_ref[rows, :] + _dot(o_ref[rows, :], wo_ref[...])
        h1_ref[rows, :] = h1
        st["h1"] = h1

    def normalise(st, r):
        hn = _rmsnorm_f32(st.pop("h1"), g)
        _store_row_tiles(hn_ref, hn, r * rs)
        st["hi"] = hn.astype(BF16)
        st["lo"] = (hn - st["hi"].astype(F32)).astype(BF16)

    def logits(st, r):
        hi, lo = st.pop("hi"), st.pop("lo")
        st["logits"] = _dot(hi, w_hi) + _dot(hi, w_lo) + _dot(lo, w_hi)

    def top2(st, r):
        lg = jnp.where(lane < N_EXPERTS, st.pop("logits"), neg_inf)
        rank_of = jnp.zeros((rs, V7X_LANES), F32)
        for s in range(1, N_EXPERTS):
            lower = pltpu.roll(lg, s, axis=1)
            higher = pltpu.roll(lg, V7X_LANES - s, axis=1)
            rank_of = rank_of + jnp.where(lower >= lg, 1.0, 0.0) + jnp.where(higher > lg, 1.0, 0.0)
        first = (rank_of == 0.0) & (lane < N_EXPERTS)
        second = (rank_of == 1.0) & (lane < N_EXPERTS)
        pick = lambda sel, x: jnp.sum(jnp.where(sel, x, 0.0), axis=1, keepdims=True)
        lane_f = lane.astype(F32)
        m0, m1 = pick(first, lg), pick(second, lg)
        i0, i1 = pick(first, lane_f).astype(jnp.int32), pick(second, lane_f).astype(jnp.int32)
        e = jnp.exp(m1 - m0)
        st.update(i0=i0, i1=i1, w0=1.0 / (1.0 + e), w1=e / (1.0 + e),
                  hot=((lane == i0) | (lane == i1)).astype(BF16))

    def rank(st, r):
        hot = st.pop("hot")
        before = _dot(earlier, hot) + count[0]
        i0, i1 = st.pop("i0"), st.pop("i1")
        r0 = jnp.sum(jnp.where(lane == i0, before, 0.0), axis=1, keepdims=True)
        r1 = jnp.sum(jnp.where(lane == i1, before, 0.0), axis=1, keepdims=True)
        count[0] = count[0] + jnp.sum(hot.astype(F32), axis=0, keepdims=True)
        slab = jnp.zeros((rs, V7X_LANES), F32)
        for c, val in ((COL_E0, i0.astype(F32)), (COL_E1, i1.astype(F32)), (COL_W0, st.pop("w0")),
                       (COL_W1, st.pop("w1")), (COL_R0, r0), (COL_R1, r1)):
            slab = jnp.where(lane == c, val, slab)
        slab_ref[pl.ds(r * rs, rs), :] = slab

    _software_pipeline([(r,) for r in range(tm // rs)], [project, normalise, logits, top2, rank])
    carry_ref[0:1, :] = count[0]
    cnt_ref[...] = carry_ref[...]


def oproj_moe_router(o, w_o, h, g, w_router, tm):
    t, d = h.shape
    wr = jnp.zeros((d, V7X_LANES), F32).at[:, :N_EXPERTS].set(w_router)
    return pl.pallas_call(
        functools.partial(_router_kernel, tm=tm),
        grid=(t // tm,),
        in_specs=[pl.BlockSpec((tm, d), lambda i: (i, 0)),
                  _whole((d, d)),
                  pl.BlockSpec((tm, d), lambda i: (i, 0)),
                  pl.BlockSpec((1, d), lambda i: (0, 0)),
                  pl.BlockSpec((d, V7X_LANES), lambda i: (0, 0))],
        out_specs=[pl.BlockSpec((tm, d), lambda i: (i, 0)),
                   pl.BlockSpec((tm * d // V7X_LANES, V7X_LANES), lambda i: (i, 0)),
                   pl.BlockSpec((tm, V7X_LANES), lambda i: (i, 0)),
                   pl.BlockSpec((8, V7X_LANES), lambda i: (0, 0))],
        out_shape=[jax.ShapeDtypeStruct((t, d), F32),
                   jax.ShapeDtypeStruct((t * d // V7X_LANES, V7X_LANES), F32),
                   jax.ShapeDtypeStruct((t, V7X_LANES), F32),
                   jax.ShapeDtypeStruct((8, V7X_LANES), F32)],
        scratch_shapes=[pltpu.VMEM((8, V7X_LANES), F32)],
        compiler_params=_cparams(("arbitrary",), 40),
        name="oproj_moe_router",
    )(o, w_o, h, g.reshape(1, d), wr)


SCALAR_UNROLL = 16


def _sorted_list_kernel(beg_ref, end_ref, d0_ref, d1_ref, out_ref, *, chunk, t, tm, rows):
    i = pl.program_id(0)

    @pl.when(i == 0)
    def _():
        for e in range(N_EXPERTS + 1):
            beg, end = beg_ref[e], end_ref[e]
            n = end - beg

            def fill_group(g, carry, end=end):
                p0 = end - SCALAR_UNROLL * (g + 1)
                r0 = lax.rem(p0, tm)
                for k in range(SCALAR_UNROLL):
                    out_ref[p0 + k] = 2 * t + r0 + k
                return carry
            lax.fori_loop(0, n // SCALAR_UNROLL, fill_group, 0)

            def fill_one(p, carry):
                out_ref[p] = 2 * t + lax.rem(p, tm)
                return carry
            lax.fori_loop(beg, beg + lax.rem(n, SCALAR_UNROLL), fill_one, 0)

    base = i * chunk

    def body(g, carry):
        for k in range(SCALAR_UNROLL):
            j = g * SCALAR_UNROLL + k
            out_ref[d0_ref[0, 0, j]] = base + j
            out_ref[d1_ref[0, 0, j]] = t + base + j
        return carry
    lax.fori_loop(0, chunk // SCALAR_UNROLL, body, 0)


def sorted_assignment_list(dest0, dest1, pad_begin, pad_end, t, tm, rows):
    chunk = min(t, 2048)
    assert t % chunk == 0 and chunk % SCALAR_UNROLL == 0 and tm % SCALAR_UNROLL == 0
    blocked = lambda a: a.reshape(t // chunk, 1, chunk)
    dest_spec = pl.BlockSpec((1, 1, chunk), lambda i, beg, end: (i, 0, 0), memory_space=pltpu.SMEM)
    grid_spec = pltpu.PrefetchScalarGridSpec(
        num_scalar_prefetch=2,
        grid=(t // chunk,),
        in_specs=[dest_spec, dest_spec],
        out_specs=pl.BlockSpec(memory_space=pltpu.SMEM),
    )
    return pl.pallas_call(
        functools.partial(_sorted_list_kernel, chunk=chunk, t=t, tm=tm, rows=rows),
        grid_spec=grid_spec,
        out_shape=jax.ShapeDtypeStruct((rows,), jnp.int32),
        compiler_params=_cparams(("arbitrary",), 16),
        name="sorted_assignment_list",
    )(pad_begin, pad_end, blocked(dest0), blocked(dest1))


def _moe_kernel(te_ref, nu_ref, tokc_ref, tokn_ref, dst_ref, hn_hbm, w1_ref, w3_ref, w2_ref, y_hbm,
                xbuf, xb16, acc_ref, ybuf, gsem, ssem, *, tm, nf, nt, n_real):
    i = pl.program_id(0)
    j = pl.program_id(1)
    nu = nu_ref[0]
    slot = i % 2
    d = acc_ref.shape[1]
    c = _row_tile_chunks(d)

    def row_loop(start_row_copy):
        for r in range(tm):
            start_row_copy(r)

    def row(r):
        return pl.ds(r * c, c) if isinstance(r, int) else pl.ds(pl.multiple_of(r * c, c), c)

    def start_gather(tok_ref, s):
        row_loop(lambda r: pltpu.make_async_copy(
            hn_hbm.at[row(tok_ref[0, 0, r]), :], xbuf.at[s, row(r), :], gsem.at[s]).start())

    def wait_gather(s):
        pltpu.make_async_copy(hn_hbm.at[pl.ds(0, tm * c), :], xbuf.at[s], gsem.at[s]).wait()

    def start_scatter():
        row_loop(lambda r: pltpu.make_async_copy(
            ybuf.at[row(r), :], y_hbm.at[row(dst_ref[0, 0, r]), :], ssem.at[0]).start())

    def spare_rows_copy():
        return pltpu.make_async_copy(ybuf, y_hbm.at[pl.ds(n_real * c, tm * c), :], ssem.at[0])

    def wait_scatter():
        spare_rows_copy().wait()

    @pl.when((j == 0) & (i == 0))
    def _():
        ybuf[...] = jnp.zeros_like(ybuf)
        spare_rows_copy().start()
        spare_rows_copy().wait()

        @pl.when(nu > 0)
        def _():
            start_gather(tokc_ref, 0)

    @pl.when((j == 0) & (i < nu))
    def _():
        wait_gather(slot)
        for k, chunk in enumerate(_load_row_tiles(xbuf.at[slot], tm, d)):
            xb16[:, k * V7X_LANES:(k + 1) * V7X_LANES] = chunk.astype(BF16)

    @pl.when((j == 0) & (i + 1 < nu))
    def _():
        start_gather(tokn_ref, 1 - slot)

    @pl.when(i < nu)
    def _():
        x = xb16[...]
        a = _dot(x, w1_ref[0])
        b = _dot(x, w3_ref[0])
        part = _dot((_silu(a) * b).astype(BF16), w2_ref[0])

        @pl.when(j == 0)
        def _():
            acc_ref[...] = part

        @pl.when(j > 0)
        def _():
            acc_ref[...] += part

    last = j == nf - 1

    @pl.when(last & (i >= 1) & (i - 1 < nu))
    def _():
        wait_scatter()

    @pl.when(last & (i < nu))
    def _():
        _store_row_tiles(ybuf, acc_ref[...])
        start_scatter()

    @pl.when(last & (i == nt - 1) & (i < nu))
    def _():
        wait_scatter()


def moe_experts(hn, slab, counts, w1, w3, w2, tm, tf):
    t = slab.shape[0]
    d = w1.shape[1]
    c = _row_tile_chunks(d)
    f = w1.shape[2]
    nf = f // tf
    nt = (2 * t) // tm + N_EXPERTS
    rows = nt * tm

    e0 = slab[:, COL_E0].astype(jnp.int32)
    e1 = slab[:, COL_E1].astype(jnp.int32)
    r0 = slab[:, COL_R0].astype(jnp.int32)
    r1 = slab[:, COL_R1].astype(jnp.int32)
    cnt = counts[0, :N_EXPERTS].astype(jnp.int32)
    tiles = (cnt + tm - 1) // tm
    tile_end = jnp.cumsum(tiles)
    tile_start = tile_end - tiles
    nu = tile_end[-1:]
    offs = tile_start * tm
    dest0 = offs[e0] + r0
    dest1 = offs[e1] + r1
    tile_expert = jnp.minimum(
        jnp.sum(jnp.arange(nt, dtype=jnp.int32)[:, None] >= tile_end[None, :], axis=1), N_EXPERTS - 1
    ).astype(jnp.int32)
    pad_begin = jnp.concatenate([offs + cnt, nu * tm]).astype(jnp.int32)
    pad_end = jnp.concatenate([tile_end * tm, jnp.full((1,), rows, jnp.int32)]).astype(jnp.int32)
    dst_sorted = sorted_assignment_list(dest0, dest1, pad_begin, pad_end, t, tm, rows)
    tok_sorted = jnp.where(dst_sorted >= 2 * t, 0, jnp.where(dst_sorted >= t, dst_sorted - t, dst_sorted))
    tok3 = tok_sorted.reshape(nt, 1, tm)
    dst3 = dst_sorted.reshape(nt, 1, tm)

    def w_in(shape, which):
        def index_map(i, j, te, nu_):
            ii = jnp.minimum(i, nu_[0] - 1)
            jj = jnp.where(i < nu_[0], j, nf - 1)
            return (te[ii], 0, jj) if which == "up" else (te[ii], jj, 0)
        return pl.BlockSpec(shape, index_map)

    smem_tile = lambda f_: pl.BlockSpec((1, 1, tm), f_, memory_space=pltpu.SMEM)
    grid_spec = pltpu.PrefetchScalarGridSpec(
        num_scalar_prefetch=2,
        grid=(nt, nf),
        in_specs=[smem_tile(lambda i, j, te, nu_: (i, 0, 0)),
                  smem_tile(lambda i, j, te, nu_: (jnp.minimum(i + 1, nt - 1), 0, 0)),
                  smem_tile(lambda i, j, te, nu_: (i, 0, 0)),
                  pl.BlockSpec(memory_space=pl.ANY),
                  w_in((1, d, tf), "up"), w_in((1, d, tf), "up"), w_in((1, tf, d), "down")],
        out_specs=pl.BlockSpec(memory_space=pl.ANY),
        scratch_shapes=[pltpu.VMEM((2, tm * c, V7X_LANES), F32),
                        pltpu.VMEM((tm, d), BF16),
                        pltpu.VMEM((tm, d), F32),
                        pltpu.VMEM((tm * c, V7X_LANES), F32),
                        pltpu.SemaphoreType.DMA((2,)),
                        pltpu.SemaphoreType.DMA((1,))],
    )
    return pl.pallas_call(
        functools.partial(_moe_kernel, tm=tm, nf=nf, nt=nt, n_real=2 * t),
        grid_spec=grid_spec,
        out_shape=jax.ShapeDtypeStruct(((2 * t + tm) * c, V7X_LANES), F32),
        compiler_params=_cparams(("arbitrary", "arbitrary"), 56),
        name="moe_experts",
    )(tile_expert, nu, tok3, tok3, dst3, hn, w1, w3, w2)


def _combine_kernel(h_ref, y0_ref, y1_ref, slab_ref, g_ref, out_ref, *, final):
    slab = slab_ref[...]
    tm, d = h_ref.shape
    y0 = jnp.concatenate(_load_row_tiles(y0_ref, tm, d), axis=1)
    y1 = jnp.concatenate(_load_row_tiles(y1_ref, tm, d), axis=1)
    x = h_ref[...] + slab[:, COL_W0:COL_W0 + 1] * y0 + slab[:, COL_W1:COL_W1 + 1] * y1
    out_ref[...] = _rmsnorm_f32(x, g_ref[...]) if final else x


def moe_combine(h, y, slab, g, tm, final):
    t, d = h.shape
    nb = t // tm
    c = _row_tile_chunks(d)
    return pl.pallas_call(
        functools.partial(_combine_kernel, final=final),
        grid=(nb,),
        in_specs=[pl.BlockSpec((tm, d), lambda i: (i, 0)),
                  pl.BlockSpec((tm * c, V7X_LANES), lambda i: (i, 0)),
                  pl.BlockSpec((tm * c, V7X_LANES), lambda i: (i + nb, 0)),
                  pl.BlockSpec((tm, V7X_LANES), lambda i: (i, 0)),
                  pl.BlockSpec((1, d), lambda i: (0, 0))],
        out_specs=pl.BlockSpec((tm, d), lambda i: (i, 0)),
        out_shape=jax.ShapeDtypeStruct((t, d), F32),
        compiler_params=_cparams(("arbitrary",), 40),
        name="moe_combine",
    )(h, y, y, slab, g.reshape(1, d))


def _norm_kernel(h_ref, g_ref, out_ref):
    out_ref[...] = _rmsnorm_f32(h_ref[...], g_ref[...])


def final_norm_only(h, g, tm):
    t, d = h.shape
    return pl.pallas_call(
        _norm_kernel,
        grid=(t // tm,),
        in_specs=[pl.BlockSpec((tm, d), lambda i: (i, 0)), pl.BlockSpec((1, d), lambda i: (0, 0))],
        out_specs=pl.BlockSpec((tm, d), lambda i: (i, 0)),
        out_shape=jax.ShapeDtypeStruct((t, d), F32),
        compiler_params=_cparams(("arbitrary",), 40),
        name="final_norm",
    )(h, g.reshape(1, d))


V7X_MXU_DIM = 256


def _tile_sizes(t, f_expert):
    def rows(want):
        tm = min(want, t)
        assert t % tm == 0
        return tm
    half = f_expert // 2
    return dict(qkv=rows(1024), tokens=rows(512), ffn_chunk=1024,
                expert_chunk=half if half % V7X_MXU_DIM == 0 else f_expert)


def kernel(x, w_qkv, w_o, mixer_norm, ffn_norm, rel_bias, w1, w3, w2, router, e_w1, e_w3, e_w2, final_norm):
    b, s, d = x.shape
    assert d == N_HEADS * HEAD_DIM and s % MOBA_BLOCK == 0
    t = b * s
    depth = w_qkv.shape[0]
    h = x.reshape(t, d)
    tiles = _tile_sizes(t, e_w1.shape[-1])
    tm_big, tm_mid = tiles["qkv"], tiles["tokens"]
    normed = False
    for i in range(depth):
        qkv = qkv_proj(h, mixer_norm[i], w_qkv[i].astype(BF16), tm_big)
        if i % 2 == 0:
            o = stick_breaking_attention(qkv, b, s, MOBA_BLOCK)
        else:
            o = moba_attention(qkv, rel_bias, b, s)
        o = o.reshape(t, d)
        wo = w_o[i].astype(BF16)
        jj = i // 2
        if i % 2 == 0:
            h = oproj_dense_ffn(o, wo, h, ffn_norm[i], w1[jj].astype(BF16), w3[jj].astype(BF16),
                                w2[jj].astype(BF16), tm_mid, tiles["ffn_chunk"])
        else:
            h, hn, slab, counts = oproj_moe_router(o, wo, h, ffn_norm[i], router[jj], tm_mid)
            y = moe_experts(hn, slab, counts, e_w1[jj].astype(BF16), e_w3[jj].astype(BF16),
                            e_w2[jj].astype(BF16), tm_mid, tiles["expert_chunk"])
            last = i == depth - 1
            h = moe_combine(h, y, slab, final_norm if last else ffn_norm[i], tm_mid, last)
            normed = last
    if not normed:
        h = final_norm_only(h, final_norm, tm_mid)
    return h.reshape(b, s, d)
```

```python
import functools
import math

import jax
import jax.numpy as jnp
from jax import lax
from jax.experimental import pallas as pl
from jax.experimental.pallas import tpu as pltpu

N_HEADS = 16
HEAD_DIM = 64
MOBA_BLOCK = 256
MOBA_TOPK = 3
N_BUCKETS = 32
MAX_DISTANCE = 128
N_EXPERTS = 8
RMS_EPS = 1e-6
NEG_BIG = -1e30

V7X_LANES = 128
SB_GROUP = 4
MOBA_GROUP = 4
ROUTER_ROWS = 128
V7X_VMEM_BYTES = 64 * 1024 * 1024
HEADS_PER_LANE_BLOCK = V7X_LANES // HEAD_DIM

F32 = jnp.float32
BF16 = jnp.bfloat16


def _cparams(semantics, vmem_mb):
    assert vmem_mb * 1024 * 1024 < V7X_VMEM_BYTES
    return pltpu.CompilerParams(dimension_semantics=semantics,
                                vmem_limit_bytes=vmem_mb * 1024 * 1024)


def _rmsnorm_f32(x, g):
    return x * lax.rsqrt(jnp.mean(x * x, axis=-1, keepdims=True) + RMS_EPS) * g


def _silu(a):
    return a * (1.0 / (1.0 + jnp.exp(-a)))


def _dot(a, b):
    return jnp.dot(a, b, preferred_element_type=F32)


def _software_pipeline(tasks, stages):
    state = [dict() for _ in tasks]
    for step in range(len(tasks) + len(stages) - 1):
        for k in reversed(range(len(stages))):
            t = step - k
            if 0 <= t < len(tasks):
                stages[k](state[t], *tasks[t])


def _dot_nt(a, b):
    return lax.dot_general(a, b, (((1,), (1,)), ((), ())), preferred_element_type=F32)


def _qkv_kernel(x_ref, g_ref, w_ref, o_ref, hn_ref):
    j = pl.program_id(1)

    @pl.when(j == 0)
    def _():
        hn_ref[...] = _rmsnorm_f32(x_ref[...], g_ref[...]).astype(BF16)

    scale = jnp.where(j == 0, HEAD_DIM ** -0.5, 1.0)
    o_ref[0] = (_dot(hn_ref[...], w_ref[...]) * scale).astype(o_ref.dtype)


def qkv_proj(h, g, w_bf16, tm):
    t, d = h.shape
    return pl.pallas_call(
        _qkv_kernel,
        grid=(t // tm, 3),
        in_specs=[pl.BlockSpec((tm, d), lambda i, j: (i, 0)),
                  pl.BlockSpec((1, d), lambda i, j: (0, 0)),
                  pl.BlockSpec((d, d), lambda i, j: (0, j))],
        out_specs=pl.BlockSpec((1, tm, d), lambda i, j: (j, i, 0)),
        out_shape=jax.ShapeDtypeStruct((3, t, d), BF16),
        scratch_shapes=[pltpu.VMEM((tm, d), BF16)],
        compiler_params=_cparams(("arbitrary", "arbitrary"), 40),
        name="qkv_proj",
    )(h, g.reshape(1, d), w_bf16)


def _sb_kernel(q_ref, k_ref, v_ref, o_ref, acc0_ref, acc1_ref, r0_ref, r1_ref, u_ref, *, tq):
    row = lax.broadcasted_iota(jnp.int32, (tq, tq), 0)
    col = lax.broadcasted_iota(jnp.int32, (tq, tq), 1)
    u_ref[...] = -(row > col).astype(BF16)

    def q_tile(qi, carry):
        _sb_tile(qi, q_ref, k_ref, v_ref, o_ref, acc0_ref, acc1_ref, r0_ref, r1_ref, u_ref, tq=tq)
        return carry

    lax.fori_loop(0, q_ref.shape[2] // tq, q_tile, 0)


def _sb_tile(qi, q_ref, k_ref, v_ref, o_ref, acc0_ref, acc1_ref, r0_ref, r1_ref, u_ref, *, tq):
    acc_ref = (acc0_ref, acc1_ref)
    r_ref = (r0_ref, r1_ref)
    row = lax.broadcasted_iota(jnp.int32, (tq, tq), 0)
    col = lax.broadcasted_iota(jnp.int32, (tq, tq), 1)
    tile_rows = pl.ds(pl.multiple_of(qi * tq, tq), tq)
    q = q_ref[0, 0, tile_rows, :]
    lane = lax.broadcasted_iota(jnp.int32, (tq, V7X_LANES), 1)
    past = col < row
    heads = range(HEADS_PER_LANE_BLOCK)
    qms = [jnp.where(lane // HEAD_DIM == h, q, jnp.zeros_like(q)) for h in heads]

    def span(first, nb, diagonal, only=False):
        tasks = [(i, h) for i in reversed(range(nb)) for h in heads]
        state = [dict() for _ in tasks]
        r = [jnp.zeros((tq, 1), F32) if diagonal else r_ref[h][...] for h in heads]
        acc = [None for h in heads]

        def stage(k, st, i, h):
            on_diagonal = diagonal and i == nb - 1
            start = pl.multiple_of((first + i) * tq, tq)
            if k == 0:
                st["z"] = _dot_nt(qms[h], k_ref[0, 0, pl.ds(start, tq), :])
            elif k == 1:
                z = st.pop("z")
                zb = z.astype(BF16)
                t = jnp.log(1 + jnp.exp(-jnp.abs(zb)))
                sp = jnp.maximum(zb, 0) + t
                if on_diagonal:
                    sp = jnp.where(past, sp, jnp.zeros_like(sp))
                st["sp"] = sp
                st["log_sig"] = jnp.minimum(z, 0.0) - t.astype(F32)
            elif k == 2:
                st["after"] = _dot(st["sp"], u_ref[...])
            elif k == 3:
                tot = st.pop("after") + r[h]
                w = jnp.exp(st.pop("log_sig") + tot)
                if on_diagonal:
                    w = jnp.where(past, w, 0.0)
                r[h] = tot[:, 0:1] - st.pop("sp")[:, 0:1].astype(F32)
                st["w"] = w.astype(BF16)
            else:
                part = _dot(st.pop("w"), v_ref[0, 0, pl.ds(start, tq), :])
                acc[h] = part if acc[h] is None else acc[h] + part

        n_stages = 5
        for step in range(len(tasks) + n_stages - 1):
            for k in reversed(range(n_stages)):
                ti = step - k
                if 0 <= ti < len(tasks):
                    stage(k, state[ti], *tasks[ti])
        if only:
            store_output(acc[0], acc[1])
            return
        for h in heads:
            r_ref[h][...] = r[h]
            if diagonal:
                acc_ref[h][...] = acc[h]
            else:
                acc_ref[h][...] += acc[h]

    def store_output(acc0, acc1):
        o_ref[0, tile_rows, :] = jnp.where(lane // HEAD_DIM == 0, acc0, acc1).astype(o_ref.dtype)

    n_blocks = qi + 1
    n_full = jnp.maximum(n_blocks // SB_GROUP - 1, 0)
    diag_size = n_blocks - n_full * SB_GROUP
    for size in range(1, 2 * SB_GROUP):
        @pl.when((diag_size == size) & (n_full == 0))
        def _(size=size):
            span(n_blocks - size, size, True, only=True)

        if size >= SB_GROUP:
            @pl.when((diag_size == size) & (n_full > 0))
            def _(size=size):
                span(n_blocks - size, size, True)

    def body(s, carry):
        span((n_full - 1 - s) * SB_GROUP, SB_GROUP, False)
        return carry

    lax.fori_loop(0, n_full, body, 0)

    @pl.when(n_full > 0)
    def _():
        store_output(acc0_ref[...], acc1_ref[...])


def stick_breaking_attention(qkv, b, s, tq):
    d = qkv.shape[-1]
    qkv4 = qkv.reshape(3, b, s, d)
    plane = lambda p: pl.BlockSpec((1, 1, s, V7X_LANES), lambda bi, hp: (p, bi, 0, hp))
    return pl.pallas_call(
        functools.partial(_sb_kernel, tq=tq),
        grid=(b, d // V7X_LANES),
        in_specs=[plane(0), plane(1), plane(2)],
        out_specs=pl.BlockSpec((1, s, V7X_LANES), lambda bi, hp: (bi, 0, hp)),
        out_shape=jax.ShapeDtypeStruct((b, s, d), BF16),
        scratch_shapes=[pltpu.VMEM((tq, V7X_LANES), F32), pltpu.VMEM((tq, V7X_LANES), F32),
                        pltpu.VMEM((tq, 1), F32), pltpu.VMEM((tq, 1), F32),
                        pltpu.VMEM((tq, tq), BF16)],
        compiler_params=_cparams(("arbitrary", "arbitrary"), 32),
        name="stick_breaking",
    )(qkv4, qkv4, qkv4)


def _t5_bias_kernel(rb_ref, o_ref, *, tq):
    h = pl.program_id(0)
    row = lax.broadcasted_iota(jnp.int32, (tq, tq), 0)
    col = lax.broadcasted_iota(jnp.int32, (tq, tq), 1)
    max_exact = N_BUCKETS // 2
    for o in range(2):
        dist = o * tq + row - col
        n = jnp.maximum(dist, 0)
        nf = jnp.maximum(n, 1).astype(F32)
        scaled = jnp.log(nf / max_exact) / math.log(MAX_DISTANCE / max_exact) * (N_BUCKETS - max_exact)
        large = max_exact + jnp.where(scaled < 0, jnp.ceil(scaled), jnp.floor(scaled)).astype(jnp.int32)
        large = jnp.minimum(large, N_BUCKETS - 1)
        bucket = jnp.where(n < max_exact, n, large)
        bias = jnp.zeros((tq, tq), F32)
        for bkt in range(N_BUCKETS):
            bias = jnp.where(bucket == bkt, rb_ref[bkt, h], bias)
        if o == 0:
            bias = jnp.where(dist >= 0, bias, NEG_BIG)
        o_ref[0, o] = bias


def t5_bias_tiles(rel_bias, tq):
    return pl.pallas_call(
        functools.partial(_t5_bias_kernel, tq=tq),
        grid=(N_HEADS,),
        in_specs=[pl.BlockSpec(memory_space=pltpu.SMEM)],
        out_specs=pl.BlockSpec((1, 2, tq, tq), lambda h: (h, 0, 0, 0)),
        out_shape=jax.ShapeDtypeStruct((N_HEADS, 2, tq, tq), F32),
        compiler_params=_cparams(("arbitrary",), 16),
        name="t5_bias_tiles",
    )(rel_bias)


def _moba_aux_lane0(h):
    assert HEADS_PER_LANE_BLOCK == 2
    return ((h + 1) % HEADS_PER_LANE_BLOCK) * HEAD_DIM


def _moba_kernel(rb_ref, qall_ref, k_ref, v_ref, bt_ref, o_ref,
                 s_ref, qaux_ref, mx_ref, l_ref, acc_ref, *, tq, nblk):
    heads = range(HEADS_PER_LANE_BLOCK)
    s_len = nblk * tq
    n_sel = min(MOBA_TOPK, nblk - 1)
    big = jnp.asarray(-NEG_BIG, BF16).astype(F32)
    aux_lane0 = _moba_aux_lane0
    assert nblk + 2 <= HEAD_DIM

    def block_choice():
        km = jnp.concatenate(
            [jnp.mean(k_ref[0, 0, n * tq:(n + 1) * tq, :].astype(F32), axis=0, keepdims=True)
             for n in range(nblk)], axis=0)
        km_hi = km.astype(BF16)
        km_lo = (km - km_hi.astype(F32)).astype(BF16)
        q_all = qall_ref[0, 0]
        lane_all = lax.broadcasted_iota(jnp.int32, (s_len, V7X_LANES), 1)
        blk = lax.broadcasted_iota(jnp.int32, (nblk, s_len), 0)
        own = lax.broadcasted_iota(jnp.int32, (nblk, s_len), 1) // tq
        own_row = own[0:1, :]
        for h in heads:
            qm = jnp.where(lane_all // HEAD_DIM == h, q_all, jnp.zeros_like(q_all))
            gate = _dot_nt(km_hi, qm) + _dot_nt(km_lo, qm)
            gate = jnp.where(blk < own, gate, NEG_BIG)
            rows = [gate[n:n + 1, :] for n in range(nblk)]
            rank = [jnp.zeros((1, s_len), F32) for _ in range(nblk)]
            for n in range(nblk):
                for m in range(n):
                    m_wins = jnp.where(rows[m] >= rows[n], 1.0, 0.0)
                    rank[n] = rank[n] + m_wins
                    rank[m] = rank[m] + (1.0 - m_wins)
            aux = [jnp.where((rank[n] < n_sel) & (n < own_row), 0.0, -big) for n in range(nblk)]
            aux += [jnp.ones((1, s_len), F32)] * 2
            before = jnp.zeros((aux_lane0(h), s_len), F32)
            after = jnp.zeros((V7X_LANES - aux_lane0(h) - len(aux), s_len), F32)
            pieces = ([before] if before.shape[0] else []) + aux + [after]
            qaux_ref[h] = jnp.concatenate(pieces, axis=0).T.astype(BF16)

    block_choice()

    def q_tile(qi, carry):
        _moba_tile(qi, rb_ref, qall_ref, k_ref, v_ref, bt_ref, o_ref,
                   s_ref, qaux_ref, mx_ref, l_ref, acc_ref, tq=tq, nblk=nblk)
        return carry

    lax.fori_loop(0, nblk, q_tile, 0)


def _moba_tile(qi, rb_ref, qall_ref, k_ref, v_ref, bt_ref, o_ref,
               s_ref, qaux_ref, mx_ref, l_ref, acc_ref, *, tq, nblk):
    hp = pl.program_id(1)
    heads = range(HEADS_PER_LANE_BLOCK)
    aux_lane0 = _moba_aux_lane0

    tile_rows = pl.ds(pl.multiple_of(qi * tq, tq), tq)
    q = qall_ref[0, 0, tile_rows, :]
    lane = lax.broadcasted_iota(jnp.int32, (tq, V7X_LANES), 1)
    q_aug = [jnp.where(lane // HEAD_DIM == h, q, qaux_ref[h, tile_rows, :]) for h in heads]
    both = lambda x: jnp.concatenate([x, x], axis=1)
    halves = lambda x: (x[:, :V7X_LANES], x[:, V7X_LANES:])
    for h in heads:
        mx_ref[h] = jnp.full((tq, V7X_LANES), NEG_BIG, F32)

    aux_rows = 16
    lane_aux = lax.broadcasted_iota(jnp.int32, (aux_rows, V7X_LANES), 1)
    key_aux_far = []
    for h in heads:
        far = jnp.full((aux_rows, V7X_LANES), rb_ref[N_BUCKETS - 1, hp * HEADS_PER_LANE_BLOCK + h], F32)
        far_hi = far.astype(BF16).astype(F32)
        far_lo = (far - far_hi).astype(BF16).astype(F32)
        key_aux_far.append(jnp.where(lane_aux == aux_lane0(h) + nblk, far_hi,
                                     jnp.where(lane_aux == aux_lane0(h) + nblk + 1, far_lo, 0.0)))

    def key_operand(h, n, role):
        start = pl.multiple_of(n * tq, tq)
        kb = k_ref[0, 0, pl.ds(start, tq), :]
        if role == "own":
            aux = jnp.zeros((aux_rows, V7X_LANES), F32)
        else:
            base = key_aux_far[h] if role == "far" else jnp.zeros((aux_rows, V7X_LANES), F32)
            aux = jnp.where(lane_aux == aux_lane0(h) + n, 1.0, base)
        aux = jnp.concatenate([aux.astype(BF16)] * (tq // aux_rows), axis=0)
        return jnp.where(lane // HEAD_DIM == h, kb, aux)

    pipeline = _software_pipeline

    def logits_span(first, roles, last=False):
        mx = [mx_ref[h] for h in heads]

        def products(st, i, h):
            st["s"] = _dot_nt(q_aug[h], key_operand(h, first + i, roles[i]))

        def finish(st, i, h):
            start = pl.multiple_of((first + i) * tq, tq)
            s = st.pop("s")
            if roles[i] == "own":
                s = s + bt_ref[h, 0]
            elif roles[i] == "prev":
                s = s + bt_ref[h, 1]
            s_ref[h, :, pl.ds(start, tq)] = s
            s_lo, s_hi = halves(s)
            mx[h] = jnp.maximum(mx[h], jnp.maximum(s_lo, s_hi))

        pipeline([(i, h) for i in range(len(roles)) for h in heads], [products, finish])
        for h in heads:
            if last:
                mx_ref[h] = jnp.broadcast_to(jnp.max(mx[h], axis=1, keepdims=True), (tq, V7X_LANES))
                l_ref[h] = jnp.zeros((tq, V7X_LANES), F32)
                acc_ref[h] = jnp.zeros((tq, V7X_LANES), F32)
            else:
                mx_ref[h] = mx[h]

    def probs_span(first, nb, last=False):
        m = [both(mx_ref[h]) for h in heads]
        l = [l_ref[h] for h in heads]
        acc = [acc_ref[h] for h in heads]

        def exponentials(st, i, h):
            start = pl.multiple_of((first + i) * tq, tq)
            p = jnp.exp(s_ref[h, :, pl.ds(start, tq)] - m[h])
            p_lo, p_hi = halves(p)
            l[h] = l[h] + (p_lo + p_hi)
            st["p"] = p.astype(BF16)

        def accumulate(st, i, h):
            start = pl.multiple_of((first + i) * tq, tq)
            acc[h] = acc[h] + _dot(st.pop("p"), v_ref[0, 0, pl.ds(start, tq), :])

        pipeline([(i, h) for i in range(nb) for h in heads], [exponentials, accumulate])
        if last:
            out = [acc[h] / jnp.sum(l[h], axis=1, keepdims=True) for h in heads]
            o_ref[0, tile_rows, :] = jnp.where(lane // HEAD_DIM == 0, out[0], out[1]).astype(o_ref.dtype)
        else:
            for h in heads:
                l_ref[h] = l[h]
                acc_ref[h] = acc[h]

    group = MOBA_GROUP
    n_blocks = qi + 1
    n_full = jnp.maximum(n_blocks // group - 1, 0)
    last_size = n_blocks - n_full * group

    def far_groups(g, carry):
        logits_span(g * group, ["far"] * group)
        return carry

    lax.fori_loop(0, n_full, far_groups, 0)
    for size in range(1, 2 * group):
        @pl.when(last_size == size)
        def _(size=size):
            logits_span(n_blocks - size, (["far"] * (2 * group) + ["prev", "own"])[-size:]
                        if size >= 2 else ["own"], last=True)

    def prob_groups(g, carry):
        probs_span(g * group, group)
        return carry

    lax.fori_loop(0, n_full, prob_groups, 0)
    for size in range(1, 2 * group):
        @pl.when(last_size == size)
        def _(size=size):
            probs_span(n_blocks - size, size, last=True)


def moba_attention(qkv, rel_bias, b, s):
    d = qkv.shape[-1]
    tq = MOBA_BLOCK
    assert s % tq == 0 and tq >= 2 * MAX_DISTANCE
    nblk = s // tq
    qkv4 = qkv.reshape(3, b, s, d)
    tiles = t5_bias_tiles(rel_bias, tq)
    return pl.pallas_call(
        functools.partial(_moba_kernel, tq=tq, nblk=nblk),
        grid=(b, d // V7X_LANES),
        in_specs=[pl.BlockSpec(memory_space=pltpu.SMEM),
                  pl.BlockSpec((1, 1, s, V7X_LANES), lambda bi, hp: (0, bi, 0, hp)),
                  pl.BlockSpec((1, 1, s, V7X_LANES), lambda bi, hp: (1, bi, 0, hp)),
                  pl.BlockSpec((1, 1, s, V7X_LANES), lambda bi, hp: (2, bi, 0, hp)),
                  pl.BlockSpec((HEADS_PER_LANE_BLOCK, 2, tq, tq), lambda bi, hp: (hp, 0, 0, 0))],
        out_specs=pl.BlockSpec((1, s, V7X_LANES), lambda bi, hp: (bi, 0, hp)),
        out_shape=jax.ShapeDtypeStruct((b, s, d), BF16),
        scratch_shapes=[pltpu.VMEM((HEADS_PER_LANE_BLOCK, tq, s), F32),
                        pltpu.VMEM((HEADS_PER_LANE_BLOCK, s, V7X_LANES), BF16),
                        pltpu.VMEM((HEADS_PER_LANE_BLOCK, tq, V7X_LANES), F32),
                        pltpu.VMEM((HEADS_PER_LANE_BLOCK, tq, V7X_LANES), F32),
                        pltpu.VMEM((HEADS_PER_LANE_BLOCK, tq, V7X_LANES), F32)],
        compiler_params=_cparams(("arbitrary", "arbitrary"), 32),
        name="moba",
    )(rel_bias, qkv4, qkv4, qkv4, tiles)


def _whole(shape):
    return pl.BlockSpec(shape, lambda i: (0,) * len(shape), pipeline_mode=pl.Buffered(1))


def _ffn_kernel(o_ref, wo_ref, h_ref, g_ref, w1_ref, w3_ref, w2_ref, out_ref, *, chunks):
    x = h_ref[...] + _dot(o_ref[...], wo_ref[...])
    hn = _rmsnorm_f32(x, g_ref[...]).astype(BF16)
    acc = x
    for c0, c1 in chunks:
        a = _dot(hn, w1_ref[:, c0:c1])
        b = _dot(hn, w3_ref[:, c0:c1])
        acc = acc + _dot((_silu(a) * b).astype(BF16), w2_ref[c0:c1, :])
    out_ref[...] = acc


def oproj_dense_ffn(o, w_o, h, g, w1, w3, w2, tm, chunk):
    t, d = h.shape
    f = w1.shape[1]
    chunks = tuple((c, min(c + chunk, f)) for c in range(0, f, chunk))
    return pl.pallas_call(
        functools.partial(_ffn_kernel, chunks=chunks),
        grid=(t // tm,),
        in_specs=[pl.BlockSpec((tm, d), lambda i: (i, 0)),
                  _whole((d, d)),
                  pl.BlockSpec((tm, d), lambda i: (i, 0)),
                  pl.BlockSpec((1, d), lambda i: (0, 0)),
                  _whole((d, f)), _whole((d, f)), _whole((f, d))],
        out_specs=pl.BlockSpec((tm, d), lambda i: (i, 0)),
        out_shape=jax.ShapeDtypeStruct((t, d), F32),
        compiler_params=_cparams(("arbitrary",), 56),
        name="oproj_dense_ffn",
    )(o, w_o, h, g.reshape(1, d), w1, w3, w2)


def _row_tile_chunks(d):
    assert d % V7X_LANES == 0
    return d // V7X_LANES


def _store_row_tiles(ref, x, row0=0):
    n, d = x.shape
    c = _row_tile_chunks(d)
    for k in range(c):
        ref[pl.ds(row0 * c + k, n, stride=c), :] = x[:, k * V7X_LANES:(k + 1) * V7X_LANES]


def _load_row_tiles(ref, n, d):
    c = _row_tile_chunks(d)
    return [ref[pl.ds(k, n, stride=c), :] for k in range(c)]


COL_E0, COL_E1, COL_W0, COL_W1, COL_R0, COL_R1 = range(6)


def _router_kernel(o_ref, wo_ref, h_ref, g_ref, wr_ref, h1_ref, hn_ref, slab_ref, cnt_ref, carry_ref, *, tm):
    i = pl.program_id(0)

    @pl.when(i == 0)
    def _():
        carry_ref[...] = jnp.zeros_like(carry_ref)

    rs = min(ROUTER_ROWS, tm)
    assert tm % rs == 0
    g = g_ref[...]
    w = wr_ref[...]
    w_hi = w.astype(BF16)
    w_lo = (w - w_hi.astype(F32)).astype(BF16)
    lane = lax.broadcasted_iota(jnp.int32, (rs, V7X_LANES), 1)
    earlier = (lax.broadcasted_iota(jnp.int32, (rs, rs), 1)
               < lax.broadcasted_iota(jnp.int32, (rs, rs), 0)).astype(BF16)
    neg_inf = jnp.float32(-jnp.inf)
    count = [carry_ref[0:1, :]]

    def project(st, r):
        rows = pl.ds(r * rs, rs)
        h1 = h_ref[rows, :] + _dot(o_ref[rows, :], wo_ref[...])
        h1_ref[rows, :] = h1
        st["h1"] = h1

    def normalise(st, r):
        hn = _rmsnorm_f32(st.pop("h1"), g)
        _store_row_tiles(hn_ref, hn, r * rs)
        st["hi"] = hn.astype(BF16)
        st["lo"] = (hn - st["hi"].astype(F32)).astype(BF16)

    def logits(st, r):
        hi, lo = st.pop("hi"), st.pop("lo")
        st["logits"] = _dot(hi, w_hi) + _dot(hi, w_lo) + _dot(lo, w_hi)

    def top2(st, r):
        lg = jnp.where(lane < N_EXPERTS, st.pop("logits"), neg_inf)
        rank_of = jnp.zeros((rs, V7X_LANES), F32)
        for s in range(1, N_EXPERTS):
            lower = pltpu.roll(lg, s, axis=1)
            higher = pltpu.roll(lg, V7X_LANES - s, axis=1)
            rank_of = rank_of + jnp.where(lower >= lg, 1.0, 0.0) + jnp.where(higher > lg, 1.0, 0.0)
        first = (rank_of == 0.0) & (lane < N_EXPERTS)
        second = (rank_of == 1.0) & (lane < N_EXPERTS)
        pick = lambda sel, x: jnp.sum(jnp.where(sel, x, 0.0), axis=1, keepdims=True)
        lane_f = lane.astype(F32)
        m0, m1 = pick(first, lg), pick(second, lg)
        i0, i1 = pick(first, lane_f).astype(jnp.int32), pick(second, lane_f).astype(jnp.int32)
        e = jnp.exp(m1 - m0)
        st.update(i0=i0, i1=i1, w0=1.0 / (1.0 + e), w1=e / (1.0 + e),
                  hot=((lane == i0) | (lane == i1)).astype(BF16))

    def rank(st, r):
        hot = st.pop("hot")
        before = _dot(earlier, hot) + count[0]
        i0, i1 = st.pop("i0"), st.pop("i1")
        r0 = jnp.sum(jnp.where(lane == i0, before, 0.0), axis=1, keepdims=True)
        r1 = jnp.sum(jnp.where(lane == i1, before, 0.0), axis=1, keepdims=True)
        count[0] = count[0] + jnp.sum(hot.astype(F32), axis=0, keepdims=True)
        slab = jnp.zeros((rs, V7X_LANES), F32)
        for c, val in ((COL_E0, i0.astype(F32)), (COL_E1, i1.astype(F32)), (COL_W0, st.pop("w0")),
                       (COL_W1, st.pop("w1")), (COL_R0, r0), (COL_R1, r1)):
            slab = jnp.where(lane == c, val, slab)
        slab_ref[pl.ds(r * rs, rs), :] = slab

    _software_pipeline([(r,) for r in range(tm // rs)], [project, normalise, logits, top2, rank])
    carry_ref[0:1, :] = count[0]
    cnt_ref[...] = carry_ref[...]


def oproj_moe_router(o, w_o, h, g, w_router, tm):
    t, d = h.shape
    wr = jnp.zeros((d, V7X_LANES), F32).at[:, :N_EXPERTS].set(w_router)
    return pl.pallas_call(
        functools.partial(_router_kernel, tm=tm),
        grid=(t // tm,),
        in_specs=[pl.BlockSpec((tm, d), lambda i: (i, 0)),
                  _whole((d, d)),
                  pl.BlockSpec((tm, d), lambda i: (i, 0)),
                  pl.BlockSpec((1, d), lambda i: (0, 0)),
                  pl.BlockSpec((d, V7X_LANES), lambda i: (0, 0))],
        out_specs=[pl.BlockSpec((tm, d), lambda i: (i, 0)),
                   pl.BlockSpec((tm * d // V7X_LANES, V7X_LANES), lambda i: (i, 0)),
                   pl.BlockSpec((tm, V7X_LANES), lambda i: (i, 0)),
                   pl.BlockSpec((8, V7X_LANES), lambda i: (0, 0))],
        out_shape=[jax.ShapeDtypeStruct((t, d), F32),
                   jax.ShapeDtypeStruct((t * d // V7X_LANES, V7X_LANES), F32),
                   jax.ShapeDtypeStruct((t, V7X_LANES), F32),
                   jax.ShapeDtypeStruct((8, V7X_LANES), F32)],
        scratch_shapes=[pltpu.VMEM((8, V7X_LANES), F32)],
        compiler_params=_cparams(("arbitrary",), 40),
        name="oproj_moe_router",
    )(o, w_o, h, g.reshape(1, d), wr)


SCALAR_UNROLL = 16


def _sorted_list_kernel(beg_ref, end_ref, d0_ref, d1_ref, out_ref, *, chunk, t, tm, rows):
    i = pl.program_id(0)

    @pl.when(i == 0)
    def _():
        for e in range(N_EXPERTS + 1):
            beg, end = beg_ref[e], end_ref[e]
            n = end - beg

            def fill_group(g, carry, end=end):
                p0 = end - SCALAR_UNROLL * (g + 1)
                r0 = lax.rem(p0, tm)
                for k in range(SCALAR_UNROLL):
                    out_ref[p0 + k] = 2 * t + r0 + k
                return carry
            lax.fori_loop(0, n // SCALAR_UNROLL, fill_group, 0)

            def fill_one(p, carry):
                out_ref[p] = 2 * t + lax.rem(p, tm)
                return carry
            lax.fori_loop(beg, beg + lax.rem(n, SCALAR_UNROLL), fill_one, 0)

    base = i * chunk

    def body(g, carry):
        for k in range(SCALAR_UNROLL):
            j = g * SCALAR_UNROLL + k
            out_ref[d0_ref[0, 0, j]] = base + j
            out_ref[d1_ref[0, 0, j]] = t + base + j
        return carry
    lax.fori_loop(0, chunk // SCALAR_UNROLL, body, 0)


def sorted_assignment_list(dest0, dest1, pad_begin, pad_end, t, tm, rows):
    chunk = min(t, 2048)
    assert t % chunk == 0 and chunk % SCALAR_UNROLL == 0 and tm % SCALAR_UNROLL == 0
    blocked = lambda a: a.reshape(t // chunk, 1, chunk)
    dest_spec = pl.BlockSpec((1, 1, chunk), lambda i, beg, end: (i, 0, 0), memory_space=pltpu.SMEM)
    grid_spec = pltpu.PrefetchScalarGridSpec(
        num_scalar_prefetch=2,
        grid=(t // chunk,),
        in_specs=[dest_spec, dest_spec],
        out_specs=pl.BlockSpec(memory_space=pltpu.SMEM),
    )
    return pl.pallas_call(
        functools.partial(_sorted_list_kernel, chunk=chunk, t=t, tm=tm, rows=rows),
        grid_spec=grid_spec,
        out_shape=jax.ShapeDtypeStruct((rows,), jnp.int32),
        compiler_params=_cparams(("arbitrary",), 16),
        name="sorted_assignment_list",
    )(pad_begin, pad_end, blocked(dest0), blocked(dest1))


def _moe_kernel(te_ref, nu_ref, tokc_ref, tokn_ref, dst_ref, hn_hbm, w1_ref, w3_ref, w2_ref, y_hbm,
                xbuf, xb16, acc_ref, ybuf, gsem, ssem, *, tm, nf, nt, n_real):
    i = pl.program_id(0)
    j = pl.program_id(1)
    nu = nu_ref[0]
    slot = i % 2
    d = acc_ref.shape[1]
    c = _row_tile_chunks(d)

    def row_loop(start_row_copy):
        for r in range(tm):
            start_row_copy(r)

    def row(r):
        return pl.ds(r * c, c) if isinstance(r, int) else pl.ds(pl.multiple_of(r * c, c), c)

    def start_gather(tok_ref, s):
        row_loop(lambda r: pltpu.make_async_copy(
            hn_hbm.at[row(tok_ref[0, 0, r]), :], xbuf.at[s, row(r), :], gsem.at[s]).start())

    def wait_gather(s):
        pltpu.make_async_copy(hn_hbm.at[pl.ds(0, tm * c), :], xbuf.at[s], gsem.at[s]).wait()

    def start_scatter():
        row_loop(lambda r: pltpu.make_async_copy(
            ybuf.at[row(r), :], y_hbm.at[row(dst_ref[0, 0, r]), :], ssem.at[0]).start())

    def spare_rows_copy():
        return pltpu.make_async_copy(ybuf, y_hbm.at[pl.ds(n_real * c, tm * c), :], ssem.at[0])

    def wait_scatter():
        spare_rows_copy().wait()

    @pl.when((j == 0) & (i == 0))
    def _():
        ybuf[...] = jnp.zeros_like(ybuf)
        spare_rows_copy().start()
        spare_rows_copy().wait()

        @pl.when(nu > 0)
        def _():
            start_gather(tokc_ref, 0)

    @pl.when((j == 0) & (i < nu))
    def _():
        wait_gather(slot)
        for k, chunk in enumerate(_load_row_tiles(xbuf.at[slot], tm, d)):
            xb16[:, k * V7X_LANES:(k + 1) * V7X_LANES] = chunk.astype(BF16)

    @pl.when((j == 0) & (i + 1 < nu))
    def _():
        start_gather(tokn_ref, 1 - slot)

    @pl.when(i < nu)
    def _():
        x = xb16[...]
        a = _dot(x, w1_ref[0])
        b = _dot(x, w3_ref[0])
        part = _dot((_silu(a) * b).astype(BF16), w2_ref[0])

        @pl.when(j == 0)
        def _():
            acc_ref[...] = part

        @pl.when(j > 0)
        def _():
            acc_ref[...] += part

    last = j == nf - 1

    @pl.when(last & (i >= 1) & (i - 1 < nu))
    def _():
        wait_scatter()

    @pl.when(last & (i < nu))
    def _():
        _store_row_tiles(ybuf, acc_ref[...])
        start_scatter()

    @pl.when(last & (i == nt - 1) & (i < nu))
    def _():
        wait_scatter()


def moe_experts(hn, slab, counts, w1, w3, w2, tm, tf):
    t = slab.shape[0]
    d = w1.shape[1]
    c = _row_tile_chunks(d)
    f = w1.shape[2]
    nf = f // tf
    nt = (2 * t) // tm + N_EXPERTS
    rows = nt * tm

    e0 = slab[:, COL_E0].astype(jnp.int32)
    e1 = slab[:, COL_E1].astype(jnp.int32)
    r0 = slab[:, COL_R0].astype(jnp.int32)
    r1 = slab[:, COL_R1].astype(jnp.int32)
    cnt = counts[0, :N_EXPERTS].astype(jnp.int32)
    tiles = (cnt + tm - 1) // tm
    tile_end = jnp.cumsum(tiles)
    tile_start = tile_end - tiles
    nu = tile_end[-1:]
    offs = tile_start * tm
    dest0 = offs[e0] + r0
    dest1 = offs[e1] + r1
    tile_expert = jnp.minimum(
        jnp.sum(jnp.arange(nt, dtype=jnp.int32)[:, None] >= tile_end[None, :], axis=1), N_EXPERTS - 1
    ).astype(jnp.int32)
    pad_begin = jnp.concatenate([offs + cnt, nu * tm]).astype(jnp.int32)
    pad_end = jnp.concatenate([tile_end * tm, jnp.full((1,), rows, jnp.int32)]).astype(jnp.int32)
    dst_sorted = sorted_assignment_list(dest0, dest1, pad_begin, pad_end, t, tm, rows)
    tok_sorted = jnp.where(dst_sorted >= 2 * t, 0, jnp.where(dst_sorted >= t, dst_sorted - t, dst_sorted))
    tok3 = tok_sorted.reshape(nt, 1, tm)
    dst3 = dst_sorted.reshape(nt, 1, tm)

    def w_in(shape, which):
        def index_map(i, j, te, nu_):
            ii = jnp.minimum(i, nu_[0] - 1)
            jj = jnp.where(i < nu_[0], j, nf - 1)
            return (te[ii], 0, jj) if which == "up" else (te[ii], jj, 0)
        return pl.BlockSpec(shape, index_map)

    smem_tile = lambda f_: pl.BlockSpec((1, 1, tm), f_, memory_space=pltpu.SMEM)
    grid_spec = pltpu.PrefetchScalarGridSpec(
        num_scalar_prefetch=2,
        grid=(nt, nf),
        in_specs=[smem_tile(lambda i, j, te, nu_: (i, 0, 0)),
                  smem_tile(lambda i, j, te, nu_: (jnp.minimum(i + 1, nt - 1), 0, 0)),
                  smem_tile(lambda i, j, te, nu_: (i, 0, 0)),
                  pl.BlockSpec(memory_space=pl.ANY),
                  w_in((1, d, tf), "up"), w_in((1, d, tf), "up"), w_in((1, tf, d), "down")],
        out_specs=pl.BlockSpec(memory_space=pl.ANY),
        scratch_shapes=[pltpu.VMEM((2, tm * c, V7X_LANES), F32),
                        pltpu.VMEM((tm, d), BF16),
                        pltpu.VMEM((tm, d), F32),
                        pltpu.VMEM((tm * c, V7X_LANES), F32),
                        pltpu.SemaphoreType.DMA((2,)),
                        pltpu.SemaphoreType.DMA((1,))],
    )
    return pl.pallas_call(
        functools.partial(_moe_kernel, tm=tm, nf=nf, nt=nt, n_real=2 * t),
        grid_spec=grid_spec,
        out_shape=jax.ShapeDtypeStruct(((2 * t + tm) * c, V7X_LANES), F32),
        compiler_params=_cparams(("arbitrary", "arbitrary"), 56),
        name="moe_experts",
    )(tile_expert, nu, tok3, tok3, dst3, hn, w1, w3, w2)


def _combine_kernel(h_ref, y0_ref, y1_ref, slab_ref, g_ref, out_ref, *, final):
    slab = slab_ref[...]
    tm, d = h_ref.shape
    y0 = jnp.concatenate(_load_row_tiles(y0_ref, tm, d), axis=1)
    y1 = jnp.concatenate(_load_row_tiles(y1_ref, tm, d), axis=1)
    x = h_ref[...] + slab[:, COL_W0:COL_W0 + 1] * y0 + slab[:, COL_W1:COL_W1 + 1] * y1
    out_ref[...] = _rmsnorm_f32(x, g_ref[...]) if final else x


def moe_combine(h, y, slab, g, tm, final):
    t, d = h.shape
    nb = t // tm
    c = _row_tile_chunks(d)
    return pl.pallas_call(
        functools.partial(_combine_kernel, final=final),
        grid=(nb,),
        in_specs=[pl.BlockSpec((tm, d), lambda i: (i, 0)),
                  pl.BlockSpec((tm * c, V7X_LANES), lambda i: (i, 0)),
                  pl.BlockSpec((tm * c, V7X_LANES), lambda i: (i + nb, 0)),
                  pl.BlockSpec((tm, V7X_LANES), lambda i: (i, 0)),
                  pl.BlockSpec((1, d), lambda i: (0, 0))],
        out_specs=pl.BlockSpec((tm, d), lambda i: (i, 0)),
        out_shape=jax.ShapeDtypeStruct((t, d), F32),
        compiler_params=_cparams(("arbitrary",), 40),
        name="moe_combine",
    )(h, y, y, slab, g.reshape(1, d))


def _norm_kernel(h_ref, g_ref, out_ref):
    out_ref[...] = _rmsnorm_f32(h_ref[...], g_ref[...])


def final_norm_only(h, g, tm):
    t, d = h.shape
    return pl.pallas_call(
        _norm_kernel,
        grid=(t // tm,),
        in_specs=[pl.BlockSpec((tm, d), lambda i: (i, 0)), pl.BlockSpec((1, d), lambda i: (0, 0))],
        out_specs=pl.BlockSpec((tm, d), lambda i: (i, 0)),
        out_shape=jax.ShapeDtypeStruct((t, d), F32),
        compiler_params=_cparams(("arbitrary",), 40),
        name="final_norm",
    )(h, g.reshape(1, d))


V7X_MXU_DIM = 256


def _tile_sizes(t, f_expert):
    def rows(want):
        tm = min(want, t)
        assert t % tm == 0
        return tm
    half = f_expert // 2
    return dict(qkv=rows(1024), tokens=rows(512), ffn_chunk=1024,
                expert_chunk=half if half % V7X_MXU_DIM == 0 else f_expert)


def kernel(x, w_qkv, w_o, mixer_norm, ffn_norm, rel_bias, w1, w3, w2, router, e_w1, e_w3, e_w2, final_norm):
    b, s, d = x.shape
    assert d == N_HEADS * HEAD_DIM and s % MOBA_BLOCK == 0
    t = b * s
    depth = w_qkv.shape[0]
    h = x.reshape(t, d)
    tiles = _tile_sizes(t, e_w1.shape[-1])
    tm_big, tm_mid = tiles["qkv"], tiles["tokens"]
    normed = False
    for i in range(depth):
        qkv = qkv_proj(h, mixer_norm[i], w_qkv[i].astype(BF16), tm_big)
        if i % 2 == 0:
            o = stick_breaking_attention(qkv, b, s, MOBA_BLOCK)
        else:
            o = moba_attention(qkv, rel_bias, b, s)
        o = o.reshape(t, d)
        wo = w_o[i].astype(BF16)
        jj = i // 2
        if i % 2 == 0:
            h = oproj_dense_ffn(o, wo, h, ffn_norm[i], w1[jj].astype(BF16), w3[jj].astype(BF16),
                                w2[jj].astype(BF16), tm_mid, tiles["ffn_chunk"])
        else:
            h, hn, slab, counts = oproj_moe_router(o, wo, h, ffn_norm[i], router[jj], tm_mid)
            y = moe_experts(hn, slab, counts, e_w1[jj].astype(BF16), e_w3[jj].astype(BF16),
                            e_w2[jj].astype(BF16), tm_mid, tiles["expert_chunk"])
            last = i == depth - 1
            h = moe_combine(h, y, slab, final_norm if last else ffn_norm[i], tm_mid, last)
            normed = last
    if not normed:
        h = final_norm_only(h, final_norm, tm_mid)
    return h.reshape(b, s, d)
```

```python
import functools
import math

import jax
import jax.numpy as jnp
from jax import lax
from jax.experimental import pallas as pl
from jax.experimental.pallas import tpu as pltpu

N_HEADS = 16
HEAD_DIM = 64
MOBA_BLOCK = 256
MOBA_TOPK = 3
N_BUCKETS = 32
MAX_DISTANCE = 128
N_EXPERTS = 8
RMS_EPS = 1e-6
NEG_BIG = -1e30

V7X_LANES = 128
SB_GROUP = 4
MOBA_GROUP = 4
ROUTER_ROWS = 128
V7X_VMEM_BYTES = 64 * 1024 * 1024
HEADS_PER_LANE_BLOCK = V7X_LANES // HEAD_DIM

F32 = jnp.float32
BF16 = jnp.bfloat16


def _cparams(semantics, vmem_mb):
    assert vmem_mb * 1024 * 1024 < V7X_VMEM_BYTES
    return pltpu.CompilerParams(dimension_semantics=semantics,
                                vmem_limit_bytes=vmem_mb * 1024 * 1024)


def _rmsnorm_f32(x, g):
    return x * lax.rsqrt(jnp.mean(x * x, axis=-1, keepdims=True) + RMS_EPS) * g


def _silu(a):
    return a * (1.0 / (1.0 + jnp.exp(-a)))


def _dot(a, b):
    return jnp.dot(a, b, preferred_element_type=F32)


def _software_pipeline(tasks, stages):
    state = [dict() for _ in tasks]
    for step in range(len(tasks) + len(stages) - 1):
        for k in reversed(range(len(stages))):
            t = step - k
            if 0 <= t < len(tasks):
                stages[k](state[t], *tasks[t])


def _dot_nt(a, b):
    return lax.dot_general(a, b, (((1,), (1,)), ((), ())), preferred_element_type=F32)


def _qkv_kernel(x_ref, g_ref, w_ref, o_ref, hn_ref):
    j = pl.program_id(1)

    @pl.when(j == 0)
    def _():
        hn_ref[...] = _rmsnorm_f32(x_ref[...], g_ref[...]).astype(BF16)

    scale = jnp.where(j == 0, HEAD_DIM ** -0.5, 1.0)
    o_ref[0] = (_dot(hn_ref[...], w_ref[...]) * scale).astype(o_ref.dtype)


def qkv_proj(h, g, w_bf16, tm):
    t, d = h.shape
    return pl.pallas_call(
        _qkv_kernel,
        grid=(t // tm, 3),
        in_specs=[pl.BlockSpec((tm, d), lambda i, j: (i, 0)),
                  pl.BlockSpec((1, d), lambda i, j: (0, 0)),
                  pl.BlockSpec((d, d), lambda i, j: (0, j))],
        out_specs=pl.BlockSpec((1, tm, d), lambda i, j: (j, i, 0)),
        out_shape=jax.ShapeDtypeStruct((3, t, d), BF16),
        scratch_shapes=[pltpu.VMEM((tm, d), BF16)],
        compiler_params=_cparams(("arbitrary", "arbitrary"), 40),
        name="qkv_proj",
    )(h, g.reshape(1, d), w_bf16)


def _sb_kernel(q_ref, k_ref, v_ref, o_ref, acc0_ref, acc1_ref, r0_ref, r1_ref, u_ref, *, tq):
    row = lax.broadcasted_iota(jnp.int32, (tq, tq), 0)
    col = lax.broadcasted_iota(jnp.int32, (tq, tq), 1)
    u_ref[...] = -(row > col).astype(BF16)

    def q_tile(qi, carry):
        _sb_tile(qi, q_ref, k_ref, v_ref, o_ref, acc0_ref, acc1_ref, r0_ref, r1_ref, u_ref, tq=tq)
        return carry

    lax.fori_loop(0, q_ref.shape[2] // tq, q_tile, 0)


def _sb_tile(qi, q_ref, k_ref, v_ref, o_ref, acc0_ref, acc1_ref, r0_ref, r1_ref, u_ref, *, tq):
    acc_ref = (acc0_ref, acc1_ref)
    r_ref = (r0_ref, r1_ref)
    row = lax.broadcasted_iota(jnp.int32, (tq, tq), 0)
    col = lax.broadcasted_iota(jnp.int32, (tq, tq), 1)
    tile_rows = pl.ds(pl.multiple_of(qi * tq, tq), tq)
    q = q_ref[0, 0, tile_rows, :]
    lane = lax.broadcasted_iota(jnp.int32, (tq, V7X_LANES), 1)
    past = col < row
    heads = range(HEADS_PER_LANE_BLOCK)
    qms = [jnp.where(lane // HEAD_DIM == h, q, jnp.zeros_like(q)) for h in heads]

    def span(first, nb, diagonal):
        tasks = [(i, h) for i in reversed(range(nb)) for h in heads]
        state = [dict() for _ in tasks]
        r = [r_ref[h][...] for h in heads]
        acc = [None for h in heads]

        def stage(k, st, i, h):
            on_diagonal = diagonal and i == nb - 1
            start = pl.multiple_of((first + i) * tq, tq)
            if k == 0:
                st["z"] = _dot_nt(qms[h], k_ref[0, 0, pl.ds(start, tq), :])
            elif k == 1:
                z = st.pop("z")
                zb = z.astype(BF16)
                t = jnp.log(1 + jnp.exp(-jnp.abs(zb)))
                sp = jnp.maximum(zb, 0) + t
                if on_diagonal:
                    sp = jnp.where(past, sp, jnp.zeros_like(sp))
                st["sp"] = sp
                st["log_sig"] = jnp.minimum(z, 0.0) - t.astype(F32)
            elif k == 2:
                st["after"] = _dot(st["sp"], u_ref[...])
            elif k == 3:
                tot = st.pop("after") + r[h]
                w = jnp.exp(st.pop("log_sig") + tot)
                if on_diagonal:
                    w = jnp.where(past, w, 0.0)
                r[h] = tot[:, 0:1] - st.pop("sp")[:, 0:1].astype(F32)
                st["w"] = w.astype(BF16)
            else:
                part = _dot(st.pop("w"), v_ref[0, 0, pl.ds(start, tq), :])
                acc[h] = part if acc[h] is None else acc[h] + part

        n_stages = 5
        for step in range(len(tasks) + n_stages - 1):
            for k in reversed(range(n_stages)):
                ti = step - k
                if 0 <= ti < len(tasks):
                    stage(k, state[ti], *tasks[ti])
        for h in heads:
            r_ref[h][...] = r[h]
            acc_ref[h][...] += acc[h]

    for h in heads:
        r_ref[h][...] = jnp.zeros((tq, 1), F32)
        acc_ref[h][...] = jnp.zeros((tq, V7X_LANES), F32)

    n_blocks = qi + 1
    n_full = jnp.maximum(n_blocks // SB_GROUP - 1, 0)
    diag_size = n_blocks - n_full * SB_GROUP
    for size in range(1, 2 * SB_GROUP):
        @pl.when(diag_size == size)
        def _(size=size):
            span(n_blocks - size, size, True)

    def body(s, carry):
        span((n_full - 1 - s) * SB_GROUP, SB_GROUP, False)
        return carry

    lax.fori_loop(0, n_full, body, 0)

    o_ref[0, tile_rows, :] = jnp.where(lane // HEAD_DIM == 0, acc0_ref[...], acc1_ref[...]).astype(o_ref.dtype)


def stick_breaking_attention(qkv, b, s, tq):
    d = qkv.shape[-1]
    qkv4 = qkv.reshape(3, b, s, d)
    plane = lambda p: pl.BlockSpec((1, 1, s, V7X_LANES), lambda bi, hp: (p, bi, 0, hp))
    return pl.pallas_call(
        functools.partial(_sb_kernel, tq=tq),
        grid=(b, d // V7X_LANES),
        in_specs=[plane(0), plane(1), plane(2)],
        out_specs=pl.BlockSpec((1, s, V7X_LANES), lambda bi, hp: (bi, 0, hp)),
        out_shape=jax.ShapeDtypeStruct((b, s, d), BF16),
        scratch_shapes=[pltpu.VMEM((tq, V7X_LANES), F32), pltpu.VMEM((tq, V7X_LANES), F32),
                        pltpu.VMEM((tq, 1), F32), pltpu.VMEM((tq, 1), F32),
                        pltpu.VMEM((tq, tq), BF16)],
        compiler_params=_cparams(("arbitrary", "arbitrary"), 32),
        name="stick_breaking",
    )(qkv4, qkv4, qkv4)


def _t5_bias_kernel(rb_ref, o_ref, *, tq):
    h = pl.program_id(0)
    row = lax.broadcasted_iota(jnp.int32, (tq, tq), 0)
    col = lax.broadcasted_iota(jnp.int32, (tq, tq), 1)
    max_exact = N_BUCKETS // 2
    for o in range(2):
        dist = o * tq + row - col
        n = jnp.maximum(dist, 0)
        nf = jnp.maximum(n, 1).astype(F32)
        scaled = jnp.log(nf / max_exact) / math.log(MAX_DISTANCE / max_exact) * (N_BUCKETS - max_exact)
        large = max_exact + jnp.where(scaled < 0, jnp.ceil(scaled), jnp.floor(scaled)).astype(jnp.int32)
        large = jnp.minimum(large, N_BUCKETS - 1)
        bucket = jnp.where(n < max_exact, n, large)
        bias = jnp.zeros((tq, tq), F32)
        for bkt in range(N_BUCKETS):
            bias = jnp.where(bucket == bkt, rb_ref[bkt, h], bias)
        if o == 0:
            bias = jnp.where(dist >= 0, bias, NEG_BIG)
        o_ref[0, o] = bias


def t5_bias_tiles(rel_bias, tq):
    return pl.pallas_call(
        functools.partial(_t5_bias_kernel, tq=tq),
        grid=(N_HEADS,),
        in_specs=[pl.BlockSpec(memory_space=pltpu.SMEM)],
        out_specs=pl.BlockSpec((1, 2, tq, tq), lambda h: (h, 0, 0, 0)),
        out_shape=jax.ShapeDtypeStruct((N_HEADS, 2, tq, tq), F32),
        compiler_params=_cparams(("arbitrary",), 16),
        name="t5_bias_tiles",
    )(rel_bias)


def _moba_aux_lane0(h):
    assert HEADS_PER_LANE_BLOCK == 2
    return ((h + 1) % HEADS_PER_LANE_BLOCK) * HEAD_DIM


def _moba_kernel(rb_ref, qall_ref, k_ref, v_ref, bt_ref, o_ref,
                 s_ref, qaux_ref, mx_ref, l_ref, acc_ref, *, tq, nblk):
    heads = range(HEADS_PER_LANE_BLOCK)
    s_len = nblk * tq
    n_sel = min(MOBA_TOPK, nblk - 1)
    big = jnp.asarray(-NEG_BIG, BF16).astype(F32)
    aux_lane0 = _moba_aux_lane0
    assert nblk + 2 <= HEAD_DIM

    def block_choice():
        km = jnp.concatenate(
            [jnp.mean(k_ref[0, 0, n * tq:(n + 1) * tq, :].astype(F32), axis=0, keepdims=True)
             for n in range(nblk)], axis=0)
        km_hi = km.astype(BF16)
        km_lo = (km - km_hi.astype(F32)).astype(BF16)
        q_all = qall_ref[0, 0]
        lane_all = lax.broadcasted_iota(jnp.int32, (s_len, V7X_LANES), 1)
        blk = lax.broadcasted_iota(jnp.int32, (nblk, s_len), 0)
        own = lax.broadcasted_iota(jnp.int32, (nblk, s_len), 1) // tq
        own_row = own[0:1, :]
        for h in heads:
            qm = jnp.where(lane_all // HEAD_DIM == h, q_all, jnp.zeros_like(q_all))
            gate = _dot_nt(km_hi, qm) + _dot_nt(km_lo, qm)
            gate = jnp.where(blk < own, gate, NEG_BIG)
            rows = [gate[n:n + 1, :] for n in range(nblk)]
            rank = [jnp.zeros((1, s_len), F32) for _ in range(nblk)]
            for n in range(nblk):
                for m in range(n):
                    m_wins = jnp.where(rows[m] >= rows[n], 1.0, 0.0)
                    rank[n] = rank[n] + m_wins
                    rank[m] = rank[m] + (1.0 - m_wins)
            aux = [jnp.where((rank[n] < n_sel) & (n < own_row), 0.0, -big) for n in range(nblk)]
            aux += [jnp.ones((1, s_len), F32)] * 2
            before = jnp.zeros((aux_lane0(h), s_len), F32)
            after = jnp.zeros((V7X_LANES - aux_lane0(h) - len(aux), s_len), F32)
            pieces = ([before] if before.shape[0] else []) + aux + [after]
            qaux_ref[h] = jnp.concatenate(pieces, axis=0).T.astype(BF16)

    block_choice()

    def q_tile(qi, carry):
        _moba_tile(qi, rb_ref, qall_ref, k_ref, v_ref, bt_ref, o_ref,
                   s_ref, qaux_ref, mx_ref, l_ref, acc_ref, tq=tq, nblk=nblk)
        return carry

    lax.fori_loop(0, nblk, q_tile, 0)


def _moba_tile(qi, rb_ref, qall_ref, k_ref, v_ref, bt_ref, o_ref,
               s_ref, qaux_ref, mx_ref, l_ref, acc_ref, *, tq, nblk):
    hp = pl.program_id(1)
    heads = range(HEADS_PER_LANE_BLOCK)
    aux_lane0 = _moba_aux_lane0

    tile_rows = pl.ds(pl.multiple_of(qi * tq, tq), tq)
    q = qall_ref[0, 0, tile_rows, :]
    lane = lax.broadcasted_iota(jnp.int32, (tq, V7X_LANES), 1)
    q_aug = [jnp.where(lane // HEAD_DIM == h, q, qaux_ref[h, tile_rows, :]) for h in heads]
    both = lambda x: jnp.concatenate([x, x], axis=1)
    halves = lambda x: (x[:, :V7X_LANES], x[:, V7X_LANES:])
    for h in heads:
        mx_ref[h] = jnp.full((tq, V7X_LANES), NEG_BIG, F32)

    aux_rows = 16
    lane_aux = lax.broadcasted_iota(jnp.int32, (aux_rows, V7X_LANES), 1)
    key_aux_far = []
    for h in heads:
        far = jnp.full((aux_rows, V7X_LANES), rb_ref[N_BUCKETS - 1, hp * HEADS_PER_LANE_BLOCK + h], F32)
        far_hi = far.astype(BF16).astype(F32)
        far_lo = (far - far_hi).astype(BF16).astype(F32)
        key_aux_far.append(jnp.where(lane_aux == aux_lane0(h) + nblk, far_hi,
                                     jnp.where(lane_aux == aux_lane0(h) + nblk + 1, far_lo, 0.0)))

    def key_operand(h, n, role):
        start = pl.multiple_of(n * tq, tq)
        kb = k_ref[0, 0, pl.ds(start, tq), :]
        if role == "own":
            aux = jnp.zeros((aux_rows, V7X_LANES), F32)
        else:
            base = key_aux_far[h] if role == "far" else jnp.zeros((aux_rows, V7X_LANES), F32)
            aux = jnp.where(lane_aux == aux_lane0(h) + n, 1.0, base)
        aux = jnp.concatenate([aux.astype(BF16)] * (tq // aux_rows), axis=0)
        return jnp.where(lane // HEAD_DIM == h, kb, aux)

    pipeline = _software_pipeline

    def logits_span(first, roles, last=False):
        mx = [mx_ref[h] for h in heads]

        def products(st, i, h):
            st["s"] = _dot_nt(q_aug[h], key_operand(h, first + i, roles[i]))

        def finish(st, i, h):
            start = pl.multiple_of((first + i) * tq, tq)
            s = st.pop("s")
            if roles[i] == "own":
                s = s + bt_ref[h, 0]
            elif roles[i] == "prev":
                s = s + bt_ref[h, 1]
            s_ref[h, :, pl.ds(start, tq)] = s
            s_lo, s_hi = halves(s)
            mx[h] = jnp.maximum(mx[h], jnp.maximum(s_lo, s_hi))

        pipeline([(i, h) for i in range(len(roles)) for h in heads], [products, finish])
        for h in heads:
            if last:
                mx_ref[h] = jnp.broadcast_to(jnp.max(mx[h], axis=1, keepdims=True), (tq, V7X_LANES))
                l_ref[h] = jnp.zeros((tq, V7X_LANES), F32)
                acc_ref[h] = jnp.zeros((tq, V7X_LANES), F32)
            else:
                mx_ref[h] = mx[h]

    def probs_span(first, nb, last=False):
        m = [both(mx_ref[h]) for h in heads]
        l = [l_ref[h] for h in heads]
        acc = [acc_ref[h] for h in heads]

        def exponentials(st, i, h):
            start = pl.multiple_of((first + i) * tq, tq)
            p = jnp.exp(s_ref[h, :, pl.ds(start, tq)] - m[h])
            p_lo, p_hi = halves(p)
            l[h] = l[h] + (p_lo + p_hi)
            st["p"] = p.astype(BF16)

        def accumulate(st, i, h):
            start = pl.multiple_of((first + i) * tq, tq)
            acc[h] = acc[h] + _dot(st.pop("p"), v_ref[0, 0, pl.ds(start, tq), :])

        pipeline([(i, h) for i in range(nb) for h in heads], [exponentials, accumulate])
        if last:
            out = [acc[h] / jnp.sum(l[h], axis=1, keepdims=True) for h in heads]
            o_ref[0, tile_rows, :] = jnp.where(lane // HEAD_DIM == 0, out[0], out[1]).astype(o_ref.dtype)
        else:
            for h in heads:
                l_ref[h] = l[h]
                acc_ref[h] = acc[h]

    group = MOBA_GROUP
    n_blocks = qi + 1
    n_full = jnp.maximum(n_blocks // group - 1, 0)
    last_size = n_blocks - n_full * group

    def far_groups(g, carry):
        logits_span(g * group, ["far"] * group)
        return carry

    lax.fori_loop(0, n_full, far_groups, 0)
    for size in range(1, 2 * group):
        @pl.when(last_size == size)
        def _(size=size):
            logits_span(n_blocks - size, (["far"] * (2 * group) + ["prev", "own"])[-size:]
                        if size >= 2 else ["own"], last=True)

    def prob_groups(g, carry):
        probs_span(g * group, group)
        return carry

    lax.fori_loop(0, n_full, prob_groups, 0)
    for size in range(1, 2 * group):
        @pl.when(last_size == size)
        def _(size=size):
            probs_span(n_blocks - size, size, last=True)


def moba_attention(qkv, rel_bias, b, s):
    d = qkv.shape[-1]
    tq = MOBA_BLOCK
    assert s % tq == 0 and tq >= 2 * MAX_DISTANCE
    nblk = s // tq
    qkv4 = qkv.reshape(3, b, s, d)
    tiles = t5_bias_tiles(rel_bias, tq)
    return pl.pallas_call(
        functools.partial(_moba_kernel, tq=tq, nblk=nblk),
        grid=(b, d // V7X_LANES),
        in_specs=[pl.BlockSpec(memory_space=pltpu.SMEM),
                  pl.BlockSpec((1, 1, s, V7X_LANES), lambda bi, hp: (0, bi, 0, hp)),
                  pl.BlockSpec((1, 1, s, V7X_LANES), lambda bi, hp: (1, bi, 0, hp)),
                  pl.BlockSpec((1, 1, s, V7X_LANES), lambda bi, hp: (2, bi, 0, hp)),
                  pl.BlockSpec((HEADS_PER_LANE_BLOCK, 2, tq, tq), lambda bi, hp: (hp, 0, 0, 0))],
        out_specs=pl.BlockSpec((1, s, V7X_LANES), lambda bi, hp: (bi, 0, hp)),
        out_shape=jax.ShapeDtypeStruct((b, s, d), BF16),
        scratch_shapes=[pltpu.VMEM((HEADS_PER_LANE_BLOCK, tq, s), F32),
                        pltpu.VMEM((HEADS_PER_LANE_BLOCK, s, V7X_LANES), BF16),
                        pltpu.VMEM((HEADS_PER_LANE_BLOCK, tq, V7X_LANES), F32),
                        pltpu.VMEM((HEADS_PER_LANE_BLOCK, tq, V7X_LANES), F32),
                        pltpu.VMEM((HEADS_PER_LANE_BLOCK, tq, V7X_LANES), F32)],
        compiler_params=_cparams(("arbitrary", "arbitrary"), 32),
        name="moba",
    )(rel_bias, qkv4, qkv4, qkv4, tiles)


def _whole(shape):
    return pl.BlockSpec(shape, lambda i: (0,) * len(shape), pipeline_mode=pl.Buffered(1))


def _ffn_kernel(o_ref, wo_ref, h_ref, g_ref, w1_ref, w3_ref, w2_ref, out_ref, *, chunks):
    x = h_ref[...] + _dot(o_ref[...], wo_ref[...])
    hn = _rmsnorm_f32(x, g_ref[...]).astype(BF16)
    acc = x
    for c0, c1 in chunks:
        a = _dot(hn, w1_ref[:, c0:c1])
        b = _dot(hn, w3_ref[:, c0:c1])
        acc = acc + _dot((_silu(a) * b).astype(BF16), w2_ref[c0:c1, :])
    out_ref[...] = acc


def oproj_dense_ffn(o, w_o, h, g, w1, w3, w2, tm, chunk):
    t, d = h.shape
    f = w1.shape[1]
    chunks = tuple((c, min(c + chunk, f)) for c in range(0, f, chunk))
    return pl.pallas_call(
        functools.partial(_ffn_kernel, chunks=chunks),
        grid=(t // tm,),
        in_specs=[pl.BlockSpec((tm, d), lambda i: (i, 0)),
                  _whole((d, d)),
                  pl.BlockSpec((tm, d), lambda i: (i, 0)),
                  pl.BlockSpec((1, d), lambda i: (0, 0)),
                  _whole((d, f)), _whole((d, f)), _whole((f, d))],
        out_specs=pl.BlockSpec((tm, d), lambda i: (i, 0)),
        out_shape=jax.ShapeDtypeStruct((t, d), F32),
        compiler_params=_cparams(("arbitrary",), 56),
        name="oproj_dense_ffn",
    )(o, w_o, h, g.reshape(1, d), w1, w3, w2)


def _row_tile_chunks(d):
    assert d % V7X_LANES == 0
    return d // V7X_LANES


def _store_row_tiles(ref, x, row0=0):
    n, d = x.shape
    c = _row_tile_chunks(d)
    for k in range(c):
        ref[pl.ds(row0 * c + k, n, stride=c), :] = x[:, k * V7X_LANES:(k + 1) * V7X_LANES]


def _load_row_tiles(ref, n, d):
    c = _row_tile_chunks(d)
    return [ref[pl.ds(k, n, stride=c), :] for k in range(c)]


COL_E0, COL_E1, COL_W0, COL_W1, COL_R0, COL_R1 = range(6)


def _router_kernel(o_ref, wo_ref, h_ref, g_ref, wr_ref, h1_ref, hn_ref, slab_ref, cnt_ref, carry_ref, *, tm):
    i = pl.program_id(0)

    @pl.when(i == 0)
    def _():
        carry_ref[...] = jnp.zeros_like(carry_ref)

    rs = min(ROUTER_ROWS, tm)
    assert tm % rs == 0
    g = g_ref[...]
    w = wr_ref[...]
    w_hi = w.astype(BF16)
    w_lo = (w - w_hi.astype(F32)).astype(BF16)
    lane = lax.broadcasted_iota(jnp.int32, (rs, V7X_LANES), 1)
    earlier = (lax.broadcasted_iota(jnp.int32, (rs, rs), 1)
               < lax.broadcasted_iota(jnp.int32, (rs, rs), 0)).astype(BF16)
    neg_inf = jnp.float32(-jnp.inf)
    count = [carry_ref[0:1, :]]

    def project(st, r):
        rows = pl.ds(r * rs, rs)
        h1 = h_ref[rows, :] + _dot(o_ref[rows, :], wo_ref[...])
        h1_ref[rows, :] = h1
        st["h1"] = h1

    def normalise(st, r):
        hn = _rmsnorm_f32(st.pop("h1"), g)
        _store_row_tiles(hn_ref, hn, r * rs)
        st["hi"] = hn.astype(BF16)
        st["lo"] = (hn - st["hi"].astype(F32)).astype(BF16)

    def logits(st, r):
        hi, lo = st.pop("hi"), st.pop("lo")
        st["logits"] = _dot(hi, w_hi) + _dot(hi, w_lo) + _dot(lo, w_hi)

    def top2(st, r):
        lg = jnp.where(lane < N_EXPERTS, st.pop("logits"), neg_inf)
        rank_of = jnp.zeros((rs, V7X_LANES), F32)
        for s in range(1, N_EXPERTS):
            lower = pltpu.roll(lg, s, axis=1)
            higher = pltpu.roll(lg, V7X_LANES - s, axis=1)
            rank_of = rank_of + jnp.where(lower >= lg, 1.0, 0.0) + jnp.where(higher > lg, 1.0, 0.0)
        first = (rank_of == 0.0) & (lane < N_EXPERTS)
        second = (rank_of == 1.0) & (lane < N_EXPERTS)
        pick = lambda sel, x: jnp.sum(jnp.where(sel, x, 0.0), axis=1, keepdims=True)
        lane_f = lane.astype(F32)
        m0, m1 = pick(first, lg), pick(second, lg)
        i0, i1 = pick(first, lane_f).astype(jnp.int32), pick(second, lane_f).astype(jnp.int32)
        e = jnp.exp(m1 - m0)
        st.update(i0=i0, i1=i1, w0=1.0 / (1.0 + e), w1=e / (1.0 + e),
                  hot=((lane == i0) | (lane == i1)).astype(BF16))

    def rank(st, r):
        hot = st.pop("hot")
        before = _dot(earlier, hot) + count[0]
        i0, i1 = st.pop("i0"), st.pop("i1")
        r0 = jnp.sum(jnp.where(lane == i0, before, 0.0), axis=1, keepdims=True)
        r1 = jnp.sum(jnp.where(lane == i1, before, 0.0), axis=1, keepdims=True)
        count[0] = count[0] + jnp.sum(hot.astype(F32), axis=0, keepdims=True)
        slab = jnp.zeros((rs, V7X_LANES), F32)
        for c, val in ((COL_E0, i0.astype(F32)), (COL_E1, i1.astype(F32)), (COL_W0, st.pop("w0")),
                       (COL_W1, st.pop("w1")), (COL_R0, r0), (COL_R1, r1)):
            slab = jnp.where(lane == c, val, slab)
        slab_ref[pl.ds(r * rs, rs), :] = slab

    _software_pipeline([(r,) for r in range(tm // rs)], [project, normalise, logits, top2, rank])
    carry_ref[0:1, :] = count[0]
    cnt_ref[...] = carry_ref[...]


def oproj_moe_router(o, w_o, h, g, w_router, tm):
    t, d = h.shape
    wr = jnp.zeros((d, V7X_LANES), F32).at[:, :N_EXPERTS].set(w_router)
    return pl.pallas_call(
        functools.partial(_router_kernel, tm=tm),
        grid=(t // tm,),
        in_specs=[pl.BlockSpec((tm, d), lambda i: (i, 0)),
                  _whole((d, d)),
                  pl.BlockSpec((tm, d), lambda i: (i, 0)),
                  pl.BlockSpec((1, d), lambda i: (0, 0)),
                  pl.BlockSpec((d, V7X_LANES), lambda i: (0, 0))],
        out_specs=[pl.BlockSpec((tm, d), lambda i: (i, 0)),
                   pl.BlockSpec((tm * d // V7X_LANES, V7X_LANES), lambda i: (i, 0)),
                   pl.BlockSpec((tm, V7X_LANES), lambda i: (i, 0)),
                   pl.BlockSpec((8, V7X_LANES), lambda i: (0, 0))],
        out_shape=[jax.ShapeDtypeStruct((t, d), F32),
                   jax.ShapeDtypeStruct((t * d // V7X_LANES, V7X_LANES), F32),
                   jax.ShapeDtypeStruct((t, V7X_LANES), F32),
                   jax.ShapeDtypeStruct((8, V7X_LANES), F32)],
        scratch_shapes=[pltpu.VMEM((8, V7X_LANES), F32)],
        compiler_params=_cparams(("arbitrary",), 40),
        name="oproj_moe_router",
    )(o, w_o, h, g.reshape(1, d), wr)


SCALAR_UNROLL = 16


def _sorted_list_kernel(beg_ref, end_ref, d0_ref, d1_ref, out_ref, *, chunk, t, tm, rows):
    i = pl.program_id(0)

    @pl.when(i == 0)
    def _():
        for e in range(N_EXPERTS + 1):
            beg, end = beg_ref[e], end_ref[e]
            n = end - beg

            def fill_group(g, carry, end=end):
                p0 = end - SCALAR_UNROLL * (g + 1)
                r0 = lax.rem(p0, tm)
                for k in range(SCALAR_UNROLL):
                    out_ref[p0 + k] = 2 * t + r0 + k
                return carry
            lax.fori_loop(0, n // SCALAR_UNROLL, fill_group, 0)

            def fill_one(p, carry):
                out_ref[p] = 2 * t + lax.rem(p, tm)
                return carry
            lax.fori_loop(beg, beg + lax.rem(n, SCALAR_UNROLL), fill_one, 0)

    base = i * chunk

    def body(g, carry):
        for k in range(SCALAR_UNROLL):
            j = g * SCALAR_UNROLL + k
            out_ref[d0_ref[0, 0, j]] = base + j
            out_ref[d1_ref[0, 0, j]] = t + base + j
        return carry
    lax.fori_loop(0, chunk // SCALAR_UNROLL, body, 0)


def sorted_assignment_list(dest0, dest1, pad_begin, pad_end, t, tm, rows):
    chunk = min(t, 2048)
    assert t % chunk == 0 and chunk % SCALAR_UNROLL == 0 and tm % SCALAR_UNROLL == 0
    blocked = lambda a: a.reshape(t // chunk, 1, chunk)
    dest_spec = pl.BlockSpec((1, 1, chunk), lambda i, beg, end: (i, 0, 0), memory_space=pltpu.SMEM)
    grid_spec = pltpu.PrefetchScalarGridSpec(
        num_scalar_prefetch=2,
        grid=(t // chunk,),
        in_specs=[dest_spec, dest_spec],
        out_specs=pl.BlockSpec(memory_space=pltpu.SMEM),
    )
    return pl.pallas_call(
        functools.partial(_sorted_list_kernel, chunk=chunk, t=t, tm=tm, rows=rows),
        grid_spec=grid_spec,
        out_shape=jax.ShapeDtypeStruct((rows,), jnp.int32),
        compiler_params=_cparams(("arbitrary",), 16),
        name="sorted_assignment_list",
    )(pad_begin, pad_end, blocked(dest0), blocked(dest1))


def _moe_kernel(te_ref, nu_ref, tokc_ref, tokn_ref, dst_ref, hn_hbm, w1_ref, w3_ref, w2_ref, y_hbm,
                xbuf, xb16, acc_ref, ybuf, gsem, ssem, *, tm, nf, nt, n_real):
    i = pl.program_id(0)
    j = pl.program_id(1)
    nu = nu_ref[0]
    slot = i % 2
    d = acc_ref.shape[1]
    c = _row_tile_chunks(d)

    def row_loop(start_row_copy):
        for r in range(tm):
            start_row_copy(r)

    def row(r):
        return pl.ds(r * c, c) if isinstance(r, int) else pl.ds(pl.multiple_of(r * c, c), c)

    def start_gather(tok_ref, s):
        row_loop(lambda r: pltpu.make_async_copy(
            hn_hbm.at[row(tok_ref[0, 0, r]), :], xbuf.at[s, row(r), :], gsem.at[s]).start())

    def wait_gather(s):
        pltpu.make_async_copy(hn_hbm.at[pl.ds(0, tm * c), :], xbuf.at[s], gsem.at[s]).wait()

    def start_scatter():
        row_loop(lambda r: pltpu.make_async_copy(
            ybuf.at[row(r), :], y_hbm.at[row(dst_ref[0, 0, r]), :], ssem.at[0]).start())

    def spare_rows_copy():
        return pltpu.make_async_copy(ybuf, y_hbm.at[pl.ds(n_real * c, tm * c), :], ssem.at[0])

    def wait_scatter():
        spare_rows_copy().wait()

    @pl.when((j == 0) & (i == 0))
    def _():
        ybuf[...] = jnp.zeros_like(ybuf)
        spare_rows_copy().start()
        spare_rows_copy().wait()

        @pl.when(nu > 0)
        def _():
            start_gather(tokc_ref, 0)

    @pl.when((j == 0) & (((i == 0) & (nu > 0)) | ((i >= 1) & (i - 1 < nu))))
    def _():
        wait_gather(slot)

    @pl.when((j == 0) & (i < nu))
    def _():
        for k, chunk in enumerate(_load_row_tiles(xbuf.at[slot], tm, d)):
            xb16[:, k * V7X_LANES:(k + 1) * V7X_LANES] = chunk.astype(BF16)

    @pl.when(i < nu)
    def _():
        x = xb16[...]
        a = _dot(x, w1_ref[0])
        share = tm // nf
        for k in range(share):
            r = j * share + k
            pltpu.make_async_copy(hn_hbm.at[row(tokn_ref[0, 0, r]), :], xbuf.at[1 - slot, row(r), :],
                                  gsem.at[1 - slot]).start()
        b = _dot(x, w3_ref[0])
        part = _dot((_silu(a) * b).astype(BF16), w2_ref[0])

        @pl.when(j == 0)
        def _():
            acc_ref[...] = part

        @pl.when(j > 0)
        def _():
            acc_ref[...] += part

    last = j == nf - 1

    @pl.when(last & (i >= 1) & (i - 1 < nu))
    def _():
        wait_scatter()

    @pl.when(last & (i < nu))
    def _():
        _store_row_tiles(ybuf, acc_ref[...])
        start_scatter()

    @pl.when(last & (i == nt - 1) & (i < nu))
    def _():
        wait_scatter()
        wait_gather(1 - slot)


def moe_experts(hn, slab, counts, w1, w3, w2, tm, tf):
    t = slab.shape[0]
    d = w1.shape[1]
    c = _row_tile_chunks(d)
    f = w1.shape[2]
    nf = f // tf
    nt = (2 * t) // tm + N_EXPERTS
    rows = nt * tm

    e0 = slab[:, COL_E0].astype(jnp.int32)
    e1 = slab[:, COL_E1].astype(jnp.int32)
    r0 = slab[:, COL_R0].astype(jnp.int32)
    r1 = slab[:, COL_R1].astype(jnp.int32)
    cnt = counts[0, :N_EXPERTS].astype(jnp.int32)
    tiles = (cnt + tm - 1) // tm
    tile_end = jnp.cumsum(tiles)
    tile_start = tile_end - tiles
    nu = tile_end[-1:]
    offs = tile_start * tm
    dest0 = offs[e0] + r0
    dest1 = offs[e1] + r1
    tile_expert = jnp.minimum(
        jnp.sum(jnp.arange(nt, dtype=jnp.int32)[:, None] >= tile_end[None, :], axis=1), N_EXPERTS - 1
    ).astype(jnp.int32)
    pad_begin = jnp.concatenate([offs + cnt, nu * tm]).astype(jnp.int32)
    pad_end = jnp.concatenate([tile_end * tm, jnp.full((1,), rows, jnp.int32)]).astype(jnp.int32)
    dst_sorted = sorted_assignment_list(dest0, dest1, pad_begin, pad_end, t, tm, rows)
    tok_sorted = jnp.where(dst_sorted >= 2 * t, 0, jnp.where(dst_sorted >= t, dst_sorted - t, dst_sorted))
    tok3 = tok_sorted.reshape(nt, 1, tm)
    dst3 = dst_sorted.reshape(nt, 1, tm)

    def w_in(shape, which):
        def index_map(i, j, te, nu_):
            ii = jnp.minimum(i, nu_[0] - 1)
            jj = jnp.where(i < nu_[0], j, nf - 1)
            return (te[ii], 0, jj) if which == "up" else (te[ii], jj, 0)
        return pl.BlockSpec(shape, index_map)

    smem_tile = lambda f_: pl.BlockSpec((1, 1, tm), f_, memory_space=pltpu.SMEM)
    grid_spec = pltpu.PrefetchScalarGridSpec(
        num_scalar_prefetch=2,
        grid=(nt, nf),
        in_specs=[smem_tile(lambda i, j, te, nu_: (i, 0, 0)),
                  smem_tile(lambda i, j, te, nu_: (jnp.minimum(i + 1, nt - 1), 0, 0)),
                  smem_tile(lambda i, j, te, nu_: (i, 0, 0)),
                  pl.BlockSpec(memory_space=pl.ANY),
                  w_in((1, d, tf), "up"), w_in((1, d, tf), "up"), w_in((1, tf, d), "down")],
        out_specs=pl.BlockSpec(memory_space=pl.ANY),
        scratch_shapes=[pltpu.VMEM((2, tm * c, V7X_LANES), F32),
                        pltpu.VMEM((tm, d), BF16),
                        pltpu.VMEM((tm, d), F32),
                        pltpu.VMEM((tm * c, V7X_LANES), F32),
                        pltpu.SemaphoreType.DMA((2,)),
                        pltpu.SemaphoreType.DMA((1,))],
    )
    return pl.pallas_call(
        functools.partial(_moe_kernel, tm=tm, nf=nf, nt=nt, n_real=2 * t),
        grid_spec=grid_spec,
        out_shape=jax.ShapeDtypeStruct(((2 * t + tm) * c, V7X_LANES), F32),
        compiler_params=_cparams(("arbitrary", "arbitrary"), 56),
        name="moe_experts",
    )(tile_expert, nu, tok3, tok3, dst3, hn, w1, w3, w2)


def _combine_kernel(h_ref, y0_ref, y1_ref, slab_ref, g_ref, out_ref, *, final):
    slab = slab_ref[...]
    tm, d = h_ref.shape
    y0 = jnp.concatenate(_load_row_tiles(y0_ref, tm, d), axis=1)
    y1 = jnp.concatenate(_load_row_tiles(y1_ref, tm, d), axis=1)
    x = h_ref[...] + slab[:, COL_W0:COL_W0 + 1] * y0 + slab[:, COL_W1:COL_W1 + 1] * y1
    out_ref[...] = _rmsnorm_f32(x, g_ref[...]) if final else x


def moe_combine(h, y, slab, g, tm, final):
    t, d = h.shape
    nb = t // tm
    c = _row_tile_chunks(d)
    return pl.pallas_call(
        functools.partial(_combine_kernel, final=final),
        grid=(nb,),
        in_specs=[pl.BlockSpec((tm, d), lambda i: (i, 0)),
                  pl.BlockSpec((tm * c, V7X_LANES), lambda i: (i, 0)),
                  pl.BlockSpec((tm * c, V7X_LANES), lambda i: (i + nb, 0)),
                  pl.BlockSpec((tm, V7X_LANES), lambda i: (i, 0)),
                  pl.BlockSpec((1, d), lambda i: (0, 0))],
        out_specs=pl.BlockSpec((tm, d), lambda i: (i, 0)),
        out_shape=jax.ShapeDtypeStruct((t, d), F32),
        compiler_params=_cparams(("arbitrary",), 40),
        name="moe_combine",
    )(h, y, y, slab, g.reshape(1, d))


def _norm_kernel(h_ref, g_ref, out_ref):
    out_ref[...] = _rmsnorm_f32(h_ref[...], g_ref[...])


def final_norm_only(h, g, tm):
    t, d = h.shape
    return pl.pallas_call(
        _norm_kernel,
        grid=(t // tm,),
        in_specs=[pl.BlockSpec((tm, d), lambda i: (i, 0)), pl.BlockSpec((1, d), lambda i: (0, 0))],
        out_specs=pl.BlockSpec((tm, d), lambda i: (i, 0)),
        out_shape=jax.ShapeDtypeStruct((t, d), F32),
        compiler_params=_cparams(("arbitrary",), 40),
        name="final_norm",
    )(h, g.reshape(1, d))


V7X_MXU_DIM = 256


def _tile_sizes(t, f_expert):
    def rows(want):
        tm = min(want, t)
        assert t % tm == 0
        return tm
    half = f_expert // 2
    return dict(qkv=rows(1024), tokens=rows(512), ffn_chunk=1024,
                expert_chunk=half if half % V7X_MXU_DIM == 0 else f_expert)


def kernel(x, w_qkv, w_o, mixer_norm, ffn_norm, rel_bias, w1, w3, w2, router, e_w1, e_w3, e_w2, final_norm):
    b, s, d = x.shape
    assert d == N_HEADS * HEAD_DIM and s % MOBA_BLOCK == 0
    t = b * s
    depth = w_qkv.shape[0]
    h = x.reshape(t, d)
    tiles = _tile_sizes(t, e_w1.shape[-1])
    tm_big, tm_mid = tiles["qkv"], tiles["tokens"]
    normed = False
    for i in range(depth):
        qkv = qkv_proj(h, mixer_norm[i], w_qkv[i].astype(BF16), tm_big)
        if i % 2 == 0:
            o = stick_breaking_attention(qkv, b, s, MOBA_BLOCK)
        else:
            o = moba_attention(qkv, rel_bias, b, s)
        o = o.reshape(t, d)
        wo = w_o[i].astype(BF16)
        jj = i // 2
        if i % 2 == 0:
            h = oproj_dense_ffn(o, wo, h, ffn_norm[i], w1[jj].astype(BF16), w3[jj].astype(BF16),
                                w2[jj].astype(BF16), tm_mid, tiles["ffn_chunk"])
        else:
            h, hn, slab, counts = oproj_moe_router(o, wo, h, ffn_norm[i], router[jj], tm_mid)
            y = moe_experts(hn, slab, counts, e_w1[jj].astype(BF16), e_w3[jj].astype(BF16),
                            e_w2[jj].astype(BF16), tm_mid, tiles["expert_chunk"])
            last = i == depth - 1
            h = moe_combine(h, y, slab, final_norm if last else ffn_norm[i], tm_mid, last)
            normed = last
    if not normed:
        h = final_norm_only(h, final_norm, tm_mid)
    return h.reshape(b, s, d)
```

```python
import functools
import math

import jax
import jax.numpy as jnp
from jax import lax
from jax.experimental import pallas as pl
from jax.experimental.pallas import tpu as pltpu

N_HEADS = 16
HEAD_DIM = 64
MOBA_BLOCK = 256
MOBA_TOPK = 3
N_BUCKETS = 32
MAX_DISTANCE = 128
N_EXPERTS = 8
RMS_EPS = 1e-6
NEG_BIG = -1e30

V7X_LANES = 128
ROW_DMA_PRIORITY = 1
SB_GROUP = 4
MOBA_GROUP = 4
ROUTER_ROWS = 128
V7X_VMEM_BYTES = 64 * 1024 * 1024
HEADS_PER_LANE_BLOCK = V7X_LANES // HEAD_DIM

F32 = jnp.float32
BF16 = jnp.bfloat16


def _cparams(semantics, vmem_mb):
    assert vmem_mb * 1024 * 1024 < V7X_VMEM_BYTES
    return pltpu.CompilerParams(dimension_semantics=semantics,
                                vmem_limit_bytes=vmem_mb * 1024 * 1024)


def _rmsnorm_f32(x, g):
    return x * lax.rsqrt(jnp.mean(x * x, axis=-1, keepdims=True) + RMS_EPS) * g


def _silu(a):
    return a * (1.0 / (1.0 + jnp.exp(-a)))


def _dot(a, b):
    return jnp.dot(a, b, preferred_element_type=F32)


def _software_pipeline(tasks, stages):
    state = [dict() for _ in tasks]
    for step in range(len(tasks) + len(stages) - 1):
        for k in reversed(range(len(stages))):
            t = step - k
            if 0 <= t < len(tasks):
                stages[k](state[t], *tasks[t])


def _dot_nt(a, b):
    return lax.dot_general(a, b, (((1,), (1,)), ((), ())), preferred_element_type=F32)


def _qkv_kernel(x_ref, g_ref, w_ref, o_ref, hn_ref):
    j = pl.program_id(1)

    @pl.when(j == 0)
    def _():
        hn_ref[...] = _rmsnorm_f32(x_ref[...], g_ref[...]).astype(BF16)

    scale = jnp.where(j == 0, HEAD_DIM ** -0.5, 1.0)
    o_ref[0] = (_dot(hn_ref[...], w_ref[...]) * scale).astype(o_ref.dtype)


def qkv_proj(h, g, w_bf16, tm):
    t, d = h.shape
    return pl.pallas_call(
        _qkv_kernel,
        grid=(t // tm, 3),
        in_specs=[pl.BlockSpec((tm, d), lambda i, j: (i, 0)),
                  pl.BlockSpec((1, d), lambda i, j: (0, 0)),
                  pl.BlockSpec((d, d), lambda i, j: (0, j))],
        out_specs=pl.BlockSpec((1, tm, d), lambda i, j: (j, i, 0)),
        out_shape=jax.ShapeDtypeStruct((3, t, d), BF16),
        scratch_shapes=[pltpu.VMEM((tm, d), BF16)],
        compiler_params=_cparams(("arbitrary", "arbitrary"), 40),
        name="qkv_proj",
    )(h, g.reshape(1, d), w_bf16)


def _sb_kernel(q_ref, k_ref, v_ref, o_ref, acc0_ref, acc1_ref, r0_ref, r1_ref, u_ref, *, tq):
    row = lax.broadcasted_iota(jnp.int32, (tq, tq), 0)
    col = lax.broadcasted_iota(jnp.int32, (tq, tq), 1)
    u_ref[...] = -(row > col).astype(BF16)

    def q_tile(qi, carry):
        _sb_tile(qi, q_ref, k_ref, v_ref, o_ref, acc0_ref, acc1_ref, r0_ref, r1_ref, u_ref, tq=tq)
        return carry

    lax.fori_loop(0, q_ref.shape[2] // tq, q_tile, 0)


def _sb_tile(qi, q_ref, k_ref, v_ref, o_ref, acc0_ref, acc1_ref, r0_ref, r1_ref, u_ref, *, tq):
    acc_ref = (acc0_ref, acc1_ref)
    r_ref = (r0_ref, r1_ref)
    row = lax.broadcasted_iota(jnp.int32, (tq, tq), 0)
    col = lax.broadcasted_iota(jnp.int32, (tq, tq), 1)
    tile_rows = pl.ds(pl.multiple_of(qi * tq, tq), tq)
    q = q_ref[0, 0, tile_rows, :]
    lane = lax.broadcasted_iota(jnp.int32, (tq, V7X_LANES), 1)
    past = col < row
    heads = range(HEADS_PER_LANE_BLOCK)
    qms = [jnp.where(lane // HEAD_DIM == h, q, jnp.zeros_like(q)) for h in heads]

    def span(first, nb, diagonal):
        tasks = [(i, h) for i in reversed(range(nb)) for h in heads]
        state = [dict() for _ in tasks]
        r = [r_ref[h][...] for h in heads]
        acc = [None for h in heads]

        def stage(k, st, i, h):
            on_diagonal = diagonal and i == nb - 1
            start = pl.multiple_of((first + i) * tq, tq)
            if k == 0:
                st["z"] = _dot_nt(qms[h], k_ref[0, 0, pl.ds(start, tq), :])
            elif k == 1:
                z = st.pop("z")
                zb = z.astype(BF16)
                t = jnp.log(1 + jnp.exp(-jnp.abs(zb)))
                sp = jnp.maximum(zb, 0) + t
                if on_diagonal:
                    sp = jnp.where(past, sp, jnp.zeros_like(sp))
                st["sp"] = sp
                st["log_sig"] = jnp.minimum(z, 0.0) - t.astype(F32)
            elif k == 2:
                st["after"] = _dot(st["sp"], u_ref[...])
            elif k == 3:
                tot = st.pop("after") + r[h]
                w = jnp.exp(st.pop("log_sig") + tot)
                if on_diagonal:
                    w = jnp.where(past, w, 0.0)
                r[h] = tot[:, 0:1] - st.pop("sp")[:, 0:1].astype(F32)
                st["w"] = w.astype(BF16)
            else:
                part = _dot(st.pop("w"), v_ref[0, 0, pl.ds(start, tq), :])
                acc[h] = part if acc[h] is None else acc[h] + part

        n_stages = 5
        for step in range(len(tasks) + n_stages - 1):
            for k in reversed(range(n_stages)):
                ti = step - k
                if 0 <= ti < len(tasks):
                    stage(k, state[ti], *tasks[ti])
        for h in heads:
            r_ref[h][...] = r[h]
            acc_ref[h][...] += acc[h]

    for h in heads:
        r_ref[h][...] = jnp.zeros((tq, 1), F32)
        acc_ref[h][...] = jnp.zeros((tq, V7X_LANES), F32)

    n_blocks = qi + 1
    n_full = jnp.maximum(n_blocks // SB_GROUP - 1, 0)
    diag_size = n_blocks - n_full * SB_GROUP
    for size in range(1, 2 * SB_GROUP):
        @pl.when(diag_size == size)
        def _(size=size):
            span(n_blocks - size, size, True)

    def body(s, carry):
        span((n_full - 1 - s) * SB_GROUP, SB_GROUP, False)
        return carry

    lax.fori_loop(0, n_full, body, 0)

    o_ref[0, tile_rows, :] = jnp.where(lane // HEAD_DIM == 0, acc0_ref[...], acc1_ref[...]).astype(o_ref.dtype)


def stick_breaking_attention(qkv, b, s, tq):
    d = qkv.shape[-1]
    qkv4 = qkv.reshape(3, b, s, d)
    plane = lambda p: pl.BlockSpec((1, 1, s, V7X_LANES), lambda bi, hp: (p, bi, 0, hp))
    return pl.pallas_call(
        functools.partial(_sb_kernel, tq=tq),
        grid=(b, d // V7X_LANES),
        in_specs=[plane(0), plane(1), plane(2)],
        out_specs=pl.BlockSpec((1, s, V7X_LANES), lambda bi, hp: (bi, 0, hp)),
        out_shape=jax.ShapeDtypeStruct((b, s, d), BF16),
        scratch_shapes=[pltpu.VMEM((tq, V7X_LANES), F32), pltpu.VMEM((tq, V7X_LANES), F32),
                        pltpu.VMEM((tq, 1), F32), pltpu.VMEM((tq, 1), F32),
                        pltpu.VMEM((tq, tq), BF16)],
        compiler_params=_cparams(("arbitrary", "arbitrary"), 32),
        name="stick_breaking",
    )(qkv4, qkv4, qkv4)


def _t5_bias_kernel(rb_ref, o_ref, *, tq):
    h = pl.program_id(0)
    row = lax.broadcasted_iota(jnp.int32, (tq, tq), 0)
    col = lax.broadcasted_iota(jnp.int32, (tq, tq), 1)
    max_exact = N_BUCKETS // 2
    for o in range(2):
        dist = o * tq + row - col
        n = jnp.maximum(dist, 0)
        nf = jnp.maximum(n, 1).astype(F32)
        scaled = jnp.log(nf / max_exact) / math.log(MAX_DISTANCE / max_exact) * (N_BUCKETS - max_exact)
        large = max_exact + jnp.where(scaled < 0, jnp.ceil(scaled), jnp.floor(scaled)).astype(jnp.int32)
        large = jnp.minimum(large, N_BUCKETS - 1)
        bucket = jnp.where(n < max_exact, n, large)
        bias = jnp.zeros((tq, tq), F32)
        for bkt in range(N_BUCKETS):
            bias = jnp.where(bucket == bkt, rb_ref[bkt, h], bias)
        if o == 0:
            bias = jnp.where(dist >= 0, bias, NEG_BIG)
        o_ref[0, o] = bias


def t5_bias_tiles(rel_bias, tq):
    return pl.pallas_call(
        functools.partial(_t5_bias_kernel, tq=tq),
        grid=(N_HEADS,),
        in_specs=[pl.BlockSpec(memory_space=pltpu.SMEM)],
        out_specs=pl.BlockSpec((1, 2, tq, tq), lambda h: (h, 0, 0, 0)),
        out_shape=jax.ShapeDtypeStruct((N_HEADS, 2, tq, tq), F32),
        compiler_params=_cparams(("arbitrary",), 16),
        name="t5_bias_tiles",
    )(rel_bias)


def _moba_aux_lane0(h):
    assert HEADS_PER_LANE_BLOCK == 2
    return ((h + 1) % HEADS_PER_LANE_BLOCK) * HEAD_DIM


def _moba_kernel(rb_ref, qall_ref, k_ref, v_ref, bt_ref, o_ref,
                 s_ref, qaux_ref, mx_ref, l_ref, acc_ref, *, tq, nblk):
    heads = range(HEADS_PER_LANE_BLOCK)
    s_len = nblk * tq
    n_sel = min(MOBA_TOPK, nblk - 1)
    big = jnp.asarray(-NEG_BIG, BF16).astype(F32)
    aux_lane0 = _moba_aux_lane0
    assert nblk + 2 <= HEAD_DIM

    def block_choice():
        km = jnp.concatenate(
            [jnp.mean(k_ref[0, 0, n * tq:(n + 1) * tq, :].astype(F32), axis=0, keepdims=True)
             for n in range(nblk)], axis=0)
        km_hi = km.astype(BF16)
        km_lo = (km - km_hi.astype(F32)).astype(BF16)
        q_all = qall_ref[0, 0]
        lane_all = lax.broadcasted_iota(jnp.int32, (s_len, V7X_LANES), 1)
        blk = lax.broadcasted_iota(jnp.int32, (nblk, s_len), 0)
        own = lax.broadcasted_iota(jnp.int32, (nblk, s_len), 1) // tq
        own_row = own[0:1, :]
        for h in heads:
            qm = jnp.where(lane_all // HEAD_DIM == h, q_all, jnp.zeros_like(q_all))
            gate = _dot_nt(km_hi, qm) + _dot_nt(km_lo, qm)
            gate = jnp.where(blk < own, gate, NEG_BIG)
            rows = [gate[n:n + 1, :] for n in range(nblk)]
            rank = [jnp.zeros((1, s_len), F32) for _ in range(nblk)]
            for n in range(nblk):
                for m in range(n):
                    m_wins = jnp.where(rows[m] >= rows[n], 1.0, 0.0)
                    rank[n] = rank[n] + m_wins
                    rank[m] = rank[m] + (1.0 - m_wins)
            aux = [jnp.where((rank[n] < n_sel) & (n < own_row), 0.0, -big) for n in range(nblk)]
            aux += [jnp.ones((1, s_len), F32)] * 2
            before = jnp.zeros((aux_lane0(h), s_len), F32)
            after = jnp.zeros((V7X_LANES - aux_lane0(h) - len(aux), s_len), F32)
            pieces = ([before] if before.shape[0] else []) + aux + [after]
            qaux_ref[h] = jnp.concatenate(pieces, axis=0).T.astype(BF16)

    block_choice()

    def q_tile(qi, carry):
        _moba_tile(qi, rb_ref, qall_ref, k_ref, v_ref, bt_ref, o_ref,
                   s_ref, qaux_ref, mx_ref, l_ref, acc_ref, tq=tq, nblk=nblk)
        return carry

    lax.fori_loop(0, nblk, q_tile, 0)


def _moba_tile(qi, rb_ref, qall_ref, k_ref, v_ref, bt_ref, o_ref,
               s_ref, qaux_ref, mx_ref, l_ref, acc_ref, *, tq, nblk):
    hp = pl.program_id(1)
    heads = range(HEADS_PER_LANE_BLOCK)
    aux_lane0 = _moba_aux_lane0

    tile_rows = pl.ds(pl.multiple_of(qi * tq, tq), tq)
    q = qall_ref[0, 0, tile_rows, :]
    lane = lax.broadcasted_iota(jnp.int32, (tq, V7X_LANES), 1)
    q_aug = [jnp.where(lane // HEAD_DIM == h, q, qaux_ref[h, tile_rows, :]) for h in heads]
    both = lambda x: jnp.concatenate([x, x], axis=1)
    halves = lambda x: (x[:, :V7X_LANES], x[:, V7X_LANES:])
    for h in heads:
        mx_ref[h] = jnp.full((tq, V7X_LANES), NEG_BIG, F32)

    aux_rows = 16
    lane_aux = lax.broadcasted_iota(jnp.int32, (aux_rows, V7X_LANES), 1)
    key_aux_far = []
    for h in heads:
        far = jnp.full((aux_rows, V7X_LANES), rb_ref[N_BUCKETS - 1, hp * HEADS_PER_LANE_BLOCK + h], F32)
        far_hi = far.astype(BF16).astype(F32)
        far_lo = (far - far_hi).astype(BF16).astype(F32)
        key_aux_far.append(jnp.where(lane_aux == aux_lane0(h) + nblk, far_hi,
                                     jnp.where(lane_aux == aux_lane0(h) + nblk + 1, far_lo, 0.0)))

    def key_operand(h, n, role):
        start = pl.multiple_of(n * tq, tq)
        kb = k_ref[0, 0, pl.ds(start, tq), :]
        if role == "own":
            aux = jnp.zeros((aux_rows, V7X_LANES), F32)
        else:
            base = key_aux_far[h] if role == "far" else jnp.zeros((aux_rows, V7X_LANES), F32)
            aux = jnp.where(lane_aux == aux_lane0(h) + n, 1.0, base)
        aux = jnp.concatenate([aux.astype(BF16)] * (tq // aux_rows), axis=0)
        return jnp.where(lane // HEAD_DIM == h, kb, aux)

    pipeline = _software_pipeline

    def logits_span(first, roles, last=False):
        mx = [mx_ref[h] for h in heads]

        def products(st, i, h):
            st["s"] = _dot_nt(q_aug[h], key_operand(h, first + i, roles[i]))

        def finish(st, i, h):
            start = pl.multiple_of((first + i) * tq, tq)
            s = st.pop("s")
            if roles[i] == "own":
                s = s + bt_ref[h, 0]
            elif roles[i] == "prev":
                s = s + bt_ref[h, 1]
            s_ref[h, :, pl.ds(start, tq)] = s
            s_lo, s_hi = halves(s)
            mx[h] = jnp.maximum(mx[h], jnp.maximum(s_lo, s_hi))

        pipeline([(i, h) for i in range(len(roles)) for h in heads], [products, finish])
        for h in heads:
            if last:
                mx_ref[h] = jnp.broadcast_to(jnp.max(mx[h], axis=1, keepdims=True), (tq, V7X_LANES))
                l_ref[h] = jnp.zeros((tq, V7X_LANES), F32)
                acc_ref[h] = jnp.zeros((tq, V7X_LANES), F32)
            else:
                mx_ref[h] = mx[h]

    def probs_span(first, nb, last=False):
        m = [both(mx_ref[h]) for h in heads]
        l = [l_ref[h] for h in heads]
        acc = [acc_ref[h] for h in heads]

        def exponentials(st, i, h):
            start = pl.multiple_of((first + i) * tq, tq)
            p = jnp.exp(s_ref[h, :, pl.ds(start, tq)] - m[h])
            p_lo, p_hi = halves(p)
            l[h] = l[h] + (p_lo + p_hi)
            st["p"] = p.astype(BF16)

        def accumulate(st, i, h):
            start = pl.multiple_of((first + i) * tq, tq)
            acc[h] = acc[h] + _dot(st.pop("p"), v_ref[0, 0, pl.ds(start, tq), :])

        pipeline([(i, h) for i in range(nb) for h in heads], [exponentials, accumulate])
        if last:
            out = [acc[h] / jnp.sum(l[h], axis=1, keepdims=True) for h in heads]
            o_ref[0, tile_rows, :] = jnp.where(lane // HEAD_DIM == 0, out[0], out[1]).astype(o_ref.dtype)
        else:
            for h in heads:
                l_ref[h] = l[h]
                acc_ref[h] = acc[h]

    group = MOBA_GROUP
    n_blocks = qi + 1
    n_full = jnp.maximum(n_blocks // group - 1, 0)
    last_size = n_blocks - n_full * group

    def far_groups(g, carry):
        logits_span(g * group, ["far"] * group)
        return carry

    lax.fori_loop(0, n_full, far_groups, 0)
    for size in range(1, 2 * group):
        @pl.when(last_size == size)
        def _(size=size):
            logits_span(n_blocks - size, (["far"] * (2 * group) + ["prev", "own"])[-size:]
                        if size >= 2 else ["own"], last=True)

    def prob_groups(g, carry):
        probs_span(g * group, group)
        return carry

    lax.fori_loop(0, n_full, prob_groups, 0)
    for size in range(1, 2 * group):
        @pl.when(last_size == size)
        def _(size=size):
            probs_span(n_blocks - size, size, last=True)


def moba_attention(qkv, rel_bias, b, s):
    d = qkv.shape[-1]
    tq = MOBA_BLOCK
    assert s % tq == 0 and tq >= 2 * MAX_DISTANCE
    nblk = s // tq
    qkv4 = qkv.reshape(3, b, s, d)
    tiles = t5_bias_tiles(rel_bias, tq)
    return pl.pallas_call(
        functools.partial(_moba_kernel, tq=tq, nblk=nblk),
        grid=(b, d // V7X_LANES),
        in_specs=[pl.BlockSpec(memory_space=pltpu.SMEM),
                  pl.BlockSpec((1, 1, s, V7X_LANES), lambda bi, hp: (0, bi, 0, hp)),
                  pl.BlockSpec((1, 1, s, V7X_LANES), lambda bi, hp: (1, bi, 0, hp)),
                  pl.BlockSpec((1, 1, s, V7X_LANES), lambda bi, hp: (2, bi, 0, hp)),
                  pl.BlockSpec((HEADS_PER_LANE_BLOCK, 2, tq, tq), lambda bi, hp: (hp, 0, 0, 0))],
        out_specs=pl.BlockSpec((1, s, V7X_LANES), lambda bi, hp: (bi, 0, hp)),
        out_shape=jax.ShapeDtypeStruct((b, s, d), BF16),
        scratch_shapes=[pltpu.VMEM((HEADS_PER_LANE_BLOCK, tq, s), F32),
                        pltpu.VMEM((HEADS_PER_LANE_BLOCK, s, V7X_LANES), BF16),
                        pltpu.VMEM((HEADS_PER_LANE_BLOCK, tq, V7X_LANES), F32),
                        pltpu.VMEM((HEADS_PER_LANE_BLOCK, tq, V7X_LANES), F32),
                        pltpu.VMEM((HEADS_PER_LANE_BLOCK, tq, V7X_LANES), F32)],
        compiler_params=_cparams(("arbitrary", "arbitrary"), 32),
        name="moba",
    )(rel_bias, qkv4, qkv4, qkv4, tiles)


def _whole(shape):
    return pl.BlockSpec(shape, lambda i: (0,) * len(shape), pipeline_mode=pl.Buffered(1))


def _ffn_kernel(o_ref, wo_ref, h_ref, g_ref, w1_ref, w3_ref, w2_ref, out_ref, *, chunks):
    x = h_ref[...] + _dot(o_ref[...], wo_ref[...])
    hn = _rmsnorm_f32(x, g_ref[...]).astype(BF16)
    acc = x
    for c0, c1 in chunks:
        a = _dot(hn, w1_ref[:, c0:c1])
        b = _dot(hn, w3_ref[:, c0:c1])
        acc = acc + _dot((_silu(a) * b).astype(BF16), w2_ref[c0:c1, :])
    out_ref[...] = acc


def oproj_dense_ffn(o, w_o, h, g, w1, w3, w2, tm, chunk):
    t, d = h.shape
    f = w1.shape[1]
    chunks = tuple((c, min(c + chunk, f)) for c in range(0, f, chunk))
    return pl.pallas_call(
        functools.partial(_ffn_kernel, chunks=chunks),
        grid=(t // tm,),
        in_specs=[pl.BlockSpec((tm, d), lambda i: (i, 0)),
                  _whole((d, d)),
                  pl.BlockSpec((tm, d), lambda i: (i, 0)),
                  pl.BlockSpec((1, d), lambda i: (0, 0)),
                  _whole((d, f)), _whole((d, f)), _whole((f, d))],
        out_specs=pl.BlockSpec((tm, d), lambda i: (i, 0)),
        out_shape=jax.ShapeDtypeStruct((t, d), F32),
        compiler_params=_cparams(("arbitrary",), 56),
        name="oproj_dense_ffn",
    )(o, w_o, h, g.reshape(1, d), w1, w3, w2)


def _row_tile_chunks(d):
    assert d % V7X_LANES == 0
    return d // V7X_LANES


def _store_row_tiles(ref, x, row0=0):
    n, d = x.shape
    c = _row_tile_chunks(d)
    for k in range(c):
        ref[pl.ds(row0 * c + k, n, stride=c), :] = x[:, k * V7X_LANES:(k + 1) * V7X_LANES]


def _load_row_tiles(ref, n, d):
    c = _row_tile_chunks(d)
    return [ref[pl.ds(k, n, stride=c), :] for k in range(c)]


COL_E0, COL_E1, COL_W0, COL_W1, COL_R0, COL_R1 = range(6)


def _router_kernel(o_ref, wo_ref, h_ref, g_ref, wr_ref, h1_ref, hn_ref, slab_ref, cnt_ref, carry_ref, *, tm):
    i = pl.program_id(0)

    @pl.when(i == 0)
    def _():
        carry_ref[...] = jnp.zeros_like(carry_ref)

    rs = min(ROUTER_ROWS, tm)
    assert tm % rs == 0
    g = g_ref[...]
    w = wr_ref[...]
    w_hi = w.astype(BF16)
    w_lo = (w - w_hi.astype(F32)).astype(BF16)
    lane = lax.broadcasted_iota(jnp.int32, (rs, V7X_LANES), 1)
    earlier = (lax.broadcasted_iota(jnp.int32, (rs, rs), 1)
               < lax.broadcasted_iota(jnp.int32, (rs, rs), 0)).astype(BF16)
    neg_inf = jnp.float32(-jnp.inf)
    count = [carry_ref[0:1, :]]

    def project(st, r):
        rows = pl.ds(r * rs, rs)
        h1 = h_ref[rows, :] + _dot(o_ref[rows, :], wo_ref[...])
        h1_ref[rows, :] = h1
        st["h1"] = h1

    def normalise(st, r):
        hn = _rmsnorm_f32(st.pop("h1"), g)
        _store_row_tiles(hn_ref, hn, r * rs)
        st["hi"] = hn.astype(BF16)
        st["lo"] = (hn - st["hi"].astype(F32)).astype(BF16)

    def logits(st, r):
        hi, lo = st.pop("hi"), st.pop("lo")
        st["logits"] = _dot(hi, w_hi) + _dot(hi, w_lo) + _dot(lo, w_hi)

    def top2(st, r):
        lg = jnp.where(lane < N_EXPERTS, st.pop("logits"), neg_inf)
        rank_of = jnp.zeros((rs, V7X_LANES), F32)
        for s in range(1, N_EXPERTS):
            lower = pltpu.roll(lg, s, axis=1)
            higher = pltpu.roll(lg, V7X_LANES - s, axis=1)
            rank_of = rank_of + jnp.where(lower >= lg, 1.0, 0.0) + jnp.where(higher > lg, 1.0, 0.0)
        first = (rank_of == 0.0) & (lane < N_EXPERTS)
        second = (rank_of == 1.0) & (lane < N_EXPERTS)
        pick = lambda sel, x: jnp.sum(jnp.where(sel, x, 0.0), axis=1, keepdims=True)
        lane_f = lane.astype(F32)
        m0, m1 = pick(first, lg), pick(second, lg)
        i0, i1 = pick(first, lane_f).astype(jnp.int32), pick(second, lane_f).astype(jnp.int32)
        e = jnp.exp(m1 - m0)
        st.update(i0=i0, i1=i1, w0=1.0 / (1.0 + e), w1=e / (1.0 + e),
                  hot=((lane == i0) | (lane == i1)).astype(BF16))

    def rank(st, r):
        hot = st.pop("hot")
        before = _dot(earlier, hot) + count[0]
        i0, i1 = st.pop("i0"), st.pop("i1")
        r0 = jnp.sum(jnp.where(lane == i0, before, 0.0), axis=1, keepdims=True)
        r1 = jnp.sum(jnp.where(lane == i1, before, 0.0), axis=1, keepdims=True)
        count[0] = count[0] + jnp.sum(hot.astype(F32), axis=0, keepdims=True)
        slab = jnp.zeros((rs, V7X_LANES), F32)
        for c, val in ((COL_E0, i0.astype(F32)), (COL_E1, i1.astype(F32)), (COL_W0, st.pop("w0")),
                       (COL_W1, st.pop("w1")), (COL_R0, r0), (COL_R1, r1)):
            slab = jnp.where(lane == c, val, slab)
        slab_ref[pl.ds(r * rs, rs), :] = slab

    _software_pipeline([(r,) for r in range(tm // rs)], [project, normalise, logits, top2, rank])
    carry_ref[0:1, :] = count[0]
    cnt_ref[...] = carry_ref[...]


def oproj_moe_router(o, w_o, h, g, w_router, tm):
    t, d = h.shape
    wr = jnp.zeros((d, V7X_LANES), F32).at[:, :N_EXPERTS].set(w_router)
    return pl.pallas_call(
        functools.partial(_router_kernel, tm=tm),
        grid=(t // tm,),
        in_specs=[pl.BlockSpec((tm, d), lambda i: (i, 0)),
                  _whole((d, d)),
                  pl.BlockSpec((tm, d), lambda i: (i, 0)),
                  pl.BlockSpec((1, d), lambda i: (0, 0)),
                  pl.BlockSpec((d, V7X_LANES), lambda i: (0, 0))],
        out_specs=[pl.BlockSpec((tm, d), lambda i: (i, 0)),
                   pl.BlockSpec((tm * d // V7X_LANES, V7X_LANES), lambda i: (i, 0)),
                   pl.BlockSpec((tm, V7X_LANES), lambda i: (i, 0)),
                   pl.BlockSpec((8, V7X_LANES), lambda i: (0, 0))],
        out_shape=[jax.ShapeDtypeStruct((t, d), F32),
                   jax.ShapeDtypeStruct((t * d // V7X_LANES, V7X_LANES), F32),
                   jax.ShapeDtypeStruct((t, V7X_LANES), F32),
                   jax.ShapeDtypeStruct((8, V7X_LANES), F32)],
        scratch_shapes=[pltpu.VMEM((8, V7X_LANES), F32)],
        compiler_params=_cparams(("arbitrary",), 40),
        name="oproj_moe_router",
    )(o, w_o, h, g.reshape(1, d), wr)


SCALAR_UNROLL = 16


def _sorted_list_kernel(beg_ref, end_ref, d0_ref, d1_ref, out_ref, *, chunk, t, tm, rows):
    i = pl.program_id(0)

    @pl.when(i == 0)
    def _():
        for e in range(N_EXPERTS + 1):
            beg, end = beg_ref[e], end_ref[e]
            n = end - beg

            def fill_group(g, carry, end=end):
                p0 = end - SCALAR_UNROLL * (g + 1)
                r0 = lax.rem(p0, tm)
                for k in range(SCALAR_UNROLL):
                    out_ref[p0 + k] = 2 * t + r0 + k
                return carry
            lax.fori_loop(0, n // SCALAR_UNROLL, fill_group, 0)

            def fill_one(p, carry):
                out_ref[p] = 2 * t + lax.rem(p, tm)
                return carry
            lax.fori_loop(beg, beg + lax.rem(n, SCALAR_UNROLL), fill_one, 0)

    base = i * chunk

    def body(g, carry):
        for k in range(SCALAR_UNROLL):
            j = g * SCALAR_UNROLL + k
            out_ref[d0_ref[0, 0, j]] = base + j
            out_ref[d1_ref[0, 0, j]] = t + base + j
        return carry
    lax.fori_loop(0, chunk // SCALAR_UNROLL, body, 0)


def sorted_assignment_list(dest0, dest1, pad_begin, pad_end, t, tm, rows):
    chunk = min(t, 2048)
    assert t % chunk == 0 and chunk % SCALAR_UNROLL == 0 and tm % SCALAR_UNROLL == 0
    blocked = lambda a: a.reshape(t // chunk, 1, chunk)
    dest_spec = pl.BlockSpec((1, 1, chunk), lambda i, beg, end: (i, 0, 0), memory_space=pltpu.SMEM)
    grid_spec = pltpu.PrefetchScalarGridSpec(
        num_scalar_prefetch=2,
        grid=(t // chunk,),
        in_specs=[dest_spec, dest_spec],
        out_specs=pl.BlockSpec(memory_space=pltpu.SMEM),
    )
    return pl.pallas_call(
        functools.partial(_sorted_list_kernel, chunk=chunk, t=t, tm=tm, rows=rows),
        grid_spec=grid_spec,
        out_shape=jax.ShapeDtypeStruct((rows,), jnp.int32),
        compiler_params=_cparams(("arbitrary",), 16),
        name="sorted_assignment_list",
    )(pad_begin, pad_end, blocked(dest0), blocked(dest1))


def _moe_kernel(te_ref, nu_ref, tokc_ref, tokn_ref, dst_ref, hn_hbm, w1_ref, w3_ref, w2_ref, y_hbm,
                xbuf, xb16, acc_ref, ybuf, gsem, ssem, *, tm, nf, nt, n_real):
    i = pl.program_id(0)
    j = pl.program_id(1)
    nu = nu_ref[0]
    slot = i % 2
    d = acc_ref.shape[1]
    c = _row_tile_chunks(d)

    def row_loop(start_row_copy):
        for r in range(tm):
            start_row_copy(r)

    def row(r):
        return pl.ds(r * c, c) if isinstance(r, int) else pl.ds(pl.multiple_of(r * c, c), c)

    def start_gather(tok_ref, s):
        row_loop(lambda r: pltpu.make_async_copy(
            hn_hbm.at[row(tok_ref[0, 0, r]), :], xbuf.at[s, row(r), :], gsem.at[s]).start(priority=ROW_DMA_PRIORITY))

    def wait_gather(s):
        pltpu.make_async_copy(hn_hbm.at[pl.ds(0, tm * c), :], xbuf.at[s], gsem.at[s]).wait()

    def start_scatter():
        row_loop(lambda r: pltpu.make_async_copy(
            ybuf.at[row(r), :], y_hbm.at[row(dst_ref[0, 0, r]), :], ssem.at[0]).start(priority=ROW_DMA_PRIORITY))

    def spare_rows_copy():
        return pltpu.make_async_copy(ybuf, y_hbm.at[pl.ds(n_real * c, tm * c), :], ssem.at[0])

    def wait_scatter():
        spare_rows_copy().wait()

    @pl.when((j == 0) & (i == 0))
    def _():
        ybuf[...] = jnp.zeros_like(ybuf)
        spare_rows_copy().start()
        spare_rows_copy().wait()

        @pl.when(nu > 0)
        def _():
            start_gather(tokc_ref, 0)

    @pl.when((j == 0) & (((i == 0) & (nu > 0)) | ((i >= 1) & (i - 1 < nu))))
    def _():
        wait_gather(slot)

    @pl.when((j == 0) & (i < nu))
    def _():
        for k, chunk in enumerate(_load_row_tiles(xbuf.at[slot], tm, d)):
            xb16[:, k * V7X_LANES:(k + 1) * V7X_LANES] = chunk.astype(BF16)

    @pl.when(i < nu)
    def _():
        x = xb16[...]
        a = _dot(x, w1_ref[0])
        share = tm // nf
        for k in range(share):
            r = j * share + k
            pltpu.make_async_copy(hn_hbm.at[row(tokn_ref[0, 0, r]), :], xbuf.at[1 - slot, row(r), :],
                                  gsem.at[1 - slot]).start(priority=ROW_DMA_PRIORITY)
        b = _dot(x, w3_ref[0])
        part = _dot((_silu(a) * b).astype(BF16), w2_ref[0])

        @pl.when(j == 0)
        def _():
            acc_ref[...] = part

        @pl.when(j > 0)
        def _():
            acc_ref[...] += part

    last = j == nf - 1

    @pl.when(last & (i >= 1) & (i - 1 < nu))
    def _():
        wait_scatter()

    @pl.when(last & (i < nu))
    def _():
        _store_row_tiles(ybuf, acc_ref[...])
        start_scatter()

    @pl.when(last & (i == nt - 1) & (i < nu))
    def _():
        wait_scatter()
        wait_gather(1 - slot)


def moe_experts(hn, slab, counts, w1, w3, w2, tm, tf):
    t = slab.shape[0]
    d = w1.shape[1]
    c = _row_tile_chunks(d)
    f = w1.shape[2]
    nf = f // tf
    nt = (2 * t) // tm + N_EXPERTS
    rows = nt * tm

    e0 = slab[:, COL_E0].astype(jnp.int32)
    e1 = slab[:, COL_E1].astype(jnp.int32)
    r0 = slab[:, COL_R0].astype(jnp.int32)
    r1 = slab[:, COL_R1].astype(jnp.int32)
    cnt = counts[0, :N_EXPERTS].astype(jnp.int32)
    tiles = (cnt + tm - 1) // tm
    tile_end = jnp.cumsum(tiles)
    tile_start = tile_end - tiles
    nu = tile_end[-1:]
    offs = tile_start * tm
    dest0 = offs[e0] + r0
    dest1 = offs[e1] + r1
    tile_expert = jnp.minimum(
        jnp.sum(jnp.arange(nt, dtype=jnp.int32)[:, None] >= tile_end[None, :], axis=1), N_EXPERTS - 1
    ).astype(jnp.int32)
    pad_begin = jnp.concatenate([offs + cnt, nu * tm]).astype(jnp.int32)
    pad_end = jnp.concatenate([tile_end * tm, jnp.full((1,), rows, jnp.int32)]).astype(jnp.int32)
    dst_sorted = sorted_assignment_list(dest0, dest1, pad_begin, pad_end, t, tm, rows)
    tok_sorted = jnp.where(dst_sorted >= 2 * t, 0, jnp.where(dst_sorted >= t, dst_sorted - t, dst_sorted))
    tok3 = tok_sorted.reshape(nt, 1, tm)
    dst3 = dst_sorted.reshape(nt, 1, tm)

    def w_in(shape, which):
        def index_map(i, j, te, nu_):
            ii = jnp.minimum(i, nu_[0] - 1)
            jj = jnp.where(i < nu_[0], j, nf - 1)
            return (te[ii], 0, jj) if which == "up" else (te[ii], jj, 0)
        return pl.BlockSpec(shape, index_map)

    smem_tile = lambda f_: pl.BlockSpec((1, 1, tm), f_, memory_space=pltpu.SMEM)
    grid_spec = pltpu.PrefetchScalarGridSpec(
        num_scalar_prefetch=2,
        grid=(nt, nf),
        in_specs=[smem_tile(lambda i, j, te, nu_: (i, 0, 0)),
                  smem_tile(lambda i, j, te, nu_: (jnp.minimum(i + 1, nt - 1), 0, 0)),
                  smem_tile(lambda i, j, te, nu_: (i, 0, 0)),
                  pl.BlockSpec(memory_space=pl.ANY),
                  w_in((1, d, tf), "up"), w_in((1, d, tf), "up"), w_in((1, tf, d), "down")],
        out_specs=pl.BlockSpec(memory_space=pl.ANY),
        scratch_shapes=[pltpu.VMEM((2, tm * c, V7X_LANES), F32),
                        pltpu.VMEM((tm, d), BF16),
                        pltpu.VMEM((tm, d), F32),
                        pltpu.VMEM((tm * c, V7X_LANES), F32),
                        pltpu.SemaphoreType.DMA((2,)),
                        pltpu.SemaphoreType.DMA((1,))],
    )
    return pl.pallas_call(
        functools.partial(_moe_kernel, tm=tm, nf=nf, nt=nt, n_real=2 * t),
        grid_spec=grid_spec,
        out_shape=jax.ShapeDtypeStruct(((2 * t + tm) * c, V7X_LANES), F32),
        compiler_params=_cparams(("arbitrary", "arbitrary"), 56),
        name="moe_experts",
    )(tile_expert, nu, tok3, tok3, dst3, hn, w1, w3, w2)


def _combine_kernel(h_ref, y0_ref, y1_ref, slab_ref, g_ref, out_ref, *, final):
    slab = slab_ref[...]
    tm, d = h_ref.shape
    y0 = jnp.concatenate(_load_row_tiles(y0_ref, tm, d), axis=1)
    y1 = jnp.concatenate(_load_row_tiles(y1_ref, tm, d), axis=1)
    x = h_ref[...] + slab[:, COL_W0:COL_W0 + 1] * y0 + slab[:, COL_W1:COL_W1 + 1] * y1
    out_ref[...] = _rmsnorm_f32(x, g_ref[...]) if final else x


def moe_combine(h, y, slab, g, tm, final):
    t, d = h.shape
    nb = t // tm
    c = _row_tile_chunks(d)
    return pl.pallas_call(
        functools.partial(_combine_kernel, final=final),
        grid=(nb,),
        in_specs=[pl.BlockSpec((tm, d), lambda i: (i, 0)),
                  pl.BlockSpec((tm * c, V7X_LANES), lambda i: (i, 0)),
                  pl.BlockSpec((tm * c, V7X_LANES), lambda i: (i + nb, 0)),
                  pl.BlockSpec((tm, V7X_LANES), lambda i: (i, 0)),
                  pl.BlockSpec((1, d), lambda i: (0, 0))],
        out_specs=pl.BlockSpec((tm, d), lambda i: (i, 0)),
        out_shape=jax.ShapeDtypeStruct((t, d), F32),
        compiler_params=_cparams(("arbitrary",), 40),
        name="moe_combine",
    )(h, y, y, slab, g.reshape(1, d))


def _norm_kernel(h_ref, g_ref, out_ref):
    out_ref[...] = _rmsnorm_f32(h_ref[...], g_ref[...])


def final_norm_only(h, g, tm):
    t, d = h.shape
    return pl.pallas_call(
        _norm_kernel,
        grid=(t // tm,),
        in_specs=[pl.BlockSpec((tm, d), lambda i: (i, 0)), pl.BlockSpec((1, d), lambda i: (0, 0))],
        out_specs=pl.BlockSpec((tm, d), lambda i: (i, 0)),
        out_shape=jax.ShapeDtypeStruct((t, d), F32),
        compiler_params=_cparams(("arbitrary",), 40),
        name="final_norm",
    )(h, g.reshape(1, d))


V7X_MXU_DIM = 256


def _tile_sizes(t, f_expert):
    def rows(want):
        tm = min(want, t)
        assert t % tm == 0
        return tm
    half = f_expert // 2
    return dict(qkv=rows(1024), tokens=rows(512), ffn_chunk=1024,
                expert_chunk=half if half % V7X_MXU_DIM == 0 else f_expert)


def kernel(x, w_qkv, w_o, mixer_norm, ffn_norm, rel_bias, w1, w3, w2, router, e_w1, e_w3, e_w2, final_norm):
    b, s, d = x.shape
    assert d == N_HEADS * HEAD_DIM and s % MOBA_BLOCK == 0
    t = b * s
    depth = w_qkv.shape[0]
    h = x.reshape(t, d)
    tiles = _tile_sizes(t, e_w1.shape[-1])
    tm_big, tm_mid = tiles["qkv"], tiles["tokens"]
    normed = False
    for i in range(depth):
        qkv = qkv_proj(h, mixer_norm[i], w_qkv[i].astype(BF16), tm_big)
        if i % 2 == 0:
            o = stick_breaking_attention(qkv, b, s, MOBA_BLOCK)
        else:
            o = moba_attention(qkv, rel_bias, b, s)
        o = o.reshape(t, d)
        wo = w_o[i].astype(BF16)
        jj = i // 2
        if i % 2 == 0:
            h = oproj_dense_ffn(o, wo, h, ffn_norm[i], w1[jj].astype(BF16), w3[jj].astype(BF16),
                                w2[jj].astype(BF16), tm_mid, tiles["ffn_chunk"])
        else:
            h, hn, slab, counts = oproj_moe_router(o, wo, h, ffn_norm[i], router[jj], tm_mid)
            y = moe_experts(hn, slab, counts, e_w1[jj].astype(BF16), e_w3[jj].astype(BF16),
                            e_w2[jj].astype(BF16), tm_mid, tiles["expert_chunk"])
            last = i == depth - 1
            h = moe_combine(h, y, slab, final_norm if last else ffn_norm[i], tm_mid, last)
            normed = last
    if not normed:
        h = final_norm_only(h, final_norm, tm_mid)
    return h.reshape(b, s, d)
```
